```python
import math
import jax, jax.numpy as jnp
from jax import lax
import numpy as np

D_MODEL = 1024
BATCH = 2
SEQ = 8192
DEPTH = 2
DEC_BATCH = 16
DEC_SEQ = 4096
PAST_LEN = 128

N_EVEN = (DEPTH + 1) // 2
N_ODD = DEPTH // 2
ROPE_THETA = 500000.0
Q_BLOCK = 128
NORM_EPS = 1e-6
NEG_INF = -1e30

MLA_HEADS = 8
MLA_Q_RANK = 384
MLA_KV_RANK = 256
MLA_NOPE = 64
MLA_ROPE = 32
MLA_V = 64
DIFF_HEADS = 4
DIFF_HEAD_DIM = 64
DIFF_ROT = DIFF_HEAD_DIM // 4
DIFF_QK = DIFF_HEADS * 2 * DIFF_HEAD_DIM
DIFF_VW = DIFF_HEADS * 2 * DIFF_HEAD_DIM
EVEN_IN = MLA_Q_RANK + MLA_KV_RANK + MLA_ROPE + 2 * DIFF_QK + DIFF_VW
EVEN_OUT = MLA_HEADS * MLA_V + DIFF_VW
GRID_W = 64
NA_HEADS = 16
NA_HEAD_DIM = D_MODEL // NA_HEADS
NA_WIDTH = NA_HEADS * NA_HEAD_DIM
WIN_ROWS = 8
WIN_COLS = 16
MEM_TOKENS = 256
XA_HEADS = 4
XA_HEAD_DIM = 128
XA_INNER = XA_HEADS * XA_HEAD_DIM
N_GROUPS = 4
EXPERTS_PER_GROUP = 4
EXPERT_TOP_K = 2
D_EXPERT = 256

kernel_name = 'hybrid_mla_diff_natten_hmoe_encoder'


def _rms_norm(x, g):
    xf = x.astype(jnp.float32)
    y = xf * lax.rsqrt(jnp.mean(xf * xf, axis=-1, keepdims=True) + NORM_EPS)
    return (y * g.astype(jnp.float32)).astype(x.dtype)


def _softmax_f32(s, scale):
    return jax.nn.softmax(s.astype(jnp.float32) * scale, axis=-1)


def _rope_tables(seq_len, rot_dim):
    inv = ROPE_THETA ** (-jnp.arange(0, rot_dim, 2, dtype=jnp.float32) / rot_dim)
    ang = jnp.arange(seq_len, dtype=jnp.float32)[:, None] * inv[None, :]
    return jnp.cos(ang), jnp.sin(ang)


def _apply_rope(x, cos, sin):
    half = x.shape[-1] // 2
    xf = x.astype(jnp.float32)
    x1, x2 = xf[..., :half], xf[..., half:]
    return jnp.concatenate([x1 * cos - x2 * sin, x2 * cos + x1 * sin], axis=-1).astype(x.dtype)


def _map_query_blocks(block_fn, *qs):
    b, s = qs[0].shape[:2]
    nb = s // Q_BLOCK
    blocks = tuple(jnp.moveaxis(q.reshape((b, nb, Q_BLOCK) + q.shape[2:]), 1, 0) for q in qs)
    out = lax.map(lambda xs: block_fn(*xs), blocks)
    out = jnp.moveaxis(out, 0, 1)
    return out.reshape((b, s) + out.shape[3:])


def _even_mixer(h, layer_idx, w_in, q_norm, kv_norm, w_uq, w_ukv,
                lam_q1, lam_k1, lam_q2, lam_k2, subln, w_o):
    b, s, _ = h.shape
    proj = h @ w_in
    splits = np.cumsum([MLA_Q_RANK, MLA_KV_RANK, MLA_ROPE, DIFF_QK, DIFF_QK]).tolist()
    c_q, c_kv, k_pe, dq, dk, dv = jnp.split(proj, splits, axis=-1)

    cos_m, sin_m = _rope_tables(s, MLA_ROPE)
    q = (_rms_norm(c_q, q_norm) @ w_uq).reshape(b, s, MLA_HEADS, MLA_NOPE + MLA_ROPE)
    q_nope = q[..., :MLA_NOPE]
    q_pe = _apply_rope(q[..., MLA_NOPE:], cos_m[None, :, None, :], sin_m[None, :, None, :])
    kv = (_rms_norm(c_kv, kv_norm) @ w_ukv).reshape(b, s, MLA_HEADS, MLA_NOPE + MLA_V)
    k_nope, v_m = kv[..., :MLA_NOPE], kv[..., MLA_NOPE:]
    k_pe = _apply_rope(k_pe, cos_m[None], sin_m[None])
    mla_scale = (MLA_NOPE + MLA_ROPE) ** -0.5

    def mla_block(qn, qp):
        sc = jnp.einsum('bqhd,bkhd->bhqk', qn, k_nope) + jnp.einsum('bqhd,bkd->bhqk', qp, k_pe)
        p = _softmax_f32(sc, mla_scale).astype(v_m.dtype)
        return jnp.einsum('bhqk,bkhd->bqhd', p, v_m)

    o_mla = _map_query_blocks(mla_block, q_nope, q_pe).reshape(b, s, MLA_HEADS * MLA_V)

    cos_d, sin_d = _rope_tables(s, DIFF_ROT)
    cd, sd = cos_d[None, :, None, None, :], sin_d[None, :, None, None, :]

    def partial_rope(x):
        return jnp.concatenate([_apply_rope(x[..., :DIFF_ROT], cd, sd), x[..., DIFF_ROT:]], axis=-1)

    dq = partial_rope(dq.reshape(b, s, DIFF_HEADS, 2, DIFF_HEAD_DIM))
    dk = partial_rope(dk.reshape(b, s, DIFF_HEADS, 2, DIFF_HEAD_DIM))
    dv = dv.reshape(b, s, DIFF_HEADS, 2 * DIFF_HEAD_DIM)
    lam_init = 0.8 - 0.6 * math.exp(-0.3 * layer_idx)
    f32 = jnp.float32
    lam = (jnp.exp(jnp.sum(lam_q1.astype(f32) * lam_k1.astype(f32)))
           - jnp.exp(jnp.sum(lam_q2.astype(f32) * lam_k2.astype(f32))) + lam_init)
    diff_scale = DIFF_HEAD_DIM ** -0.5

    def diff_block(qb):
        sc = jnp.einsum('bqhmd,bkhmd->bhmqk', qb, dk)
        p = _softmax_f32(sc, diff_scale)
        pd = (p[:, :, 0] - lam * p[:, :, 1]).astype(dv.dtype)
        return jnp.einsum('bhqk,bkhd->bqhd', pd, dv)

    o_diff = _map_query_blocks(diff_block, dq)
    o_diff = (_rms_norm(o_diff, subln) * (1.0 - lam_init)).reshape(b, s, DIFF_VW)

    return jnp.concatenate([o_mla, o_diff], axis=-1) @ w_o


def _odd_mixer(h, w_in, rpb, w_o):
    b, s, _ = h.shape
    rows = s // GRID_W
    wr = min(WIN_ROWS, rows)
    qkv = (h @ w_in).reshape(b, rows, GRID_W, 3, NA_HEADS, NA_HEAD_DIM)
    q, k, v = qkv[:, :, :, 0], qkv[:, :, :, 1], qkv[:, :, :, 2]
    col = np.arange(GRID_W)
    col_start = np.clip(col - WIN_COLS // 2, 0, GRID_W - WIN_COLS)
    col_valid = (col[None, :] >= col_start[:, None]) & (col[None, :] < col_start[:, None] + WIN_COLS)
    col_bias_idx = np.clip(col[None, :] - col[:, None], -(WIN_COLS - 1), WIN_COLS - 1) + WIN_COLS - 1
    scale = NA_HEAD_DIM ** -0.5

    def row_block(args):
        r, q_row = args
        r0 = jnp.clip(r - wr // 2, 0, rows - wr)
        k_band = lax.dynamic_slice_in_dim(k, r0, wr, axis=1)
        v_band = lax.dynamic_slice_in_dim(v, r0, wr, axis=1)
        sc = jnp.einsum('bqhd,bichd->bhqic', q_row, k_band).astype(jnp.float32) * scale
        row_idx = r0 + jnp.arange(wr) - r + WIN_ROWS - 1
        bias = rpb[:, row_idx][:, :, col_bias_idx]
        bias = jnp.transpose(bias, (0, 2, 1, 3)).astype(jnp.float32)
        sc = jnp.where(col_valid[None, None, :, None, :], sc + bias[None], NEG_INF)
        p = jax.nn.softmax(sc.reshape(b, NA_HEADS, GRID_W, wr * GRID_W), axis=-1)
        p = p.reshape(sc.shape).astype(v.dtype)
        return jnp.einsum('bhqic,bichd->bqhd', p, v_band)

    out = lax.map(row_block, (jnp.arange(rows), jnp.moveaxis(q, 1, 0)))
    out = jnp.moveaxis(out, 0, 1).reshape(b, s, NA_WIDTH)
    return out @ w_o


def _mem_cross_attention(h, mem_n, w_q, w_kv, w_o):
    b, s, _ = h.shape
    m = mem_n.shape[1]
    q = (h @ w_q).reshape(b, s, XA_HEADS, XA_HEAD_DIM)
    kv = (mem_n @ w_kv).reshape(b, m, 2, XA_HEADS, XA_HEAD_DIM)
    k, v = kv[:, :, 0], kv[:, :, 1]
    sc = jnp.einsum('bqhd,bkhd->bhqk', q, k)
    p = _softmax_f32(sc, XA_HEAD_DIM ** -0.5).astype(v.dtype)
    o = jnp.einsum('bhqk,bkhd->bqhd', p, v).reshape(b, s, XA_INNER)
    return o @ w_o


def _hier_moe(h, w_rg, b_rg, w_re, b_re, w_gate, w_up, w_down):
    b, s, d = h.shape
    t = h.reshape(b * s, d)
    f32 = jnp.float32
    g_prob = jax.nn.softmax((t @ w_rg + b_rg).astype(f32), axis=-1)
    g_p, g_i = lax.top_k(g_prob, 1)
    group_mask = jax.nn.one_hot(g_i[:, 0], N_GROUPS, dtype=f32)
    e_logits = (t @ w_re + b_re).astype(f32).reshape(-1, N_GROUPS, EXPERTS_PER_GROUP)
    e_in_group = jnp.einsum('tg,tge->te', group_mask, e_logits)
    e_v, e_i = lax.top_k(e_in_group, EXPERT_TOP_K)
    e_w = jax.nn.softmax(e_v, axis=-1) * g_p
    within = jnp.sum(jax.nn.one_hot(e_i, EXPERTS_PER_GROUP, dtype=f32) * e_w[..., None], axis=1)
    y = jnp.zeros_like(t)
    for g in range(N_GROUPS):
        gate = (group_mask[:, g:g + 1] * within).astype(t.dtype)
        hg = jnp.einsum('td,edf->tef', t, w_gate[g])
        hu = jnp.einsum('td,edf->tef', t, w_up[g])
        act = jax.nn.silu(hg) * hu * gate[..., None]
        y = y + jnp.einsum('tef,efd->td', act, w_down[g])
    return y.reshape(b, s, d)


def _trunk(x, mem, p):
    for layer in range(DEPTH):
        i = layer // 2
        h = _rms_norm(x, p['ln_mix'][layer])
        if layer % 2 == 0:
            mix = _even_mixer(h, layer, p['w_in_even'][i], p['mla_q_norm'][i], p['mla_kv_norm'][i],
                              p['mla_w_uq'][i], p['mla_w_ukv'][i], p['diff_lambda_q1'][i],
                              p['diff_lambda_k1'][i], p['diff_lambda_q2'][i], p['diff_lambda_k2'][i],
                              p['diff_subln'][i], p['w_o_even'][i])
        else:
            mix = _odd_mixer(h, p['w_in_odd'][i], p['na_rpb'][i], p['w_o_odd'][i])
        x = x + mix
        x = x + _mem_cross_attention(_rms_norm(x, p['ln_xattn'][layer]), _rms_norm(mem, p['ln_mem'][layer]),
                                     p['xa_w_q'][layer], p['xa_w_kv'][layer], p['xa_w_o'][layer])
        x = x + _hier_moe(_rms_norm(x, p['ln_ffn'][layer]), p['moe_w_group'][layer], p['moe_b_group'][layer],
                          p['moe_w_expert'][layer], p['moe_b_expert'][layer], p['moe_w_gate'][layer],
                          p['moe_w_up'][layer], p['moe_w_down'][layer])
    return _rms_norm(x, p['ln_final'])


def setup_inputs(seed: int = 0) -> dict:
    key = jax.random.key(seed)
    keys = jax.random.split(key, 64)
    counter = [0]

    def nxt():
        k = keys[counter[0]]
        counter[0] += 1
        return k

    def nrm(shape, scale):
        return scale * jax.random.normal(nxt(), shape, jnp.float32)

    def gain(shape):
        return 1.0 + 0.02 * jax.random.normal(nxt(), shape, jnp.float32)

    ge = N_GROUPS * EXPERTS_PER_GROUP
    return {
        'x_prompt': nrm((BATCH, SEQ, D_MODEL), 1.0),
        'x_sample': nrm((DEC_BATCH, DEC_SEQ, D_MODEL), 1.0),
        'mem_prompt': nrm((BATCH, MEM_TOKENS, D_MODEL), 1.0),
        'mem_sample': nrm((DEC_BATCH, MEM_TOKENS, D_MODEL), 1.0),
        'ln_mix': gain((DEPTH, D_MODEL)),
        'w_in_even': nrm((N_EVEN, D_MODEL, EVEN_IN), D_MODEL ** -0.5),
        'mla_q_norm': gain((N_EVEN, MLA_Q_RANK)),
        'mla_kv_norm': gain((N_EVEN, MLA_KV_RANK)),
        'mla_w_uq': nrm((N_EVEN, MLA_Q_RANK, MLA_HEADS * (MLA_NOPE + MLA_ROPE)), MLA_Q_RANK ** -0.5),
        'mla_w_ukv': nrm((N_EVEN, MLA_KV_RANK, MLA_HEADS * (MLA_NOPE + MLA_V)), MLA_KV_RANK ** -0.5),
        'diff_lambda_q1': nrm((N_EVEN, DIFF_HEAD_DIM), 0.1),
        'diff_lambda_k1': nrm((N_EVEN, DIFF_HEAD_DIM), 0.1),
        'diff_lambda_q2': nrm((N_EVEN, DIFF_HEAD_DIM), 0.1),
        'diff_lambda_k2': nrm((N_EVEN, DIFF_HEAD_DIM), 0.1),
        'diff_subln': gain((N_EVEN, 2 * DIFF_HEAD_DIM)),
        'w_o_even': nrm((N_EVEN, EVEN_OUT, D_MODEL), EVEN_OUT ** -0.5),
        'w_in_odd': nrm((N_ODD, D_MODEL, 3 * NA_WIDTH), D_MODEL ** -0.5),
        'na_rpb': nrm((N_ODD, NA_HEADS, 2 * WIN_ROWS - 1, 2 * WIN_COLS - 1), 0.02),
        'w_o_odd': nrm((N_ODD, NA_WIDTH, D_MODEL), NA_WIDTH ** -0.5),
        'ln_xattn': gain((DEPTH, D_MODEL)),
        'ln_mem': gain((DEPTH, D_MODEL)),
        'xa_w_q': nrm((DEPTH, D_MODEL, XA_INNER), D_MODEL ** -0.5),
        'xa_w_kv': nrm((DEPTH, D_MODEL, 2 * XA_INNER), D_MODEL ** -0.5),
        'xa_w_o': nrm((DEPTH, XA_INNER, D_MODEL), XA_INNER ** -0.5),
        'ln_ffn': gain((DEPTH, D_MODEL)),
        'moe_w_group': nrm((DEPTH, D_MODEL, N_GROUPS), D_MODEL ** -0.5),
        'moe_b_group': nrm((DEPTH, N_GROUPS), 0.01),
        'moe_w_expert': nrm((DEPTH, D_MODEL, ge), D_MODEL ** -0.5),
        'moe_b_expert': nrm((DEPTH, ge), 0.01),
        'moe_w_gate': nrm((DEPTH, N_GROUPS, EXPERTS_PER_GROUP, D_MODEL, D_EXPERT), D_MODEL ** -0.5),
        'moe_w_up': nrm((DEPTH, N_GROUPS, EXPERTS_PER_GROUP, D_MODEL, D_EXPERT), D_MODEL ** -0.5),
        'moe_w_down': nrm((DEPTH, N_GROUPS, EXPERTS_PER_GROUP, D_EXPERT, D_MODEL), D_EXPERT ** -0.5),
        'ln_final': gain((D_MODEL,)),
    }


def reference(x_prompt, x_sample, mem_prompt, mem_sample, ln_mix, w_in_even, mla_q_norm, mla_kv_norm,
              mla_w_uq, mla_w_ukv, diff_lambda_q1, diff_lambda_k1, diff_lambda_q2, diff_lambda_k2,
              diff_subln, w_o_even, w_in_odd, na_rpb, w_o_odd, ln_xattn, ln_mem, xa_w_q, xa_w_kv, xa_w_o,
              ln_ffn, moe_w_group, moe_b_group, moe_w_expert, moe_b_expert, moe_w_gate, moe_w_up,
              moe_w_down, ln_final):
    params = dict(ln_mix=ln_mix, w_in_even=w_in_even, mla_q_norm=mla_q_norm, mla_kv_norm=mla_kv_norm,
                  mla_w_uq=mla_w_uq, mla_w_ukv=mla_w_ukv, diff_lambda_q1=diff_lambda_q1,
                  diff_lambda_k1=diff_lambda_k1, diff_lambda_q2=diff_lambda_q2, diff_lambda_k2=diff_lambda_k2,
                  diff_subln=diff_subln, w_o_even=w_o_even, w_in_odd=w_in_odd, na_rpb=na_rpb, w_o_odd=w_o_odd,
                  ln_xattn=ln_xattn, ln_mem=ln_mem, xa_w_q=xa_w_q, xa_w_kv=xa_w_kv, xa_w_o=xa_w_o,
                  ln_ffn=ln_ffn, moe_w_group=moe_w_group, moe_b_group=moe_b_group, moe_w_expert=moe_w_expert,
                  moe_b_expert=moe_b_expert, moe_w_gate=moe_w_gate, moe_w_up=moe_w_up, moe_w_down=moe_w_down,
                  ln_final=ln_final)
    y_prompt = _trunk(x_prompt, mem_prompt, params)
    y_sample = _trunk(x_sample, mem_sample, params)
    return (y_prompt, y_sample)
```

```python
import functools
import math

import numpy as np
import jax
import jax.numpy as jnp
from jax import lax
from jax.experimental import pallas as pl
from jax.experimental.pallas import tpu as pltpu

F32 = jnp.float32
BF16 = jnp.bfloat16

D_MODEL = 1024
ROPE_THETA = 500000.0
NORM_EPS = 1e-6
NEG_INF = -1e30
LOG2E = math.log2(math.e)

MLA_HEADS = 8
MLA_Q_RANK = 384
MLA_KV_RANK = 256
MLA_NOPE = 64
MLA_ROPE = 32
MLA_V = 64
DIFF_HEADS = 4
DIFF_HEAD_DIM = 64
DIFF_ROT = DIFF_HEAD_DIM // 4
DIFF_QK = DIFF_HEADS * 2 * DIFF_HEAD_DIM
DIFF_VW = DIFF_HEADS * 2 * DIFF_HEAD_DIM
GRID_W = 64
NA_HEADS = 16
NA_HEAD_DIM = 64
WIN_ROWS = 8
WIN_COLS = 16
MEM_TOKENS = 256
XA_HEADS = 4
XA_HEAD_DIM = 128
XA_INNER = XA_HEADS * XA_HEAD_DIM
N_GROUPS = 4
EXPERTS_PER_GROUP = 4
N_EXPERTS = N_GROUPS * EXPERTS_PER_GROUP
D_EXPERT = 256

LANES = 128
VMEM_LIMIT = 56 * 1024 * 1024

TOKEN_TILE = 512
FLASH_TQ = 256
FLASH_TK = 512
NA_ROW_BLOCK = 8
NA_KEY_ROWS = 2 * WIN_ROWS
NA_TILE_CODES = 3
NA_TILES_PER_HEAD = NA_TILE_CODES * 16 + 1


def _cparams(sem):
    return pltpu.CompilerParams(dimension_semantics=sem, vmem_limit_bytes=VMEM_LIMIT)


def _rms(xf, g):
    return xf * lax.rsqrt(jnp.mean(xf * xf, axis=-1, keepdims=True) + NORM_EPS) * g


def _full(shape):
    nd = len(shape)
    return pl.BlockSpec(shape, lambda *_: (0,) * nd)


def _rope_block(x, cos, sin, half):
    lane = lax.broadcasted_iota(jnp.int32, x.shape, 1)
    up = pltpu.roll(x, LANES - half, 1)
    down = pltpu.roll(x, half, 1)
    first = lax.bitwise_and(lane, 2 * half - 1) < half
    return x * cos + jnp.where(first, up, down) * sin


def _even_in_kernel(x_ref, g_ref, win_ref, qn_ref, kvn_ref, wuq_ref, wuk_ref, wuv_ref,
                    cm_ref, sm_ref, cd_ref, sd_ref,
                    q_ref, k_ref, v_ref, dq_ref, dk_ref, dv_ref):
    h = _rms(x_ref[0], g_ref[...]).astype(BF16)
    proj = jnp.dot(h, win_ref[...], preferred_element_type=F32)
    o_cq, o_ckv, o_dq, o_dk, o_dv, o_kpe = 0, 384, 640, 1152, 1664, 2176
    cm, sm, cd, sd = cm_ref[...], sm_ref[...], cd_ref[...], sd_ref[...]

    cq = _rms(proj[:, o_cq:o_cq + MLA_Q_RANK], qn_ref[...]).astype(BF16)
    q = jnp.dot(cq, wuq_ref[...], preferred_element_type=F32)
    mla_c = (MLA_NOPE + MLA_ROPE) ** -0.5 * LOG2E
    for hd in range(MLA_HEADS):
        sl = slice(hd * LANES, (hd + 1) * LANES)
        q_ref[0, :, sl] = (_rope_block(q[:, sl], cm, sm, MLA_ROPE // 2) * mla_c).astype(BF16)

    ckv = _rms(proj[:, o_ckv:o_ckv + MLA_KV_RANK], kvn_ref[...]).astype(BF16)
    kn = jnp.dot(ckv, wuk_ref[...], preferred_element_type=F32)
    kpe = _rope_block(proj[:, o_kpe:o_kpe + LANES], cm, sm, MLA_ROPE // 2)
    for hd in range(MLA_HEADS):
        sl = slice(hd * LANES, (hd + 1) * LANES)
        k_ref[0, :, sl] = (kn[:, sl] + kpe).astype(BF16)
    v_ref[0] = jnp.dot(ckv, wuv_ref[...], preferred_element_type=F32).astype(BF16)

    diff_c = DIFF_HEAD_DIM ** -0.5 * LOG2E
    for hd in range(DIFF_HEADS):
        sl = slice(hd * LANES, (hd + 1) * LANES)
        dq_ref[0, :, sl] = (_rope_block(proj[:, o_dq + hd * LANES:o_dq + (hd + 1) * LANES], cd, sd,
                                        DIFF_ROT // 2) * diff_c).astype(BF16)
        dk_ref[0, :, sl] = _rope_block(proj[:, o_dk + hd * LANES:o_dk + (hd + 1) * LANES], cd, sd,
                                       DIFF_ROT // 2).astype(BF16)
    dv_ref[0] = proj[:, o_dv:o_dv + DIFF_VW].astype(BF16)


def _even_in(x, g, win, qn, kvn, wuq, wuk, wuv, cm, sm, cd, sd):
    b, s, d = x.shape
    tm = TOKEN_TILE
    tok = lambda w: pl.BlockSpec((1, tm, w), lambda i, j: (i, j, 0))
    tab = pl.BlockSpec((tm, LANES), lambda i, j: (j, 0))
    outs = [jax.ShapeDtypeStruct((b, s, w), BF16) for w in (1024, 1024, 512, 512, 512, 512)]
    return pl.pallas_call(
        _even_in_kernel,
        out_shape=outs,
        grid=(b, s // tm),
        in_specs=[tok(d), _full(g.shape), _full(win.shape), _full(qn.shape), _full(kvn.shape),
                  _full(wuq.shape), _full(wuk.shape), _full(wuv.shape), tab, tab, tab, tab],
        out_specs=[tok(1024), tok(1024), tok(512), tok(512), tok(512), tok(512)],
        compiler_params=_cparams(("parallel", "parallel")),
        name="even_in",
    )(x, g, win, qn, kvn, wuq, wuk, wuv, cm, sm, cd, sd)


def _flash_kernel(*refs, tq, tk, n_k, width, diff, lam_init):
    if diff:
        q_ref, k_ref, v_ref, lamp_ref, g_ref, o_ref, qs_ref, m_ref, l_ref, acc_ref = refs
    else:
        q_ref, k_ref, v_ref, o_ref, qs_ref, m_ref, l_ref, acc_ref = refs
    half = width // 2
    q = q_ref[0]
    lane = lax.broadcasted_iota(jnp.int32, (tq, width), 1)
    zero = jnp.zeros_like(q)
    qs_ref[0:tq, :] = jnp.where(lane < half, q, zero)
    qs_ref[tq:2 * tq, :] = jnp.where(lane >= half, q, zero)
    m_ref[...] = jnp.full(m_ref.shape, NEG_INF, F32)
    l_ref[...] = jnp.zeros(l_ref.shape, F32)
    acc_ref[...] = jnp.zeros(acc_ref.shape, F32)

    def body(i, carry):
        off = pl.multiple_of(i * tk, tk)
        kc = k_ref[0, pl.ds(off, tk), :]
        vc = v_ref[0, pl.ds(off, tk), :]
        s = lax.dot_general(qs_ref[...], kc, (((1,), (1,)), ((), ())), preferred_element_type=F32)
        m_prev = m_ref[...]
        m_new = jnp.maximum(m_prev, jnp.max(s, axis=1, keepdims=True))
        alpha = jnp.exp2(m_prev - m_new)
        p = jnp.exp2(s - m_new)
        l_ref[...] = alpha * l_ref[...] + jnp.sum(p, axis=1, keepdims=True)
        acc_ref[...] = alpha * acc_ref[...] + jnp.dot(p.astype(BF16), vc, preferred_element_type=F32)
        m_ref[...] = m_new
        return carry

    lax.fori_loop(0, n_k, body, 0)

    o = acc_ref[...] / l_ref[...]
    top, bot = o[0:tq], o[tq:2 * tq]
    if diff:
        lp = lamp_ref[...]
        lam = (jnp.exp(jnp.sum(lp[0:1] * lp[1:2], axis=1, keepdims=True))
               - jnp.exp(jnp.sum(lp[2:3] * lp[3:4], axis=1, keepdims=True)) + lam_init)
        od = top - lam * bot
        o_ref[0] = (_rms(od, g_ref[...]) * (1.0 - lam_init)).astype(BF16)
    else:
        lane_o = lax.broadcasted_iota(jnp.int32, (tq, LANES), 1)
        o_ref[0] = jnp.where(lane_o < LANES // 2, top, bot).astype(BF16)


def _flash(q, k, v, width, lamp=None, subln=None, lam_init=0.0):
    b, s, _ = q.shape
    n = q.shape[2] // width
    tq, tk = FLASH_TQ, FLASH_TK
    diff = lamp is not None
    kern = functools.partial(_flash_kernel, tq=tq, tk=tk, n_k=s // tk, width=width, diff=diff,
                             lam_init=lam_init)
    in_specs = [pl.BlockSpec((1, tq, width), lambda bi, p, qi: (bi, qi, p)),
                pl.BlockSpec((1, s, width), lambda bi, p, qi: (bi, 0, p)),
                pl.BlockSpec((1, s, LANES), lambda bi, p, qi: (bi, 0, p))]
    args = [q, k, v]
    if diff:
        in_specs += [_full(lamp.shape), _full(subln.shape)]
        args += [lamp, subln]
    return pl.pallas_call(
        kern,
        out_shape=jax.ShapeDtypeStruct((b, s, n * LANES), BF16),
        grid=(b, n, s // tq),
        in_specs=in_specs,
        out_specs=pl.BlockSpec((1, tq, LANES), lambda bi, p, qi: (bi, qi, p)),
        scratch_shapes=[pltpu.VMEM((2 * tq, width), BF16), pltpu.VMEM((2 * tq, 1), F32),
                        pltpu.VMEM((2 * tq, 1), F32), pltpu.VMEM((2 * tq, LANES), F32)],
        compiler_params=_cparams(("parallel", "parallel", "arbitrary")),
        name="flash_diff" if diff else "flash_mla",
    )(*args)


def _odd_in_kernel(x_ref, g_ref, w_ref, q_ref, k_ref, v_ref):
    h = _rms(x_ref[0], g_ref[...]).astype(BF16)
    qkv = jnp.dot(h, w_ref[...], preferred_element_type=F32)
    c = NA_HEAD_DIM ** -0.5 * LOG2E
    q_ref[0] = (qkv[:, 0:1024] * c).astype(BF16)
    k_ref[0] = qkv[:, 1024:2048].astype(BF16)
    v_ref[0] = qkv[:, 2048:3072].astype(BF16)


def _odd_in(x, g, w):
    b, s, d = x.shape
    tm = TOKEN_TILE
    tok = pl.BlockSpec((1, tm, d), lambda i, j: (i, j, 0))
    return pl.pallas_call(
        _odd_in_kernel,
        out_shape=[jax.ShapeDtypeStruct((b, s, d), BF16)] * 3,
        grid=(b, s // tm),
        in_specs=[tok, _full(g.shape), _full(w.shape)],
        out_specs=[tok, tok, tok],
        compiler_params=_cparams(("parallel", "parallel")),
        name="odd_in",
    )(x, g, w)


def _na_kernel(idx_ref, q_ref, k_ref, v_ref, tbl_ref, o_ref, qs_ref, s_ref, p_ref, linv_ref, *, rows, n_rb):
    nq = NA_ROW_BLOCK * GRID_W
    nk = NA_KEY_ROWS * GRID_W
    rb = pl.program_id(2)
    ks = jnp.clip(NA_ROW_BLOCK * rb - WIN_ROWS // 2, 0, rows - NA_KEY_ROWS)
    pat = jnp.where(rb == 0, 0, jnp.where(rb == n_rb - 1, 2, 1))
    off = pl.multiple_of(ks * GRID_W, (WIN_ROWS // 2) * GRID_W)
    kw = k_ref[0, pl.ds(off, nk), :]
    vw = v_ref[0, pl.ds(off, nk), :]
    q = q_ref[0]
    lane = lax.broadcasted_iota(jnp.int32, (nq, LANES), 1)
    zero = jnp.zeros_like(q)
    qs_ref[0:nq, :] = jnp.where(lane < LANES // 2, q, zero)
    qs_ref[nq:2 * nq, :] = jnp.where(lane >= LANES // 2, q, zero)
    s_ref[...] = lax.dot_general(qs_ref[...], kw, (((1,), (1,)), ((), ())), preferred_element_type=F32)

    def group(g, carry):
        r0 = pl.multiple_of(g * GRID_W, GRID_W)
        hh = lax.shift_right_logical(g, 3)
        i = lax.bitwise_and(g, 7)
        base = pat * (NA_ROW_BLOCK * NA_KEY_ROWS // 2) + i * (NA_KEY_ROWS // 2)
        tiles = [tbl_ref[0, hh * NA_TILES_PER_HEAD + idx_ref[base + kp]] for kp in range(NA_KEY_ROWS // 2)]
        s = s_ref[pl.ds(r0, GRID_W), :] + jnp.concatenate(tiles, axis=1)
        m = jnp.max(s, axis=1, keepdims=True)
        p = jnp.exp2(s - m)
        linv_ref[pl.ds(r0, GRID_W), :] = 1.0 / jnp.sum(p, axis=1, keepdims=True)
        p_ref[pl.ds(r0, GRID_W), :] = p.astype(BF16)
        return carry

    lax.fori_loop(0, 2 * NA_ROW_BLOCK, group, 0)
    o = jnp.dot(p_ref[...], vw, preferred_element_type=F32) * linv_ref[...]
    o_ref[0] = jnp.where(lane < LANES // 2, o[0:nq], o[nq:2 * nq]).astype(BF16)


def _na_tables(rpb):
    col = np.arange(GRID_W)
    col_start = np.clip(col - WIN_COLS // 2, 0, GRID_W - WIN_COLS)
    col_valid = (col[None, :] >= col_start[:, None]) & (col[None, :] < col_start[:, None] + WIN_COLS)
    col_idx = np.clip(col[None, :] - col[:, None], -(WIN_COLS - 1), WIN_COLS - 1) + WIN_COLS - 1
    n_ri = 2 * WIN_ROWS - 1
    ri = np.zeros((NA_TILES_PER_HEAD, LANES), np.int32)
    ok = np.zeros((NA_TILES_PER_HEAD, LANES), bool)
    for code in range(NA_TILE_CODES):
        for a in range(-1, n_ri):
            t = code * 16 + a + 1
            ri[t, :GRID_W] = np.clip(a, 0, n_ri - 1)
            ri[t, GRID_W:] = np.clip(a + 1, 0, n_ri - 1)
            ok[t, :GRID_W] = code in (0, 1) and 0 <= a < n_ri
            ok[t, GRID_W:] = code in (0, 2) and 0 <= a + 1 < n_ri
    cc = np.concatenate([col_idx, col_idx], axis=1)
    cv = np.concatenate([col_valid, col_valid], axis=1)
    g = rpb[:, ri[:, None, :], cc[None, :, :]]
    mask = ok[:, None, :] & cv[None, :, :]
    tbl = jnp.where(mask[None], g * LOG2E, NEG_INF).astype(F32)
    tbl = tbl.reshape(NA_HEADS // 2, 2 * NA_TILES_PER_HEAD, GRID_W, LANES)

    idx = np.zeros((3, NA_ROW_BLOCK, NA_KEY_ROWS // 2), np.int32)
    for pat in range(3):
        for i in range(NA_ROW_BLOCK):
            rel = (i, i + WIN_ROWS // 2, i + WIN_ROWS)[pat]
            start = (max(i - WIN_ROWS // 2, 0), i, min(i + WIN_ROWS // 2, WIN_ROWS))[pat]
            for kp in range(NA_KEY_ROWS // 2):
                kk = 2 * kp
                a = kk - rel + WIN_ROWS - 1
                v0 = start <= kk < start + WIN_ROWS
                v1 = start <= kk + 1 < start + WIN_ROWS
                if v0:
                    assert 0 <= a < n_ri
                if v1:
                    assert 0 <= a + 1 < n_ri
                if v0 and v1:
                    idx[pat, i, kp] = a + 1
                elif v0:
                    idx[pat, i, kp] = 16 + a + 1
                elif v1:
                    idx[pat, i, kp] = 32 + a + 1
                else:
                    idx[pat, i, kp] = NA_TILES_PER_HEAD - 1
    return tbl, jnp.asarray(idx.reshape(-1))


def _na_attention(q, k, v, tbl, idx):
    b, s, d = q.shape
    rows = s // GRID_W
    n_rb = rows // NA_ROW_BLOCK
    nq = NA_ROW_BLOCK * GRID_W
    nk = NA_KEY_ROWS * GRID_W
    kern = functools.partial(_na_kernel, rows=rows, n_rb=n_rb)
    grid_spec = pltpu.PrefetchScalarGridSpec(
        num_scalar_prefetch=1,
        grid=(NA_HEADS // 2, b, n_rb),
        in_specs=[pl.BlockSpec((1, nq, LANES), lambda p, bi, r, idx_r: (bi, r, p)),
                  pl.BlockSpec((1, s, LANES), lambda p, bi, r, idx_r: (bi, 0, p)),
                  pl.BlockSpec((1, s, LANES), lambda p, bi, r, idx_r: (bi, 0, p)),
                  pl.BlockSpec((1, 2 * NA_TILES_PER_HEAD, GRID_W, LANES), lambda p, bi, r, idx_r: (p, 0, 0, 0))],
        out_specs=pl.BlockSpec((1, nq, LANES), lambda p, bi, r, idx_r: (bi, r, p)),
        scratch_shapes=[pltpu.VMEM((2 * nq, LANES), BF16), pltpu.VMEM((2 * nq, nk), F32),
                        pltpu.VMEM((2 * nq, nk), BF16), pltpu.VMEM((2 * nq, 1), F32)],
    )
    return pl.pallas_call(
        kern,
        out_shape=jax.ShapeDtypeStruct((b, s, d), BF16),
        grid_spec=grid_spec,
        compiler_params=_cparams(("parallel", "parallel", "arbitrary")),
        name="na_attention",
    )(idx, q, k, v, tbl)


def _mem_kv_kernel(m_ref, g_ref, w_ref, k_ref, v_ref):
    h = _rms(m_ref[0], g_ref[...]).astype(BF16)
    kv = jnp.dot(h, w_ref[...], preferred_element_type=F32)
    k_ref[0] = kv[:, 0:XA_INNER].astype(BF16)
    v_ref[0] = kv[:, XA_INNER:2 * XA_INNER].astype(BF16)


def _mem_kv(mem, g, w):
    b, m, d = mem.shape
    return pl.pallas_call(
        _mem_kv_kernel,
        out_shape=[jax.ShapeDtypeStruct((b, m, XA_INNER), BF16)] * 2,
        grid=(b,),
        in_specs=[pl.BlockSpec((1, m, d), lambda i: (i, 0, 0)), _full(g.shape), _full(w.shape)],
        out_specs=[pl.BlockSpec((1, m, XA_INNER), lambda i: (i, 0, 0))] * 2,
        compiler_params=_cparams(("parallel",)),
        name="mem_kv",
    )(mem, g, w)


def _post_mix_kernel(*refs, n_a):
    x_ref = refs[0]
    a_refs = refs[1:1 + n_a]
    w_refs = refs[1 + n_a:1 + 2 * n_a]
    g_ref, wq_ref, mk_ref, mv_ref, wo_ref, o_ref = refs[1 + 2 * n_a:]
    x1 = x_ref[0]
    for a_ref, w_ref in zip(a_refs, w_refs):
        x1 = x1 + jnp.dot(a_ref[0], w_ref[...], preferred_element_type=F32)
    h = _rms(x1, g_ref[...]).astype(BF16)
    c = XA_HEAD_DIM ** -0.5 * LOG2E
    q = (jnp.dot(h, wq_ref[...], preferred_element_type=F32) * c).astype(BF16)
    outs = []
    for hd in range(XA_HEADS):
        sl = slice(hd * XA_HEAD_DIM, (hd + 1) * XA_HEAD_DIM)
        s = lax.dot_general(q[:, sl], mk_ref[0, :, sl], (((1,), (1,)), ((), ())), preferred_element_type=F32)
        p = jnp.exp2(s - jnp.max(s, axis=1, keepdims=True))
        linv = 1.0 / jnp.sum(p, axis=1, keepdims=True)
        o = jnp.dot(p.astype(BF16), mv_ref[0, :, sl], preferred_element_type=F32) * linv
        outs.append(o.astype(BF16))
    o_all = jnp.concatenate(outs, axis=1)
    o_ref[0] = x1 + jnp.dot(o_all, wo_ref[...], preferred_element_type=F32)


def _post_mix(x, a_list, w_list, g, wq, mk, mv, wo):
    b, s, d = x.shape
    tm = TOKEN_TILE
    n_a = len(a_list)
    tok = lambda w: pl.BlockSpec((1, tm, w), lambda i, j: (i, j, 0))
    memspec = pl.BlockSpec((1, MEM_TOKENS, XA_INNER), lambda i, j: (i, 0, 0))
    return pl.pallas_call(
        functools.partial(_post_mix_kernel, n_a=n_a),
        out_shape=jax.ShapeDtypeStruct((b, s, d), F32),
        grid=(b, s // tm),
        in_specs=([tok(d)] + [tok(a.shape[2]) for a in a_list] + [_full(w.shape) for w in w_list]
                  + [_full(g.shape), _full(wq.shape), memspec, memspec, _full(wo.shape)]),
        out_specs=tok(d),
        compiler_params=_cparams(("parallel", "parallel")),
        name="post_mix",
    )(x, *a_list, *w_list, g, wq, mk, mv, wo)


def _router_gates(tf, wr_hi_ref, wr_lo_ref, br_ref):
    t_hi = tf.astype(BF16)
    t_lo = (tf - t_hi.astype(F32)).astype(BF16)
    w_hi, w_lo = wr_hi_ref[...], wr_lo_ref[...]
    logits = (jnp.dot(t_hi, w_hi, preferred_element_type=F32) + jnp.dot(t_lo, w_hi, preferred_element_type=F32)
              + jnp.dot(t_hi, w_lo, preferred_element_type=F32)) + br_ref[...]
    lane = lax.broadcasted_iota(jnp.int32, logits.shape, 1)
    big = jnp.int32(LANES)
    is_g = lane < N_GROUPS
    lg = jnp.where(is_g, logits, NEG_INF)
    gmax = jnp.max(lg, axis=1, keepdims=True)
    g_p = 1.0 / jnp.sum(jnp.where(is_g, jnp.exp(lg - gmax), 0.0), axis=1, keepdims=True)
    g_i = jnp.min(jnp.where(lg == gmax, lane, big), axis=1, keepdims=True)
    lo = N_GROUPS + EXPERTS_PER_GROUP * g_i
    in_grp = (lane >= lo) & (lane < lo + EXPERTS_PER_GROUP)
    el = jnp.where(in_grp, logits, NEG_INF)
    v1 = jnp.max(el, axis=1, keepdims=True)
    i1 = jnp.min(jnp.where(el == v1, lane, big), axis=1, keepdims=True)
    el2 = jnp.where(lane == i1, NEG_INF, el)
    v2 = jnp.max(el2, axis=1, keepdims=True)
    i2 = jnp.min(jnp.where(el2 == v2, lane, big), axis=1, keepdims=True)
    e2 = jnp.exp(v2 - v1)
    w1 = g_p / (1.0 + e2)
    w2 = g_p * e2 / (1.0 + e2)
    return jnp.where(lane == i1, w1, 0.0) + jnp.where(lane == i2, w2, 0.0)


def _moe_kernel(x_ref, g_ref, wr_hi_ref, wr_lo_ref, br_ref, wgu_ref, wd_ref, gf_ref, o_ref,
                t_ref, gate_ref, acc_ref, *, final_norm):
    e = pl.program_id(1)

    @pl.when(e == 0)
    def _():
        tf = _rms(x_ref[...], g_ref[...])
        t_ref[...] = tf.astype(BF16)
        gate_ref[...] = _router_gates(tf, wr_hi_ref, wr_lo_ref, br_ref)
        acc_ref[...] = jnp.zeros(acc_ref.shape, F32)

    hgu = jnp.dot(t_ref[...], wgu_ref[0], preferred_element_type=F32)
    hg, hu = hgu[:, 0:D_EXPERT], hgu[:, D_EXPERT:2 * D_EXPERT]
    gates = gate_ref[...]
    lane = lax.broadcasted_iota(jnp.int32, gates.shape, 1)
    ge = jnp.sum(jnp.where(lane == N_GROUPS + e, gates, 0.0), axis=1, keepdims=True)
    act = (hg * jax.nn.sigmoid(hg) * hu * ge).astype(BF16)
    acc_ref[...] += jnp.dot(act, wd_ref[0], preferred_element_type=F32)

    @pl.when(e == N_EXPERTS - 1)
    def _():
        y = x_ref[...] + acc_ref[...]
        if final_norm:
            y = _rms(y, gf_ref[...])
        o_ref[...] = y


def _moe(x2d, g, wr_hi, wr_lo, br, wgu, wd, gf, final_norm):
    t, d = x2d.shape
    tm = TOKEN_TILE
    tok = pl.BlockSpec((tm, d), lambda i, e: (i, 0))
    return pl.pallas_call(
        functools.partial(_moe_kernel, final_norm=final_norm),
        out_shape=jax.ShapeDtypeStruct((t, d), F32),
        grid=(t // tm, N_EXPERTS),
        in_specs=[tok, _full(g.shape), _full(wr_hi.shape), _full(wr_lo.shape), _full(br.shape),
                  pl.BlockSpec((1, d, 2 * D_EXPERT), lambda i, e: (e, 0, 0)),
                  pl.BlockSpec((1, D_EXPERT, d), lambda i, e: (e, 0, 0)),
                  _full(gf.shape)],
        out_specs=tok,
        scratch_shapes=[pltpu.VMEM((tm, d), BF16), pltpu.VMEM((tm, LANES), F32), pltpu.VMEM((tm, d), F32)],
        compiler_params=_cparams(("parallel", "arbitrary")),
        name="moe",
    )(x2d, g, wr_hi, wr_lo, br, wgu, wd, gf)


def _row(v):
    return v.reshape(1, -1).astype(F32)


def _rope_tables(seq_len, rot_dim, lane_starts):
    inv = ROPE_THETA ** (-jnp.arange(0, rot_dim, 2, dtype=F32) / rot_dim)
    ang = jnp.arange(seq_len, dtype=F32)[:, None] * inv[None, :]
    cos, sin = jnp.cos(ang), jnp.sin(ang)
    ct = jnp.ones((seq_len, LANES), F32)
    st = jnp.zeros((seq_len, LANES), F32)
    for l0 in lane_starts:
        ct = ct.at[:, l0:l0 + rot_dim].set(jnp.concatenate([cos, cos], axis=1))
        st = st.at[:, l0:l0 + rot_dim].set(jnp.concatenate([-sin, sin], axis=1))
    return ct, st


def _prep_even(w_in, w_uq, w_ukv, w_o):
    c = np.cumsum([0, MLA_Q_RANK, MLA_KV_RANK, MLA_ROPE, DIFF_QK, DIFF_QK, DIFF_VW])
    kpe = jnp.zeros((D_MODEL, LANES), F32).at[:, MLA_NOPE:MLA_NOPE + MLA_ROPE].set(w_in[:, c[2]:c[3]])
    win = jnp.concatenate([w_in[:, c[0]:c[2]], w_in[:, c[3]:c[6]], kpe], axis=1).astype(BF16)
    uq = w_uq.reshape(MLA_Q_RANK, MLA_HEADS, MLA_NOPE + MLA_ROPE)
    uq = jnp.pad(uq, ((0, 0), (0, 0), (0, LANES - MLA_NOPE - MLA_ROPE))).reshape(MLA_Q_RANK, MLA_HEADS * LANES)
    ukv = w_ukv.reshape(MLA_KV_RANK, MLA_HEADS, MLA_NOPE + MLA_V)
    uk = jnp.pad(ukv[:, :, :MLA_NOPE], ((0, 0), (0, 0), (0, LANES - MLA_NOPE))).reshape(MLA_KV_RANK,
                                                                                       MLA_HEADS * LANES)
    uv = ukv[:, :, MLA_NOPE:].reshape(MLA_KV_RANK, MLA_HEADS * MLA_V)
    n_mla = MLA_HEADS * MLA_V
    return (win, uq.astype(BF16), uk.astype(BF16), uv.astype(BF16),
            w_o[:n_mla].astype(BF16), w_o[n_mla:].astype(BF16))


def _prep_moe(w_rg, b_rg, w_re, b_re, w_gate, w_up, w_down):
    wr = jnp.zeros((D_MODEL, LANES), F32).at[:, :N_GROUPS].set(w_rg).at[:, N_GROUPS:N_GROUPS + N_EXPERTS].set(w_re)
    br = jnp.zeros((1, LANES), F32).at[0, :N_GROUPS].set(b_rg).at[0, N_GROUPS:N_GROUPS + N_EXPERTS].set(b_re)
    wr_hi = wr.astype(BF16)
    wr_lo = (wr - wr_hi.astype(F32)).astype(BF16)
    wgu = jnp.concatenate([w_gate, w_up], axis=-1).reshape(N_EXPERTS, D_MODEL, 2 * D_EXPERT).astype(BF16)
    wd = w_down.reshape(N_EXPERTS, D_EXPERT, D_MODEL).astype(BF16)
    return wr_hi, wr_lo, br, wgu, wd


def _trunk(x, mem, p, tables):
    b, s, d = x.shape
    depth = p['ln_mix'].shape[0]
    for layer in range(depth):
        i = layer // 2
        if layer % 2 == 0:
            win, uq, uk, uv, wo_a, wo_b = _prep_even(p['w_in_even'][i], p['mla_w_uq'][i], p['mla_w_ukv'][i],
                                                     p['w_o_even'][i])
            cm, sm, cd, sd = tables[s]
            q, k, v, dq, dk, dv = _even_in(x, _row(p['ln_mix'][layer]), win, _row(p['mla_q_norm'][i]),
                                           _row(p['mla_kv_norm'][i]), uq, uk, uv, cm, sm, cd, sd)
            o_mla = _flash(q, k, v, 2 * LANES)
            lamp = jnp.zeros((8, LANES), F32)
            for r, name in enumerate(('diff_lambda_q1', 'diff_lambda_k1', 'diff_lambda_q2', 'diff_lambda_k2')):
                lamp = lamp.at[r, :DIFF_HEAD_DIM].set(p[name][i])
            lam_init = 0.8 - 0.6 * math.exp(-0.3 * layer)
            o_diff = _flash(dq, dk, dv, LANES, lamp=lamp, subln=_row(p['diff_subln'][i]), lam_init=lam_init)
            a_list, w_list = [o_mla, o_diff], [wo_a, wo_b]
        else:
            q, k, v = _odd_in(x, _row(p['ln_mix'][layer]), p['w_in_odd'][i].astype(BF16))
            tbl, idx = _na_tables(p['na_rpb'][i])
            a_list, w_list = [_na_attention(q, k, v, tbl, idx)], [p['w_o_odd'][i].astype(BF16)]
        mk, mv = _mem_kv(mem, _row(p['ln_mem'][layer]), p['xa_w_kv'][layer].astype(BF16))
        x = _post_mix(x, a_list, w_list, _row(p['ln_xattn'][layer]), p['xa_w_q'][layer].astype(BF16), mk, mv,
                      p['xa_w_o'][layer].astype(BF16))
        wr_hi, wr_lo, br, wgu, wd = _prep_moe(p['moe_w_group'][layer], p['moe_b_group'][layer],
                                              p['moe_w_expert'][layer], p['moe_b_expert'][layer],
                                              p['moe_w_gate'][layer], p['moe_w_up'][layer], p['moe_w_down'][layer])
        x = _moe(x.reshape(b * s, d), _row(p['ln_ffn'][layer]), wr_hi, wr_lo, br, wgu, wd, _row(p['ln_final']),
                 final_norm=(layer == depth - 1)).reshape(b, s, d)
    return x


def kernel(x_prompt, x_sample, mem_prompt, mem_sample, ln_mix, w_in_even, mla_q_norm, mla_kv_norm, mla_w_uq, mla_w_ukv, diff_lambda_q1, diff_lambda_k1, diff_lambda_q2, diff_lambda_k2, diff_subln, w_o_even, w_in_odd, na_rpb, w_o_odd, ln_xattn, ln_mem, xa_w_q, xa_w_kv, xa_w_o, ln_ffn, moe_w_group, moe_b_group, moe_w_expert, moe_b_expert, moe_w_gate, moe_w_up, moe_w_down, ln_final):
    p = dict(ln_mix=ln_mix, w_in_even=w_in_even, mla_q_norm=mla_q_norm, mla_kv_norm=mla_kv_norm,
             mla_w_uq=mla_w_uq, mla_w_ukv=mla_w_ukv, diff_lambda_q1=diff_lambda_q1,
             diff_lambda_k1=diff_lambda_k1, diff_lambda_q2=diff_lambda_q2, diff_lambda_k2=diff_lambda_k2,
             diff_subln=diff_subln, w_o_even=w_o_even, w_in_odd=w_in_odd, na_rpb=na_rpb, w_o_odd=w_o_odd,
             ln_xattn=ln_xattn, ln_mem=ln_mem, xa_w_q=xa_w_q, xa_w_kv=xa_w_kv, xa_w_o=xa_w_o,
             ln_ffn=ln_ffn, moe_w_group=moe_w_group, moe_b_group=moe_b_group, moe_w_expert=moe_w_expert,
             moe_b_expert=moe_b_expert, moe_w_gate=moe_w_gate, moe_w_up=moe_w_up, moe_w_down=moe_w_down,
             ln_final=ln_final)
    tables = {}
    for s in {x_prompt.shape[1], x_sample.shape[1]}:
        cm, sm = _rope_tables(s, MLA_ROPE, (MLA_NOPE,))
        cd, sd = _rope_tables(s, DIFF_ROT, (0, DIFF_HEAD_DIM))
        tables[s] = (cm, sm, cd, sd)
    return (_trunk(x_prompt, mem_prompt, p, tables), _trunk(x_sample, mem_sample, p, tables))
```

```python
import functools
import math

import numpy as np
import jax
import jax.numpy as jnp
from jax import lax
from jax.experimental import pallas as pl
from jax.experimental.pallas import tpu as pltpu

F32 = jnp.float32
BF16 = jnp.bfloat16

D_MODEL = 1024
ROPE_THETA = 500000.0
NORM_EPS = 1e-6
NEG_INF = -1e30
LOG2E = math.log2(math.e)

MLA_HEADS = 8
MLA_Q_RANK = 384
MLA_KV_RANK = 256
MLA_NOPE = 64
MLA_ROPE = 32
MLA_V = 64
DIFF_HEADS = 4
DIFF_HEAD_DIM = 64
DIFF_ROT = DIFF_HEAD_DIM // 4
DIFF_QK = DIFF_HEADS * 2 * DIFF_HEAD_DIM
DIFF_VW = DIFF_HEADS * 2 * DIFF_HEAD_DIM
GRID_W = 64
NA_HEADS = 16
NA_HEAD_DIM = 64
WIN_ROWS = 8
WIN_COLS = 16
MEM_TOKENS = 256
XA_HEADS = 4
XA_HEAD_DIM = 128
XA_INNER = XA_HEADS * XA_HEAD_DIM
N_GROUPS = 4
EXPERTS_PER_GROUP = 4
N_EXPERTS = N_GROUPS * EXPERTS_PER_GROUP
D_EXPERT = 256

LANES = 128
VMEM_LIMIT = 56 * 1024 * 1024

TOKEN_TILE = 512
FLASH_TQ = 256
FLASH_TK = 512
FLASH_UNROLL = 8
NA_ROW_BLOCK = 8
NA_KEY_ROWS = 2 * WIN_ROWS
NA_GROUP_ROWS = 4
NA_TILE_CODES = 3
NA_TILES_PER_HEAD = NA_TILE_CODES * 16 + 1


def _cparams(sem):
    return pltpu.CompilerParams(dimension_semantics=sem, vmem_limit_bytes=VMEM_LIMIT)


def _rms(xf, g):
    return xf * lax.rsqrt(jnp.mean(xf * xf, axis=-1, keepdims=True) + NORM_EPS) * g


def _full(shape):
    nd = len(shape)
    return pl.BlockSpec(shape, lambda *_: (0,) * nd)


def _rope_block(x, cos, sin, half):
    lane = lax.broadcasted_iota(jnp.int32, x.shape, 1)
    up = pltpu.roll(x, LANES - half, 1)
    down = pltpu.roll(x, half, 1)
    first = lax.bitwise_and(lane, 2 * half - 1) < half
    return x * cos + jnp.where(first, up, down) * sin


def _even_in_kernel(x_ref, g_ref, win_ref, qn_ref, kvn_ref, wuq_ref, wuk_ref, wuv_ref,
                    cm_ref, sm_ref, cd_ref, sd_ref,
                    q_ref, k_ref, v_ref, dq_ref, dk_ref, dv_ref):
    h = _rms(x_ref[0], g_ref[...]).astype(BF16)
    proj = jnp.dot(h, win_ref[...], preferred_element_type=F32)
    o_cq, o_ckv, o_dq, o_dk, o_dv, o_kpe = 0, 384, 640, 1152, 1664, 2176
    cm, sm, cd, sd = cm_ref[...], sm_ref[...], cd_ref[...], sd_ref[...]

    cq = _rms(proj[:, o_cq:o_cq + MLA_Q_RANK], qn_ref[...]).astype(BF16)
    q = jnp.dot(cq, wuq_ref[...], preferred_element_type=F32)
    mla_c = (MLA_NOPE + MLA_ROPE) ** -0.5 * LOG2E
    for hd in range(MLA_HEADS):
        sl = slice(hd * LANES, (hd + 1) * LANES)
        q_ref[0, :, sl] = (_rope_block(q[:, sl], cm, sm, MLA_ROPE // 2) * mla_c).astype(BF16)

    ckv = _rms(proj[:, o_ckv:o_ckv + MLA_KV_RANK], kvn_ref[...]).astype(BF16)
    kn = jnp.dot(ckv, wuk_ref[...], preferred_element_type=F32)
    kpe = _rope_block(proj[:, o_kpe:o_kpe + LANES], cm, sm, MLA_ROPE // 2)
    for hd in range(MLA_HEADS):
        sl = slice(hd * LANES, (hd + 1) * LANES)
        k_ref[0, :, sl] = (kn[:, sl] + kpe).astype(BF16)
    v_ref[0] = jnp.dot(ckv, wuv_ref[...], preferred_element_type=F32).astype(BF16)

    diff_c = DIFF_HEAD_DIM ** -0.5 * LOG2E
    for hd in range(DIFF_HEADS):
        sl = slice(hd * LANES, (hd + 1) * LANES)
        dq_ref[0, :, sl] = (_rope_block(proj[:, o_dq + hd * LANES:o_dq + (hd + 1) * LANES], cd, sd,
                                        DIFF_ROT // 2) * diff_c).astype(BF16)
        dk_ref[0, :, sl] = _rope_block(proj[:, o_dk + hd * LANES:o_dk + (hd + 1) * LANES], cd, sd,
                                       DIFF_ROT // 2).astype(BF16)
    dv_ref[0] = proj[:, o_dv:o_dv + DIFF_VW].astype(BF16)


def _even_in(x, g, win, qn, kvn, wuq, wuk, wuv, cm, sm, cd, sd):
    b, s, d = x.shape
    tm = TOKEN_TILE
    tok = lambda w: pl.BlockSpec((1, tm, w), lambda i, j: (i, j, 0))
    tab = pl.BlockSpec((tm, LANES), lambda i, j: (j, 0))
    outs = [jax.ShapeDtypeStruct((b, s, w), BF16) for w in (1024, 1024, 512, 512, 512, 512)]
    return pl.pallas_call(
        _even_in_kernel,
        out_shape=outs,
        grid=(b, s // tm),
        in_specs=[tok(d), _full(g.shape), _full(win.shape), _full(qn.shape), _full(kvn.shape),
                  _full(wuq.shape), _full(wuk.shape), _full(wuv.shape), tab, tab, tab, tab],
        out_specs=[tok(1024), tok(1024), tok(512), tok(512), tok(512), tok(512)],
        compiler_params=_cparams(("parallel", "parallel")),
        name="even_in",
    )(x, g, win, qn, kvn, wuq, wuk, wuv, cm, sm, cd, sd)


def _flash_kernel(*refs, tq, tk, n_k, width, diff, lam_init):
    if diff:
        q_ref, k_ref, v_ref, lamp_ref, g_ref, o_ref, qs_ref, m_ref, l_ref, acc_ref = refs
    else:
        q_ref, k_ref, v_ref, o_ref, qs_ref, m_ref, l_ref, acc_ref = refs
    half = width // 2
    n_j = tk // LANES
    q = q_ref[0]
    lane = lax.broadcasted_iota(jnp.int32, (tq, width), 1)
    zero = jnp.zeros_like(q)
    qs_ref[0:tq, :] = jnp.where(lane < half, q, zero)
    qs_ref[tq:2 * tq, :] = jnp.where(lane >= half, q, zero)
    m_ref[...] = jnp.full(m_ref.shape, NEG_INF, F32)
    l_ref[...] = jnp.zeros(l_ref.shape, F32)
    acc_ref[...] = jnp.zeros(acc_ref.shape, F32)

    def scores(i):
        off = pl.multiple_of(i * tk, tk)
        kc = k_ref[0, pl.ds(off, tk), :]
        return off, lax.dot_general(qs_ref[...], kc, (((1,), (1,)), ((), ())), preferred_element_type=F32)

    def max_pass(i, carry):
        _, s = scores(i)
        m = m_ref[...]
        for j in range(n_j):
            m = jnp.maximum(m, s[:, j * LANES:(j + 1) * LANES])
        m_ref[...] = m
        return carry

    lax.fori_loop(0, n_k, max_pass, 0, unroll=FLASH_UNROLL)
    m_ref[...] = jnp.broadcast_to(jnp.max(m_ref[...], axis=1, keepdims=True), m_ref.shape)

    def pv_pass(i, carry):
        off, s = scores(i)
        vc = v_ref[0, pl.ds(off, tk), :]
        m = m_ref[...]
        ps = [jnp.exp2(s[:, j * LANES:(j + 1) * LANES] - m) for j in range(n_j)]
        lsum = ps[0]
        for pj in ps[1:]:
            lsum = lsum + pj
        l_ref[...] += lsum
        p = jnp.concatenate([pj.astype(BF16) for pj in ps], axis=1)
        acc_ref[...] += jnp.dot(p, vc, preferred_element_type=F32)
        return carry

    lax.fori_loop(0, n_k, pv_pass, 0, unroll=FLASH_UNROLL)

    o = acc_ref[...] / jnp.sum(l_ref[...], axis=1, keepdims=True)
    top, bot = o[0:tq], o[tq:2 * tq]
    if diff:
        lp = lamp_ref[...]
        lam = (jnp.exp(jnp.sum(lp[0:1] * lp[1:2], axis=1, keepdims=True))
               - jnp.exp(jnp.sum(lp[2:3] * lp[3:4], axis=1, keepdims=True)) + lam_init)
        od = top - lam * bot
        o_ref[0] = (_rms(od, g_ref[...]) * (1.0 - lam_init)).astype(BF16)
    else:
        lane_o = lax.broadcasted_iota(jnp.int32, (tq, LANES), 1)
        o_ref[0] = jnp.where(lane_o < LANES // 2, top, bot).astype(BF16)


def _flash(q, k, v, width, lamp=None, subln=None, lam_init=0.0):
    b, s, _ = q.shape
    n = q.shape[2] // width
    tq, tk = FLASH_TQ, FLASH_TK
    diff = lamp is not None
    kern = functools.partial(_flash_kernel, tq=tq, tk=tk, n_k=s // tk, width=width, diff=diff,
                             lam_init=lam_init)
    in_specs = [pl.BlockSpec((1, tq, width), lambda bi, p, qi: (bi, qi, p)),
                pl.BlockSpec((1, s, width), lambda bi, p, qi: (bi, 0, p)),
                pl.BlockSpec((1, s, LANES), lambda bi, p, qi: (bi, 0, p))]
    args = [q, k, v]
    if diff:
        in_specs += [_full(lamp.shape), _full(subln.shape)]
        args += [lamp, subln]
    return pl.pallas_call(
        kern,
        out_shape=jax.ShapeDtypeStruct((b, s, n * LANES), BF16),
        grid=(b, n, s // tq),
        in_specs=in_specs,
        out_specs=pl.BlockSpec((1, tq, LANES), lambda bi, p, qi: (bi, qi, p)),
        scratch_shapes=[pltpu.VMEM((2 * tq, width), BF16), pltpu.VMEM((2 * tq, LANES), F32),
                        pltpu.VMEM((2 * tq, LANES), F32), pltpu.VMEM((2 * tq, LANES), F32)],
        compiler_params=_cparams(("parallel", "parallel", "arbitrary")),
        name="flash_diff" if diff else "flash_mla",
    )(*args)


def _odd_in_kernel(x_ref, g_ref, w_ref, q_ref, k_ref, v_ref):
    h = _rms(x_ref[0], g_ref[...]).astype(BF16)
    qkv = jnp.dot(h, w_ref[...], preferred_element_type=F32)
    c = NA_HEAD_DIM ** -0.5 * LOG2E
    q_ref[0] = (qkv[:, 0:1024] * c).astype(BF16)
    k_ref[0] = qkv[:, 1024:2048].astype(BF16)
    v_ref[0] = qkv[:, 2048:3072].astype(BF16)


def _odd_in(x, g, w):
    b, s, d = x.shape
    tm = TOKEN_TILE
    tok = pl.BlockSpec((1, tm, d), lambda i, j: (i, j, 0))
    return pl.pallas_call(
        _odd_in_kernel,
        out_shape=[jax.ShapeDtypeStruct((b, s, d), BF16)] * 3,
        grid=(b, s // tm),
        in_specs=[tok, _full(g.shape), _full(w.shape)],
        out_specs=[tok, tok, tok],
        compiler_params=_cparams(("parallel", "parallel")),
        name="odd_in",
    )(x, g, w)


def _na_kernel(idx_ref, q_ref, k_ref, v_ref, tbl_ref, o_ref, qs_ref, os_ref, *, rows, n_rb):
    nq = NA_ROW_BLOCK * GRID_W
    nk = NA_KEY_ROWS * GRID_W
    gq = NA_GROUP_ROWS * GRID_W
    n_kp = NA_KEY_ROWS // 2
    rb = pl.program_id(2)
    ks = jnp.clip(NA_ROW_BLOCK * rb - WIN_ROWS // 2, 0, rows - NA_KEY_ROWS)
    pat = jnp.where(rb == 0, 0, jnp.where(rb == n_rb - 1, 2, 1))
    off = pl.multiple_of(ks * GRID_W, (WIN_ROWS // 2) * GRID_W)
    q = q_ref[0]
    lane = lax.broadcasted_iota(jnp.int32, (nq, LANES), 1)
    zero = jnp.zeros_like(q)
    qs_ref[0:nq, :] = jnp.where(lane < LANES // 2, q, zero)
    qs_ref[nq:2 * nq, :] = jnp.where(lane >= LANES // 2, q, zero)

    def group(g, carry):
        r0 = pl.multiple_of(g * gq, gq)
        per_head = NA_ROW_BLOCK // NA_GROUP_ROWS
        hh = g // per_head
        i0 = (g % per_head) * NA_GROUP_ROWS
        kw = k_ref[0, pl.ds(off, nk), :]
        vw = v_ref[0, pl.ds(off, nk), :]
        s = lax.dot_general(qs_ref[pl.ds(r0, gq), :], kw, (((1,), (1,)), ((), ())), preferred_element_type=F32)
        bias_rows = []
        for ii in range(NA_GROUP_ROWS):
            base = pat * (NA_ROW_BLOCK * n_kp) + (i0 + ii) * n_kp
            bias_rows.append(jnp.concatenate(
                [tbl_ref[0, hh * NA_TILES_PER_HEAD + idx_ref[base + kp]] for kp in range(n_kp)], axis=1))
        s = s + jnp.concatenate(bias_rows, axis=0)
        m = jnp.max(s, axis=1, keepdims=True)
        p = jnp.exp2(s - m)
        linv = 1.0 / jnp.sum(p, axis=1, keepdims=True)
        os_ref[pl.ds(r0, gq), :] = jnp.dot(p.astype(BF16), vw, preferred_element_type=F32) * linv
        return carry

    lax.fori_loop(0, 2 * nq // gq, group, 0, unroll=True)
    o_ref[0] = jnp.where(lane < LANES // 2, os_ref[0:nq, :], os_ref[nq:2 * nq, :]).astype(BF16)


def _na_tables(rpb):
    col = np.arange(GRID_W)
    col_start = np.clip(col - WIN_COLS // 2, 0, GRID_W - WIN_COLS)
    col_valid = (col[None, :] >= col_start[:, None]) & (col[None, :] < col_start[:, None] + WIN_COLS)
    col_idx = np.clip(col[None, :] - col[:, None], -(WIN_COLS - 1), WIN_COLS - 1) + WIN_COLS - 1
    n_ri = 2 * WIN_ROWS - 1
    ri = np.zeros((NA_TILES_PER_HEAD, LANES), np.int32)
    ok = np.zeros((NA_TILES_PER_HEAD, LANES), bool)
    for code in range(NA_TILE_CODES):
        for a in range(-1, n_ri):
            t = code * 16 + a + 1
            ri[t, :GRID_W] = np.clip(a, 0, n_ri - 1)
            ri[t, GRID_W:] = np.clip(a + 1, 0, n_ri - 1)
            ok[t, :GRID_W] = code in (0, 1) and 0 <= a < n_ri
            ok[t, GRID_W:] = code in (0, 2) and 0 <= a + 1 < n_ri
    onehot = (col_idx.reshape(-1)[None, :] == np.arange(2 * WIN_COLS - 1)[:, None]).astype(np.float32)
    band = jnp.dot(rpb.reshape(NA_HEADS * n_ri, 2 * WIN_COLS - 1).astype(F32), jnp.asarray(onehot),
                   precision=lax.Precision.HIGHEST).reshape(NA_HEADS, n_ri, GRID_W, GRID_W)
    band = jnp.where(col_valid[None, None], band * LOG2E, NEG_INF)
    masked = jnp.full((NA_HEADS, GRID_W, GRID_W), NEG_INF, F32)
    tiles = []
    for t in range(NA_TILES_PER_HEAD):
        left = band[:, ri[t, 0]] if ok[t, 0] else masked
        right = band[:, ri[t, GRID_W]] if ok[t, GRID_W] else masked
        tiles.append(jnp.concatenate([left, right], axis=-1))
    tbl = jnp.stack(tiles, axis=1).reshape(NA_HEADS // 2, 2 * NA_TILES_PER_HEAD, GRID_W, LANES)

    idx = np.zeros((3, NA_ROW_BLOCK, NA_KEY_ROWS // 2), np.int32)
    for pat in range(3):
        for i in range(NA_ROW_BLOCK):
            rel = (i, i + WIN_ROWS // 2, i + WIN_ROWS)[pat]
            start = (max(i - WIN_ROWS // 2, 0), i, min(i + WIN_ROWS // 2, WIN_ROWS))[pat]
            for kp in range(NA_KEY_ROWS // 2):
                kk = 2 * kp
                a = kk - rel + WIN_ROWS - 1
                v0 = start <= kk < start + WIN_ROWS
                v1 = start <= kk + 1 < start + WIN_ROWS
                if v0:
                    assert 0 <= a < n_ri
                if v1:
                    assert 0 <= a + 1 < n_ri
                if v0 and v1:
                    idx[pat, i, kp] = a + 1
                elif v0:
                    idx[pat, i, kp] = 16 + a + 1
                elif v1:
                    idx[pat, i, kp] = 32 + a + 1
                else:
                    idx[pat, i, kp] = NA_TILES_PER_HEAD - 1
    return tbl, jnp.asarray(idx.reshape(-1))


def _na_attention(q, k, v, tbl, idx):
    b, s, d = q.shape
    rows = s // GRID_W
    n_rb = rows // NA_ROW_BLOCK
    nq = NA_ROW_BLOCK * GRID_W
    nk = NA_KEY_ROWS * GRID_W
    kern = functools.partial(_na_kernel, rows=rows, n_rb=n_rb)
    grid_spec = pltpu.PrefetchScalarGridSpec(
        num_scalar_prefetch=1,
        grid=(NA_HEADS // 2, b, n_rb),
        in_specs=[pl.BlockSpec((1, nq, LANES), lambda p, bi, r, idx_r: (bi, r, p)),
                  pl.BlockSpec((1, s, LANES), lambda p, bi, r, idx_r: (bi, 0, p)),
                  pl.BlockSpec((1, s, LANES), lambda p, bi, r, idx_r: (bi, 0, p)),
                  pl.BlockSpec((1, 2 * NA_TILES_PER_HEAD, GRID_W, LANES), lambda p, bi, r, idx_r: (p, 0, 0, 0))],
        out_specs=pl.BlockSpec((1, nq, LANES), lambda p, bi, r, idx_r: (bi, r, p)),
        scratch_shapes=[pltpu.VMEM((2 * nq, LANES), BF16), pltpu.VMEM((2 * nq, LANES), F32)],
    )
    return pl.pallas_call(
        kern,
        out_shape=jax.ShapeDtypeStruct((b, s, d), BF16),
        grid_spec=grid_spec,
        compiler_params=_cparams(("parallel", "parallel", "arbitrary")),
        name="na_attention",
    )(idx, q, k, v, tbl)


def _mem_kv_kernel(m_ref, g_ref, w_ref, k_ref, v_ref):
    h = _rms(m_ref[0], g_ref[...]).astype(BF16)
    kv = jnp.dot(h, w_ref[...], preferred_element_type=F32)
    k_ref[0] = kv[:, 0:XA_INNER].astype(BF16)
    v_ref[0] = kv[:, XA_INNER:2 * XA_INNER].astype(BF16)


def _mem_kv(mem, g, w):
    b, m, d = mem.shape
    return pl.pallas_call(
        _mem_kv_kernel,
        out_shape=[jax.ShapeDtypeStruct((b, m, XA_INNER), BF16)] * 2,
        grid=(b,),
        in_specs=[pl.BlockSpec((1, m, d), lambda i: (i, 0, 0)), _full(g.shape), _full(w.shape)],
        out_specs=[pl.BlockSpec((1, m, XA_INNER), lambda i: (i, 0, 0))] * 2,
        compiler_params=_cparams(("parallel",)),
        name="mem_kv",
    )(mem, g, w)


def _post_mix_kernel(*refs, n_a):
    x_ref = refs[0]
    a_refs = refs[1:1 + n_a]
    w_refs = refs[1 + n_a:1 + 2 * n_a]
    g_ref, wq_ref, mk_ref, mv_ref, wo_ref, o_ref = refs[1 + 2 * n_a:]
    x1 = x_ref[0]
    for a_ref, w_ref in zip(a_refs, w_refs):
        x1 = x1 + jnp.dot(a_ref[0], w_ref[...], preferred_element_type=F32)
    h = _rms(x1, g_ref[...]).astype(BF16)
    c = XA_HEAD_DIM ** -0.5 * LOG2E
    q = (jnp.dot(h, wq_ref[...], preferred_element_type=F32) * c).astype(BF16)
    outs = []
    for hd in range(XA_HEADS):
        sl = slice(hd * XA_HEAD_DIM, (hd + 1) * XA_HEAD_DIM)
        s = lax.dot_general(q[:, sl], mk_ref[0, :, sl], (((1,), (1,)), ((), ())), preferred_element_type=F32)
        p = jnp.exp2(s - jnp.max(s, axis=1, keepdims=True))
        linv = 1.0 / jnp.sum(p, axis=1, keepdims=True)
        o = jnp.dot(p.astype(BF16), mv_ref[0, :, sl], preferred_element_type=F32) * linv
        outs.append(o.astype(BF16))
    o_all = jnp.concatenate(outs, axis=1)
    o_ref[0] = x1 + jnp.dot(o_all, wo_ref[...], preferred_element_type=F32)


def _post_mix(x, a_list, w_list, g, wq, mk, mv, wo):
    b, s, d = x.shape
    tm = TOKEN_TILE
    n_a = len(a_list)
    tok = lambda w: pl.BlockSpec((1, tm, w), lambda i, j: (i, j, 0))
    memspec = pl.BlockSpec((1, MEM_TOKENS, XA_INNER), lambda i, j: (i, 0, 0))
    return pl.pallas_call(
        functools.partial(_post_mix_kernel, n_a=n_a),
        out_shape=jax.ShapeDtypeStruct((b, s, d), F32),
        grid=(b, s // tm),
        in_specs=([tok(d)] + [tok(a.shape[2]) for a in a_list] + [_full(w.shape) for w in w_list]
                  + [_full(g.shape), _full(wq.shape), memspec, memspec, _full(wo.shape)]),
        out_specs=tok(d),
        compiler_params=_cparams(("parallel", "parallel")),
        name="post_mix",
    )(x, *a_list, *w_list, g, wq, mk, mv, wo)


def _router_gates(tf, wr_hi_ref, wr_lo_ref, br_ref):
    t_hi = tf.astype(BF16)
    t_lo = (tf - t_hi.astype(F32)).astype(BF16)
    w_hi, w_lo = wr_hi_ref[...], wr_lo_ref[...]
    logits = (jnp.dot(t_hi, w_hi, preferred_element_type=F32) + jnp.dot(t_lo, w_hi, preferred_element_type=F32)
              + jnp.dot(t_hi, w_lo, preferred_element_type=F32)) + br_ref[...]
    lane = lax.broadcasted_iota(jnp.int32, logits.shape, 1)
    big = jnp.int32(LANES)
    is_g = lane < N_GROUPS
    lg = jnp.where(is_g, logits, NEG_INF)
    gmax = jnp.max(lg, axis=1, keepdims=True)
    g_p = 1.0 / jnp.sum(jnp.where(is_g, jnp.exp(lg - gmax), 0.0), axis=1, keepdims=True)
    g_i = jnp.min(jnp.where(lg == gmax, lane, big), axis=1, keepdims=True)
    lo = N_GROUPS + EXPERTS_PER_GROUP * g_i
    in_grp = (lane >= lo) & (lane < lo + EXPERTS_PER_GROUP)
    el = jnp.where(in_grp, logits, NEG_INF)
    v1 = jnp.max(el, axis=1, keepdims=True)
    i1 = jnp.min(jnp.where(el == v1, lane, big), axis=1, keepdims=True)
    el2 = jnp.where(lane == i1, NEG_INF, el)
    v2 = jnp.max(el2, axis=1, keepdims=True)
    i2 = jnp.min(jnp.where(el2 == v2, lane, big), axis=1, keepdims=True)
    e2 = jnp.exp(v2 - v1)
    w1 = g_p / (1.0 + e2)
    w2 = g_p * e2 / (1.0 + e2)
    return jnp.where(lane == i1, w1, 0.0) + jnp.where(lane == i2, w2, 0.0)


def _moe_kernel(x_ref, g_ref, wr_hi_ref, wr_lo_ref, br_ref, wgu_ref, wd_ref, gf_ref, o_ref,
                t_ref, gate_ref, acc_ref, *, final_norm):
    e = pl.program_id(1)

    @pl.when(e == 0)
    def _():
        tf = _rms(x_ref[...], g_ref[...])
        t_ref[...] = tf.astype(BF16)
        gate_ref[...] = _router_gates(tf, wr_hi_ref, wr_lo_ref, br_ref)
        acc_ref[...] = jnp.zeros(acc_ref.shape, F32)

    hgu = jnp.dot(t_ref[...], wgu_ref[0], preferred_element_type=F32)
    hg, hu = hgu[:, 0:D_EXPERT], hgu[:, D_EXPERT:2 * D_EXPERT]
    gates = gate_ref[...]
    lane = lax.broadcasted_iota(jnp.int32, gates.shape, 1)
    ge = jnp.sum(jnp.where(lane == N_GROUPS + e, gates, 0.0), axis=1, keepdims=True)
    act = (hg * jax.nn.sigmoid(hg) * hu * ge).astype(BF16)
    acc_ref[...] += jnp.dot(act, wd_ref[0], preferred_element_type=F32)

    @pl.when(e == N_EXPERTS - 1)
    def _():
        y = x_ref[...] + acc_ref[...]
        if final_norm:
            y = _rms(y, gf_ref[...])
        o_ref[...] = y


def _moe(x2d, g, wr_hi, wr_lo, br, wgu, wd, gf, final_norm):
    t, d = x2d.shape
    tm = TOKEN_TILE
    tok = pl.BlockSpec((tm, d), lambda i, e: (i, 0))
    return pl.pallas_call(
        functools.partial(_moe_kernel, final_norm=final_norm),
        out_shape=jax.ShapeDtypeStruct((t, d), F32),
        grid=(t // tm, N_EXPERTS),
        in_specs=[tok, _full(g.shape), _full(wr_hi.shape), _full(wr_lo.shape), _full(br.shape),
                  pl.BlockSpec((1, d, 2 * D_EXPERT), lambda i, e: (e, 0, 0)),
                  pl.BlockSpec((1, D_EXPERT, d), lambda i, e: (e, 0, 0)),
                  _full(gf.shape)],
        out_specs=tok,
        scratch_shapes=[pltpu.VMEM((tm, d), BF16), pltpu.VMEM((tm, LANES), F32), pltpu.VMEM((tm, d), F32)],
        compiler_params=_cparams(("parallel", "arbitrary")),
        name="moe",
    )(x2d, g, wr_hi, wr_lo, br, wgu, wd, gf)


def _row(v):
    return v.reshape(1, -1).astype(F32)


def _rope_tables(seq_len, rot_dim, lane_starts):
    inv = ROPE_THETA ** (-jnp.arange(0, rot_dim, 2, dtype=F32) / rot_dim)
    ang = jnp.arange(seq_len, dtype=F32)[:, None] * inv[None, :]
    cos, sin = jnp.cos(ang), jnp.sin(ang)
    ct = jnp.ones((seq_len, LANES), F32)
    st = jnp.zeros((seq_len, LANES), F32)
    for l0 in lane_starts:
        ct = ct.at[:, l0:l0 + rot_dim].set(jnp.concatenate([cos, cos], axis=1))
        st = st.at[:, l0:l0 + rot_dim].set(jnp.concatenate([-sin, sin], axis=1))
    return ct, st


def _prep_even(w_in, w_uq, w_ukv, w_o):
    c = np.cumsum([0, MLA_Q_RANK, MLA_KV_RANK, MLA_ROPE, DIFF_QK, DIFF_QK, DIFF_VW])
    kpe = jnp.zeros((D_MODEL, LANES), F32).at[:, MLA_NOPE:MLA_NOPE + MLA_ROPE].set(w_in[:, c[2]:c[3]])
    win = jnp.concatenate([w_in[:, c[0]:c[2]], w_in[:, c[3]:c[6]], kpe], axis=1).astype(BF16)
    uq = w_uq.reshape(MLA_Q_RANK, MLA_HEADS, MLA_NOPE + MLA_ROPE)
    uq = jnp.pad(uq, ((0, 0), (0, 0), (0, LANES - MLA_NOPE - MLA_ROPE))).reshape(MLA_Q_RANK, MLA_HEADS * LANES)
    ukv = w_ukv.reshape(MLA_KV_RANK, MLA_HEADS, MLA_NOPE + MLA_V)
    uk = jnp.pad(ukv[:, :, :MLA_NOPE], ((0, 0), (0, 0), (0, LANES - MLA_NOPE))).reshape(MLA_KV_RANK,
                                                                                       MLA_HEADS * LANES)
    uv = ukv[:, :, MLA_NOPE:].reshape(MLA_KV_RANK, MLA_HEADS * MLA_V)
    n_mla = MLA_HEADS * MLA_V
    return (win, uq.astype(BF16), uk.astype(BF16), uv.astype(BF16),
            w_o[:n_mla].astype(BF16), w_o[n_mla:].astype(BF16))


def _prep_moe(w_rg, b_rg, w_re, b_re, w_gate, w_up, w_down):
    wr = jnp.zeros((D_MODEL, LANES), F32).at[:, :N_GROUPS].set(w_rg).at[:, N_GROUPS:N_GROUPS + N_EXPERTS].set(w_re)
    br = jnp.zeros((1, LANES), F32).at[0, :N_GROUPS].set(b_rg).at[0, N_GROUPS:N_GROUPS + N_EXPERTS].set(b_re)
    wr_hi = wr.astype(BF16)
    wr_lo = (wr - wr_hi.astype(F32)).astype(BF16)
    wgu = jnp.concatenate([w_gate, w_up], axis=-1).reshape(N_EXPERTS, D_MODEL, 2 * D_EXPERT).astype(BF16)
    wd = w_down.reshape(N_EXPERTS, D_EXPERT, D_MODEL).astype(BF16)
    return wr_hi, wr_lo, br, wgu, wd


def _trunk(x, mem, p, tables):
    b, s, d = x.shape
    depth = p['ln_mix'].shape[0]
    for layer in range(depth):
        i = layer // 2
        if layer % 2 == 0:
            win, uq, uk, uv, wo_a, wo_b = _prep_even(p['w_in_even'][i], p['mla_w_uq'][i], p['mla_w_ukv'][i],
                                                     p['w_o_even'][i])
            cm, sm, cd, sd = tables[s]
            q, k, v, dq, dk, dv = _even_in(x, _row(p['ln_mix'][layer]), win, _row(p['mla_q_norm'][i]),
                                           _row(p['mla_kv_norm'][i]), uq, uk, uv, cm, sm, cd, sd)
            o_mla = _flash(q, k, v, 2 * LANES)
            lamp = jnp.zeros((8, LANES), F32)
            for r, name in enumerate(('diff_lambda_q1', 'diff_lambda_k1', 'diff_lambda_q2', 'diff_lambda_k2')):
                lamp = lamp.at[r, :DIFF_HEAD_DIM].set(p[name][i])
            lam_init = 0.8 - 0.6 * math.exp(-0.3 * layer)
            o_diff = _flash(dq, dk, dv, LANES, lamp=lamp, subln=_row(p['diff_subln'][i]), lam_init=lam_init)
            a_list, w_list = [o_mla, o_diff], [wo_a, wo_b]
        else:
            q, k, v = _odd_in(x, _row(p['ln_mix'][layer]), p['w_in_odd'][i].astype(BF16))
            tbl, idx = _na_tables(p['na_rpb'][i])
            a_list, w_list = [_na_attention(q, k, v, tbl, idx)], [p['w_o_odd'][i].astype(BF16)]
        mk, mv = _mem_kv(mem, _row(p['ln_mem'][layer]), p['xa_w_kv'][layer].astype(BF16))
        x = _post_mix(x, a_list, w_list, _row(p['ln_xattn'][layer]), p['xa_w_q'][layer].astype(BF16), mk, mv,
                      p['xa_w_o'][layer].astype(BF16))
        wr_hi, wr_lo, br, wgu, wd = _prep_moe(p['moe_w_group'][layer], p['moe_b_group'][layer],
                                              p['moe_w_expert'][layer], p['moe_b_expert'][layer],
                                              p['moe_w_gate'][layer], p['moe_w_up'][layer], p['moe_w_down'][layer])
        x = _moe(x.reshape(b * s, d), _row(p['ln_ffn'][layer]), wr_hi, wr_lo, br, wgu, wd, _row(p['ln_final']),
                 final_norm=(layer == depth - 1)).reshape(b, s, d)
    return x


def kernel(x_prompt, x_sample, mem_prompt, mem_sample, ln_mix, w_in_even, mla_q_norm, mla_kv_norm, mla_w_uq, mla_w_ukv, diff_lambda_q1, diff_lambda_k1, diff_lambda_q2, diff_lambda_k2, diff_subln, w_o_even, w_in_odd, na_rpb, w_o_odd, ln_xattn, ln_mem, xa_w_q, xa_w_kv, xa_w_o, ln_ffn, moe_w_group, moe_b_group, moe_w_expert, moe_b_expert, moe_w_gate, moe_w_up, moe_w_down, ln_final):
    p = dict(ln_mix=ln_mix, w_in_even=w_in_even, mla_q_norm=mla_q_norm, mla_kv_norm=mla_kv_norm,
             mla_w_uq=mla_w_uq, mla_w_ukv=mla_w_ukv, diff_lambda_q1=diff_lambda_q1,
             diff_lambda_k1=diff_lambda_k1, diff_lambda_q2=diff_lambda_q2, diff_lambda_k2=diff_lambda_k2,
             diff_subln=diff_subln, w_o_even=w_o_even, w_in_odd=w_in_odd, na_rpb=na_rpb, w_o_odd=w_o_odd,
             ln_xattn=ln_xattn, ln_mem=ln_mem, xa_w_q=xa_w_q, xa_w_kv=xa_w_kv, xa_w_o=xa_w_o,
             ln_ffn=ln_ffn, moe_w_group=moe_w_group, moe_b_group=moe_b_group, moe_w_expert=moe_w_expert,
             moe_b_expert=moe_b_expert, moe_w_gate=moe_w_gate, moe_w_up=moe_w_up, moe_w_down=moe_w_down,
             ln_final=ln_final)
    tables = {}
    for s in {x_prompt.shape[1], x_sample.shape[1]}:
        cm, sm = _rope_tables(s, MLA_ROPE, (MLA_NOPE,))
        cd, sd = _rope_tables(s, DIFF_ROT, (0, DIFF_HEAD_DIM))
        tables[s] = (cm, sm, cd, sd)
    return (_trunk(x_prompt, mem_prompt, p, tables), _trunk(x_sample, mem_sample, p, tables))
```

```python
import functools
import math

import numpy as np
import jax
import jax.numpy as jnp
from jax import lax
from jax.experimental import pallas as pl
from jax.experimental.pallas import tpu as pltpu

F32 = jnp.float32
BF16 = jnp.bfloat16

D_MODEL = 1024
ROPE_THETA = 500000.0
NORM_EPS = 1e-6
NEG_INF = -1e30
LOG2E = math.log2(math.e)

MLA_HEADS = 8
MLA_Q_RANK = 384
MLA_KV_RANK = 256
MLA_NOPE = 64
MLA_ROPE = 32
MLA_V = 64
DIFF_HEADS = 4
DIFF_HEAD_DIM = 64
DIFF_ROT = DIFF_HEAD_DIM // 4
DIFF_QK = DIFF_HEADS * 2 * DIFF_HEAD_DIM
DIFF_VW = DIFF_HEADS * 2 * DIFF_HEAD_DIM
GRID_W = 64
NA_HEADS = 16
NA_HEAD_DIM = 64
WIN_ROWS = 8
WIN_COLS = 16
MEM_TOKENS = 256
XA_HEADS = 4
XA_HEAD_DIM = 128
XA_INNER = XA_HEADS * XA_HEAD_DIM
N_GROUPS = 4
EXPERTS_PER_GROUP = 4
N_EXPERTS = N_GROUPS * EXPERTS_PER_GROUP
PAIRS_PER_GROUP = EXPERTS_PER_GROUP * (EXPERTS_PER_GROUP - 1) // 2
D_EXPERT = 256

LANES = 128
VMEM_LIMIT = 56 * 1024 * 1024

TOKEN_TILE = 512
MOE_TILE = 256
FLASH_TQ = 256
FLASH_TK = 512
FLASH_UNROLL = 8
NA_ROW_BLOCK = 8
NA_KEY_ROWS = 2 * WIN_ROWS
NA_GROUP_ROWS = 4
NA_TILE_CODES = 3
NA_TILES_PER_HEAD = NA_TILE_CODES * 16 + 1


def _cparams(sem):
    return pltpu.CompilerParams(dimension_semantics=sem, vmem_limit_bytes=VMEM_LIMIT)


def _rms(xf, g):
    return xf * lax.rsqrt(jnp.mean(xf * xf, axis=-1, keepdims=True) + NORM_EPS) * g


def _full(shape):
    nd = len(shape)
    return pl.BlockSpec(shape, lambda *_: (0,) * nd)


def _rope_block(x, cos, sin, half):
    lane = lax.broadcasted_iota(jnp.int32, x.shape, 1)
    up = pltpu.roll(x, LANES - half, 1)
    down = pltpu.roll(x, half, 1)
    first = lax.bitwise_and(lane, 2 * half - 1) < half
    return x * cos + jnp.where(first, up, down) * sin


def _even_in_kernel(x_ref, g_ref, win_ref, qn_ref, kvn_ref, wuq_ref, wuk_ref, wuv_ref,
                    cm_ref, sm_ref, cd_ref, sd_ref,
                    q_ref, k_ref, v_ref, dq_ref, dk_ref, dv_ref):
    h = _rms(x_ref[0], g_ref[...]).astype(BF16)
    proj = jnp.dot(h, win_ref[...], preferred_element_type=F32)
    o_cq, o_ckv, o_dq, o_dk, o_dv, o_kpe = 0, 384, 640, 1152, 1664, 2176
    cm, sm, cd, sd = cm_ref[...], sm_ref[...], cd_ref[...], sd_ref[...]

    cq = _rms(proj[:, o_cq:o_cq + MLA_Q_RANK], qn_ref[...]).astype(BF16)
    q = jnp.dot(cq, wuq_ref[...], preferred_element_type=F32)
    mla_c = (MLA_NOPE + MLA_ROPE) ** -0.5 * LOG2E
    for hd in range(MLA_HEADS):
        sl = slice(hd * LANES, (hd + 1) * LANES)
        q_ref[0, :, sl] = (_rope_block(q[:, sl], cm, sm, MLA_ROPE // 2) * mla_c).astype(BF16)

    ckv = _rms(proj[:, o_ckv:o_ckv + MLA_KV_RANK], kvn_ref[...]).astype(BF16)
    kn = jnp.dot(ckv, wuk_ref[...], preferred_element_type=F32)
    kpe = _rope_block(proj[:, o_kpe:o_kpe + LANES], cm, sm, MLA_ROPE // 2)
    for hd in range(MLA_HEADS):
        sl = slice(hd * LANES, (hd + 1) * LANES)
        k_ref[0, :, sl] = (kn[:, sl] + kpe).astype(BF16)
    v_ref[0] = jnp.dot(ckv, wuv_ref[...], preferred_element_type=F32).astype(BF16)

    diff_c = DIFF_HEAD_DIM ** -0.5 * LOG2E
    for hd in range(DIFF_HEADS):
        sl = slice(hd * LANES, (hd + 1) * LANES)
        dq_ref[0, :, sl] = (_rope_block(proj[:, o_dq + hd * LANES:o_dq + (hd + 1) * LANES], cd, sd,
                                        DIFF_ROT // 2) * diff_c).astype(BF16)
        dk_ref[0, :, sl] = _rope_block(proj[:, o_dk + hd * LANES:o_dk + (hd + 1) * LANES], cd, sd,
                                       DIFF_ROT // 2).astype(BF16)
    dv_ref[0] = proj[:, o_dv:o_dv + DIFF_VW].astype(BF16)


def _even_in(x, g, win, qn, kvn, wuq, wuk, wuv, cm, sm, cd, sd):
    b, s, d = x.shape
    tm = TOKEN_TILE
    tok = lambda w: pl.BlockSpec((1, tm, w), lambda i, j: (i, j, 0))
    tab = pl.BlockSpec((tm, LANES), lambda i, j: (j, 0))
    outs = [jax.ShapeDtypeStruct((b, s, w), BF16) for w in (1024, 1024, 512, 512, 512, 512)]
    return pl.pallas_call(
        _even_in_kernel,
        out_shape=outs,
        grid=(b, s // tm),
        in_specs=[tok(d), _full(g.shape), _full(win.shape), _full(qn.shape), _full(kvn.shape),
                  _full(wuq.shape), _full(wuk.shape), _full(wuv.shape), tab, tab, tab, tab],
        out_specs=[tok(1024), tok(1024), tok(512), tok(512), tok(512), tok(512)],
        compiler_params=_cparams(("parallel", "parallel")),
        name="even_in",
    )(x, g, win, qn, kvn, wuq, wuk, wuv, cm, sm, cd, sd)


def _flash_kernel(*refs, tq, tk, n_k, width, diff, lam_init):
    if diff:
        q_ref, k_ref, v_ref, lamp_ref, g_ref, o_ref, qs_ref, m_ref, l_ref, acc_ref = refs
    else:
        q_ref, k_ref, v_ref, o_ref, qs_ref, m_ref, l_ref, acc_ref = refs
    half = width // 2
    n_j = tk // LANES
    q = q_ref[0]
    lane = lax.broadcasted_iota(jnp.int32, (tq, width), 1)
    zero = jnp.zeros_like(q)
    qs_ref[0:tq, :] = jnp.where(lane < half, q, zero)
    qs_ref[tq:2 * tq, :] = jnp.where(lane >= half, q, zero)
    m_ref[...] = jnp.full(m_ref.shape, NEG_INF, F32)
    l_ref[...] = jnp.zeros(l_ref.shape, F32)
    acc_ref[...] = jnp.zeros(acc_ref.shape, F32)

    def scores(i):
        off = pl.multiple_of(i * tk, tk)
        kc = k_ref[0, pl.ds(off, tk), :]
        return off, lax.dot_general(qs_ref[...], kc, (((1,), (1,)), ((), ())), preferred_element_type=F32)

    def max_pass(i, carry):
        _, s = scores(i)
        m = m_ref[...]
        for j in range(n_j):
            m = jnp.maximum(m, s[:, j * LANES:(j + 1) * LANES])
        m_ref[...] = m
        return carry

    lax.fori_loop(0, n_k, max_pass, 0, unroll=FLASH_UNROLL)
    m_ref[...] = jnp.broadcast_to(jnp.max(m_ref[...], axis=1, keepdims=True), m_ref.shape)

    def pv_pass(i, carry):
        off, s = scores(i)
        vc = v_ref[0, pl.ds(off, tk), :]
        m = m_ref[...]
        ps = [jnp.exp2(s[:, j * LANES:(j + 1) * LANES] - m) for j in range(n_j)]
        lsum = ps[0]
        for pj in ps[1:]:
            lsum = lsum + pj
        l_ref[...] += lsum
        p = jnp.concatenate([pj.astype(BF16) for pj in ps], axis=1)
        acc_ref[...] += jnp.dot(p, vc, preferred_element_type=F32)
        return carry

    lax.fori_loop(0, n_k, pv_pass, 0, unroll=FLASH_UNROLL)

    o = acc_ref[...] / jnp.sum(l_ref[...], axis=1, keepdims=True)
    top, bot = o[0:tq], o[tq:2 * tq]
    if diff:
        lp = lamp_ref[...]
        lam = (jnp.exp(jnp.sum(lp[0:1] * lp[1:2], axis=1, keepdims=True))
               - jnp.exp(jnp.sum(lp[2:3] * lp[3:4], axis=1, keepdims=True)) + lam_init)
        od = top - lam * bot
        o_ref[0] = (_rms(od, g_ref[...]) * (1.0 - lam_init)).astype(BF16)
    else:
        lane_o = lax.broadcasted_iota(jnp.int32, (tq, LANES), 1)
        o_ref[0] = jnp.where(lane_o < LANES // 2, top, bot).astype(BF16)


def _flash(q, k, v, width, lamp=None, subln=None, lam_init=0.0):
    b, s, _ = q.shape
    n = q.shape[2] // width
    tq, tk = FLASH_TQ, FLASH_TK
    diff = lamp is not None
    kern = functools.partial(_flash_kernel, tq=tq, tk=tk, n_k=s // tk, width=width, diff=diff,
                             lam_init=lam_init)
    in_specs = [pl.BlockSpec((1, tq, width), lambda bi, p, qi: (bi, qi, p)),
                pl.BlockSpec((1, s, width), lambda bi, p, qi: (bi, 0, p)),
                pl.BlockSpec((1, s, LANES), lambda bi, p, qi: (bi, 0, p))]
    args = [q, k, v]
    if diff:
        in_specs += [_full(lamp.shape), _full(subln.shape)]
        args += [lamp, subln]
    return pl.pallas_call(
        kern,
        out_shape=jax.ShapeDtypeStruct((b, s, n * LANES), BF16),
        grid=(b, n, s // tq),
        in_specs=in_specs,
        out_specs=pl.BlockSpec((1, tq, LANES), lambda bi, p, qi: (bi, qi, p)),
        scratch_shapes=[pltpu.VMEM((2 * tq, width), BF16), pltpu.VMEM((2 * tq, LANES), F32),
                        pltpu.VMEM((2 * tq, LANES), F32), pltpu.VMEM((2 * tq, LANES), F32)],
        compiler_params=_cparams(("parallel", "parallel", "arbitrary")),
        name="flash_diff" if diff else "flash_mla",
    )(*args)


def _odd_in_kernel(x_ref, g_ref, w_ref, q_ref, k_ref, v_ref):
    h = _rms(x_ref[0], g_ref[...]).astype(BF16)
    qkv = jnp.dot(h, w_ref[...], preferred_element_type=F32)
    c = NA_HEAD_DIM ** -0.5 * LOG2E
    q_ref[0] = (qkv[:, 0:1024] * c).astype(BF16)
    k_ref[0] = qkv[:, 1024:2048].astype(BF16)
    v_ref[0] = qkv[:, 2048:3072].astype(BF16)


def _odd_in(x, g, w):
    b, s, d = x.shape
    tm = TOKEN_TILE
    tok = pl.BlockSpec((1, tm, d), lambda i, j: (i, j, 0))
    return pl.pallas_call(
        _odd_in_kernel,
        out_shape=[jax.ShapeDtypeStruct((b, s, d), BF16)] * 3,
        grid=(b, s // tm),
        in_specs=[tok, _full(g.shape), _full(w.shape)],
        out_specs=[tok, tok, tok],
        compiler_params=_cparams(("parallel", "parallel")),
        name="odd_in",
    )(x, g, w)


def _na_kernel(idx_ref, q_ref, k_ref, v_ref, tbl_ref, o_ref, qs_ref, os_ref, *, rows, n_rb):
    nq = NA_ROW_BLOCK * GRID_W
    nk = NA_KEY_ROWS * GRID_W
    gq = NA_GROUP_ROWS * GRID_W
    n_kp = NA_KEY_ROWS // 2
    rb = pl.program_id(2)
    ks = jnp.clip(NA_ROW_BLOCK * rb - WIN_ROWS // 2, 0, rows - NA_KEY_ROWS)
    pat = jnp.where(rb == 0, 0, jnp.where(rb == n_rb - 1, 2, 1))
    off = pl.multiple_of(ks * GRID_W, (WIN_ROWS // 2) * GRID_W)
    q = q_ref[0]
    lane = lax.broadcasted_iota(jnp.int32, (nq, LANES), 1)
    zero = jnp.zeros_like(q)
    qs_ref[0:nq, :] = jnp.where(lane < LANES // 2, q, zero)
    qs_ref[nq:2 * nq, :] = jnp.where(lane >= LANES // 2, q, zero)

    def group(g, carry):
        r0 = pl.multiple_of(g * gq, gq)
        per_head = NA_ROW_BLOCK // NA_GROUP_ROWS
        hh = g // per_head
        i0 = (g % per_head) * NA_GROUP_ROWS
        kw = k_ref[0, pl.ds(off, nk), :]
        vw = v_ref[0, pl.ds(off, nk), :]
        s = lax.dot_general(qs_ref[pl.ds(r0, gq), :], kw, (((1,), (1,)), ((), ())), preferred_element_type=F32)
        bias_rows = []
        for ii in range(NA_GROUP_ROWS):
            base = pat * (NA_ROW_BLOCK * n_kp) + (i0 + ii) * n_kp
            bias_rows.append(jnp.concatenate(
                [tbl_ref[0, hh * NA_TILES_PER_HEAD + idx_ref[base + kp]] for kp in range(n_kp)], axis=1))
        s = s + jnp.concatenate(bias_rows, axis=0)
        m = jnp.max(s, axis=1, keepdims=True)
        p = jnp.exp2(s - m)
        linv = 1.0 / jnp.sum(p, axis=1, keepdims=True)
        os_ref[pl.ds(r0, gq), :] = jnp.dot(p.astype(BF16), vw, preferred_element_type=F32) * linv
        return carry

    lax.fori_loop(0, 2 * nq // gq, group, 0, unroll=True)
    o_ref[0] = jnp.where(lane < LANES // 2, os_ref[0:nq, :], os_ref[nq:2 * nq, :]).astype(BF16)


def _na_tables(rpb):
    col = np.arange(GRID_W)
    col_start = np.clip(col - WIN_COLS // 2, 0, GRID_W - WIN_COLS)
    col_valid = (col[None, :] >= col_start[:, None]) & (col[None, :] < col_start[:, None] + WIN_COLS)
    col_idx = np.clip(col[None, :] - col[:, None], -(WIN_COLS - 1), WIN_COLS - 1) + WIN_COLS - 1
    n_ri = 2 * WIN_ROWS - 1
    ri = np.zeros((NA_TILES_PER_HEAD, LANES), np.int32)
    ok = np.zeros((NA_TILES_PER_HEAD, LANES), bool)
    for code in range(NA_TILE_CODES):
        for a in range(-1, n_ri):
            t = code * 16 + a + 1
            ri[t, :GRID_W] = np.clip(a, 0, n_ri - 1)
            ri[t, GRID_W:] = np.clip(a + 1, 0, n_ri - 1)
            ok[t, :GRID_W] = code in (0, 1) and 0 <= a < n_ri
            ok[t, GRID_W:] = code in (0, 2) and 0 <= a + 1 < n_ri
    onehot = (col_idx.reshape(-1)[None, :] == np.arange(2 * WIN_COLS - 1)[:, None]).astype(np.float32)
    band = jnp.dot(rpb.reshape(NA_HEADS * n_ri, 2 * WIN_COLS - 1).astype(F32), jnp.asarray(onehot),
                   precision=lax.Precision.HIGHEST).reshape(NA_HEADS, n_ri, GRID_W, GRID_W)
    band = jnp.where(col_valid[None, None], band * LOG2E, NEG_INF)
    masked = jnp.full((NA_HEADS, GRID_W, GRID_W), NEG_INF, F32)
    tiles = []
    for t in range(NA_TILES_PER_HEAD):
        left = band[:, ri[t, 0]] if ok[t, 0] else masked
        right = band[:, ri[t, GRID_W]] if ok[t, GRID_W] else masked
        tiles.append(jnp.concatenate([left, right], axis=-1))
    tbl = jnp.stack(tiles, axis=1).reshape(NA_HEADS // 2, 2 * NA_TILES_PER_HEAD, GRID_W, LANES)

    idx = np.zeros((3, NA_ROW_BLOCK, NA_KEY_ROWS // 2), np.int32)
    for pat in range(3):
        for i in range(NA_ROW_BLOCK):
            rel = (i, i + WIN_ROWS // 2, i + WIN_ROWS)[pat]
            start = (max(i - WIN_ROWS // 2, 0), i, min(i + WIN_ROWS // 2, WIN_ROWS))[pat]
            for kp in range(NA_KEY_ROWS // 2):
                kk = 2 * kp
                a = kk - rel + WIN_ROWS - 1
                v0 = start <= kk < start + WIN_ROWS
                v1 = start <= kk + 1 < start + WIN_ROWS
                if v0:
                    assert 0 <= a < n_ri
                if v1:
                    assert 0 <= a + 1 < n_ri
                if v0 and v1:
                    idx[pat, i, kp] = a + 1
                elif v0:
                    idx[pat, i, kp] = 16 + a + 1
                elif v1:
                    idx[pat, i, kp] = 32 + a + 1
                else:
                    idx[pat, i, kp] = NA_TILES_PER_HEAD - 1
    return tbl, jnp.asarray(idx.reshape(-1))


def _na_attention(q, k, v, tbl, idx):
    b, s, d = q.shape
    rows = s // GRID_W
    n_rb = rows // NA_ROW_BLOCK
    nq = NA_ROW_BLOCK * GRID_W
    nk = NA_KEY_ROWS * GRID_W
    kern = functools.partial(_na_kernel, rows=rows, n_rb=n_rb)
    grid_spec = pltpu.PrefetchScalarGridSpec(
        num_scalar_prefetch=1,
        grid=(NA_HEADS // 2, b, n_rb),
        in_specs=[pl.BlockSpec((1, nq, LANES), lambda p, bi, r, idx_r: (bi, r, p)),
                  pl.BlockSpec((1, s, LANES), lambda p, bi, r, idx_r: (bi, 0, p)),
                  pl.BlockSpec((1, s, LANES), lambda p, bi, r, idx_r: (bi, 0, p)),
                  pl.BlockSpec((1, 2 * NA_TILES_PER_HEAD, GRID_W, LANES), lambda p, bi, r, idx_r: (p, 0, 0, 0))],
        out_specs=pl.BlockSpec((1, nq, LANES), lambda p, bi, r, idx_r: (bi, r, p)),
        scratch_shapes=[pltpu.VMEM((2 * nq, LANES), BF16), pltpu.VMEM((2 * nq, LANES), F32)],
    )
    return pl.pallas_call(
        kern,
        out_shape=jax.ShapeDtypeStruct((b, s, d), BF16),
        grid_spec=grid_spec,
        compiler_params=_cparams(("parallel", "parallel", "arbitrary")),
        name="na_attention",
    )(idx, q, k, v, tbl)


def _mem_kv_kernel(m_ref, g_ref, w_ref, k_ref, v_ref):
    h = _rms(m_ref[0], g_ref[...]).astype(BF16)
    kv = jnp.dot(h, w_ref[...], preferred_element_type=F32)
    k_ref[0] = kv[:, 0:XA_INNER].astype(BF16)
    v_ref[0] = kv[:, XA_INNER:2 * XA_INNER].astype(BF16)


def _mem_kv(mem, g, w):
    b, m, d = mem.shape
    return pl.pallas_call(
        _mem_kv_kernel,
        out_shape=[jax.ShapeDtypeStruct((b, m, XA_INNER), BF16)] * 2,
        grid=(b,),
        in_specs=[pl.BlockSpec((1, m, d), lambda i: (i, 0, 0)), _full(g.shape), _full(w.shape)],
        out_specs=[pl.BlockSpec((1, m, XA_INNER), lambda i: (i, 0, 0))] * 2,
        compiler_params=_cparams(("parallel",)),
        name="mem_kv",
    )(mem, g, w)


def _post_mix_kernel(*refs, n_a):
    x_ref = refs[0]
    a_refs = refs[1:1 + n_a]
    w_refs = refs[1 + n_a:1 + 2 * n_a]
    (g_ref, wq_ref, mk_ref, mv_ref, wo_ref, gffn_ref, wr_hi_ref, wr_lo_ref, br_ref, tri_ref,
     o_ref, route_ref, cnt_ref) = refs[1 + 2 * n_a:]

    @pl.when((pl.program_id(0) == 0) & (pl.program_id(1) == 0))
    def _():
        cnt_ref[...] = jnp.zeros(cnt_ref.shape, F32)

    x1 = x_ref[0]
    for a_ref, w_ref in zip(a_refs, w_refs):
        x1 = x1 + jnp.dot(a_ref[0], w_ref[...], preferred_element_type=F32)
    h = _rms(x1, g_ref[...]).astype(BF16)
    c = XA_HEAD_DIM ** -0.5 * LOG2E
    q = (jnp.dot(h, wq_ref[...], preferred_element_type=F32) * c).astype(BF16)
    outs = []
    for hd in range(XA_HEADS):
        sl = slice(hd * XA_HEAD_DIM, (hd + 1) * XA_HEAD_DIM)
        s = lax.dot_general(q[:, sl], mk_ref[0, :, sl], (((1,), (1,)), ((), ())), preferred_element_type=F32)
        p = jnp.exp2(s - jnp.max(s, axis=1, keepdims=True))
        linv = 1.0 / jnp.sum(p, axis=1, keepdims=True)
        o = jnp.dot(p.astype(BF16), mv_ref[0, :, sl], preferred_element_type=F32) * linv
        outs.append(o.astype(BF16))
    o_all = jnp.concatenate(outs, axis=1)
    x2 = x1 + jnp.dot(o_all, wo_ref[...], preferred_element_type=F32)
    o_ref[0] = x2

    bucket, w_a, w_b = _route(_rms(x2, gffn_ref[...]), wr_hi_ref, wr_lo_ref, br_ref)
    lane = lax.broadcasted_iota(jnp.int32, (x2.shape[0], LANES), 1)
    onehot = lane == bucket
    oh = jnp.where(onehot, 1.0, 0.0)
    before = jnp.dot(tri_ref[...], oh.astype(BF16), preferred_element_type=F32) + cnt_ref[0:1, :]
    rank = jnp.sum(jnp.where(onehot, before, 0.0), axis=1, keepdims=True)
    cnt_ref[0:1, :] += jnp.sum(oh, axis=0, keepdims=True)
    route_ref[0] = (jnp.where(lane == 0, bucket.astype(F32), 0.0) + jnp.where(lane == 1, rank, 0.0)
                    + jnp.where(lane == 2, w_a, 0.0) + jnp.where(lane == 3, w_b, 0.0))


def _post_mix(x, a_list, w_list, g, wq, mk, mv, wo, gffn, wr_hi, wr_lo, br):
    b, s, d = x.shape
    tm = TOKEN_TILE
    n_a = len(a_list)
    tok = lambda w: pl.BlockSpec((1, tm, w), lambda i, j: (i, j, 0))
    memspec = pl.BlockSpec((1, MEM_TOKENS, XA_INNER), lambda i, j: (i, 0, 0))
    tri = jnp.tril(jnp.ones((tm, tm), F32), -1).astype(BF16)
    return pl.pallas_call(
        functools.partial(_post_mix_kernel, n_a=n_a),
        out_shape=[jax.ShapeDtypeStruct((b, s, d), F32), jax.ShapeDtypeStruct((b, s, LANES), F32),
                   jax.ShapeDtypeStruct((8, LANES), F32)],
        grid=(b, s // tm),
        in_specs=([tok(d)] + [tok(a.shape[2]) for a in a_list] + [_full(w.shape) for w in w_list]
                  + [_full(g.shape), _full(wq.shape), memspec, memspec, _full(wo.shape), _full(gffn.shape),
                     _full(wr_hi.shape), _full(wr_lo.shape), _full(br.shape), _full(tri.shape)]),
        out_specs=[tok(d), tok(LANES), _full((8, LANES))],
        compiler_params=_cparams(("arbitrary", "arbitrary")),
        name="post_mix",
    )(x, *a_list, *w_list, g, wq, mk, mv, wo, gffn, wr_hi, wr_lo, br, tri)


def _route(tf, wr_hi_ref, wr_lo_ref, br_ref):
    t_hi = tf.astype(BF16)
    t_lo = (tf - t_hi.astype(F32)).astype(BF16)
    w_hi, w_lo = wr_hi_ref[...], wr_lo_ref[...]
    logits = (jnp.dot(t_hi, w_hi, preferred_element_type=F32) + jnp.dot(t_lo, w_hi, preferred_element_type=F32)
              + jnp.dot(t_hi, w_lo, preferred_element_type=F32)) + br_ref[...]
    lane = lax.broadcasted_iota(jnp.int32, logits.shape, 1)
    big = jnp.int32(LANES)
    is_g = lane < N_GROUPS
    lg = jnp.where(is_g, logits, NEG_INF)
    gmax = jnp.max(lg, axis=1, keepdims=True)
    g_p = 1.0 / jnp.sum(jnp.where(is_g, jnp.exp(lg - gmax), 0.0), axis=1, keepdims=True)
    g_i = jnp.min(jnp.where(lg == gmax, lane, big), axis=1, keepdims=True)
    lo = N_GROUPS + EXPERTS_PER_GROUP * g_i
    in_grp = (lane >= lo) & (lane < lo + EXPERTS_PER_GROUP)
    el = jnp.where(in_grp, logits, NEG_INF)
    v1 = jnp.max(el, axis=1, keepdims=True)
    i1 = jnp.min(jnp.where(el == v1, lane, big), axis=1, keepdims=True)
    el2 = jnp.where(lane == i1, NEG_INF, el)
    v2 = jnp.max(el2, axis=1, keepdims=True)
    i2 = jnp.min(jnp.where(el2 == v2, lane, big), axis=1, keepdims=True)
    e2 = jnp.exp(v2 - v1)
    w1 = g_p / (1.0 + e2)
    w2 = g_p * e2 / (1.0 + e2)
    first_low = i1 < i2
    e_lo = jnp.minimum(i1, i2) - lo
    e_hi = jnp.maximum(i1, i2) - lo
    pair = jnp.where(e_lo == 0, 0, jnp.where(e_lo == 1, 3, 5)) + e_hi - e_lo - 1
    bucket = g_i * PAIRS_PER_GROUP + pair
    return bucket, jnp.where(first_low, w1, w2), jnp.where(first_low, w2, w1)


def _moe_kernel(ea_ref, eb_ref, valid_ref, src_hbm, x_hbm, gate_ref, g_ref, wgu_a_ref, wgu_b_ref,
                wd_a_ref, wd_b_ref, gf_ref, out_hbm, idx_ref, xbuf, obuf, isem, gsem, ssem, *, n_tiles,
                final_norm):
    tms = xbuf.shape[1]
    i = pl.program_id(0)
    slot = lax.rem(i, 2)
    other = 1 - slot

    def idx_copy(tile, s):
        return pltpu.make_async_copy(src_hbm.at[tile], idx_ref.at[s], isem.at[s])

    def gather_row(tok, s, r):
        return pltpu.make_async_copy(x_hbm.at[pl.ds(tok, 1)], xbuf.at[s, pl.ds(r, 1)], gsem.at[s])

    def scatter_row(tok, s, r):
        return pltpu.make_async_copy(obuf.at[s, pl.ds(r, 1)], out_hbm.at[pl.ds(tok, 1)], ssem.at[s])

    def start_gather(s):
        def body(r, carry):
            gather_row(idx_ref[s, r], s, r).start()
            return carry
        lax.fori_loop(0, tms, body, 0, unroll=8)

    def wait_gather(s):
        def body(r, carry):
            gather_row(0, s, r).wait()
            return carry
        lax.fori_loop(0, tms, body, 0, unroll=8)

    def wait_scatter(s, n):
        def body(r, carry):
            scatter_row(0, s, r).wait()
            return carry
        lax.fori_loop(0, n, body, 0)

    @pl.when(i == 0)
    def _():
        idx_copy(0, 0).start()
        idx_copy(0, 0).wait()
        start_gather(0)
        if n_tiles > 1:
            idx_copy(1, 1).start()

    @pl.when(i + 1 < n_tiles)
    def _():
        idx_copy(i + 1, other).wait()
        start_gather(other)

    wait_gather(slot)

    @pl.when(i >= 2)
    def _():
        wait_scatter(slot, valid_ref[jnp.maximum(i - 2, 0)])

    n_valid = valid_ref[i]

    @pl.when(n_valid > 0)
    def _():
        x = xbuf[slot]
        t = _rms(x, g_ref[...]).astype(BF16)
        gates = gate_ref[...]
        y = x
        for wgu_ref, wd_ref, col in ((wgu_a_ref, wd_a_ref, 0), (wgu_b_ref, wd_b_ref, 1)):
            hgu = jnp.dot(t, wgu_ref[0], preferred_element_type=F32)
            hg, hu = hgu[:, 0:D_EXPERT], hgu[:, D_EXPERT:2 * D_EXPERT]
            act = (hg * jax.nn.sigmoid(hg) * hu * gates[:, col:col + 1]).astype(BF16)
            y = y + jnp.dot(act, wd_ref[0], preferred_element_type=F32)
        if final_norm:
            y = _rms(y, gf_ref[...])
        obuf[slot] = y

    def scatter_body(r, carry):
        scatter_row(idx_ref[slot, r], slot, r).start()
        return carry
    lax.fori_loop(0, n_valid, scatter_body, 0)

    @pl.when(i + 2 < n_tiles)
    def _():
        idx_copy(i + 2, slot).start()

    @pl.when(i == n_tiles - 1)
    def _():
        if n_tiles > 1:
            wait_scatter(other, valid_ref[jnp.maximum(i - 1, 0)])
        wait_scatter(slot, n_valid)


def _moe_plan(route, counts, n_tok):
    tms = MOE_TILE
    n_b = N_GROUPS * PAIRS_PER_GROUP
    n_tiles = n_tok // tms + n_b
    bucket = route[:, 0].astype(jnp.int32)
    rank = route[:, 1].astype(jnp.int32)
    cnt = counts[0, :n_b].astype(jnp.int32)
    padded = (cnt + tms - 1) // tms * tms
    pad_end = jnp.cumsum(padded)
    pad_off = pad_end - padded
    dest = pad_off[bucket] + rank
    payload = jnp.concatenate([jnp.arange(n_tok, dtype=F32)[:, None], route[:, 2:4]], axis=1)
    plan = jnp.zeros((n_tiles * tms, 3), F32).at[dest].set(payload, unique_indices=True)
    src = plan[:, 0].astype(jnp.int32).reshape(n_tiles, tms)
    gates = plan[:, 1:3]
    start = jnp.arange(n_tiles, dtype=jnp.int32) * tms
    tb = jnp.minimum(jnp.sum(pad_end[None, :] <= start[:, None], axis=1), n_b - 1).astype(jnp.int32)
    valid = jnp.clip(cnt[tb] - (start - pad_off[tb]), 0, tms).astype(jnp.int32)
    lo_hi = [(a, b) for a in range(EXPERTS_PER_GROUP) for b in range(a + 1, EXPERTS_PER_GROUP)]
    ea_tab = np.array([EXPERTS_PER_GROUP * g + a for g in range(N_GROUPS) for a, _ in lo_hi], np.int32)
    eb_tab = np.array([EXPERTS_PER_GROUP * g + b for g in range(N_GROUPS) for _, b in lo_hi], np.int32)
    return src, gates, jnp.asarray(ea_tab)[tb], jnp.asarray(eb_tab)[tb], valid


def _moe(x2d, route, counts, g, wgu, wd, gf, final_norm):
    n_tok, d = x2d.shape
    tms = MOE_TILE
    src, gates, ea, eb, valid = _moe_plan(route, counts, n_tok)
    n_tiles = src.shape[0]
    any_spec = pl.BlockSpec(memory_space=pl.ANY)
    grid_spec = pltpu.PrefetchScalarGridSpec(
        num_scalar_prefetch=3,
        grid=(n_tiles,),
        in_specs=[any_spec, any_spec,
                  pl.BlockSpec((tms, 2), lambda i, ea_r, eb_r, v_r: (i, 0)),
                  pl.BlockSpec(g.shape, lambda i, ea_r, eb_r, v_r: (0, 0)),
                  pl.BlockSpec((1, d, 2 * D_EXPERT), lambda i, ea_r, eb_r, v_r: (ea_r[i], 0, 0)),
                  pl.BlockSpec((1, d, 2 * D_EXPERT), lambda i, ea_r, eb_r, v_r: (eb_r[i], 0, 0)),
                  pl.BlockSpec((1, D_EXPERT, d), lambda i, ea_r, eb_r, v_r: (ea_r[i], 0, 0)),
                  pl.BlockSpec((1, D_EXPERT, d), lambda i, ea_r, eb_r, v_r: (eb_r[i], 0, 0)),
                  pl.BlockSpec(gf.shape, lambda i, ea_r, eb_r, v_r: (0, 0))],
        out_specs=any_spec,
        scratch_shapes=[pltpu.SMEM((2, tms), jnp.int32), pltpu.VMEM((2, tms, d), F32), pltpu.VMEM((2, tms, d), F32),
                        pltpu.SemaphoreType.DMA((2,)), pltpu.SemaphoreType.DMA((2,)),
                        pltpu.SemaphoreType.DMA((2,))],
    )
    return pl.pallas_call(
        functools.partial(_moe_kernel, n_tiles=n_tiles, final_norm=final_norm),
        out_shape=jax.ShapeDtypeStruct((n_tok, d), F32),
        grid_spec=grid_spec,
        compiler_params=_cparams(("arbitrary",)),
        name="moe",
    )(ea, eb, valid, src, x2d, gates, g, wgu, wgu, wd, wd, gf)


def _row(v):
    return v.reshape(1, -1).astype(F32)


def _rope_tables(seq_len, rot_dim, lane_starts):
    inv = ROPE_THETA ** (-jnp.arange(0, rot_dim, 2, dtype=F32) / rot_dim)
    ang = jnp.arange(seq_len, dtype=F32)[:, None] * inv[None, :]
    cos, sin = jnp.cos(ang), jnp.sin(ang)
    ct = jnp.ones((seq_len, LANES), F32)
    st = jnp.zeros((seq_len, LANES), F32)
    for l0 in lane_starts:
        ct = ct.at[:, l0:l0 + rot_dim].set(jnp.concatenate([cos, cos], axis=1))
        st = st.at[:, l0:l0 + rot_dim].set(jnp.concatenate([-sin, sin], axis=1))
    return ct, st


def _prep_even(w_in, w_uq, w_ukv, w_o):
    c = np.cumsum([0, MLA_Q_RANK, MLA_KV_RANK, MLA_ROPE, DIFF_QK, DIFF_QK, DIFF_VW])
    kpe = jnp.zeros((D_MODEL, LANES), F32).at[:, MLA_NOPE:MLA_NOPE + MLA_ROPE].set(w_in[:, c[2]:c[3]])
    win = jnp.concatenate([w_in[:, c[0]:c[2]], w_in[:, c[3]:c[6]], kpe], axis=1).astype(BF16)
    uq = w_uq.reshape(MLA_Q_RANK, MLA_HEADS, MLA_NOPE + MLA_ROPE)
    uq = jnp.pad(uq, ((0, 0), (0, 0), (0, LANES - MLA_NOPE - MLA_ROPE))).reshape(MLA_Q_RANK, MLA_HEADS * LANES)
    ukv = w_ukv.reshape(MLA_KV_RANK, MLA_HEADS, MLA_NOPE + MLA_V)
    uk = jnp.pad(ukv[:, :, :MLA_NOPE], ((0, 0), (0, 0), (0, LANES - MLA_NOPE))).reshape(MLA_KV_RANK,
                                                                                       MLA_HEADS * LANES)
    uv = ukv[:, :, MLA_NOPE:].reshape(MLA_KV_RANK, MLA_HEADS * MLA_V)
    n_mla = MLA_HEADS * MLA_V
    return (win, uq.astype(BF16), uk.astype(BF16), uv.astype(BF16),
            w_o[:n_mla].astype(BF16), w_o[n_mla:].astype(BF16))


def _prep_moe(w_rg, b_rg, w_re, b_re, w_gate, w_up, w_down):
    wr = jnp.zeros((D_MODEL, LANES), F32).at[:, :N_GROUPS].set(w_rg).at[:, N_GROUPS:N_GROUPS + N_EXPERTS].set(w_re)
    br = jnp.zeros((1, LANES), F32).at[0, :N_GROUPS].set(b_rg).at[0, N_GROUPS:N_GROUPS + N_EXPERTS].set(b_re)
    wr_hi = wr.astype(BF16)
    wr_lo = (wr - wr_hi.astype(F32)).astype(BF16)
    wgu = jnp.concatenate([w_gate, w_up], axis=-1).reshape(N_EXPERTS, D_MODEL, 2 * D_EXPERT).astype(BF16)
    wd = w_down.reshape(N_EXPERTS, D_EXPERT, D_MODEL).astype(BF16)
    return wr_hi, wr_lo, br, wgu, wd


def _trunk(x, mem, p, tables):
    b, s, d = x.shape
    depth = p['ln_mix'].shape[0]
    for layer in range(depth):
        i = layer // 2
        if layer % 2 == 0:
            win, uq, uk, uv, wo_a, wo_b = _prep_even(p['w_in_even'][i], p['mla_w_uq'][i], p['mla_w_ukv'][i],
                                                     p['w_o_even'][i])
            cm, sm, cd, sd = tables[s]
            q, k, v, dq, dk, dv = _even_in(x, _row(p['ln_mix'][layer]), win, _row(p['mla_q_norm'][i]),
                                           _row(p['mla_kv_norm'][i]), uq, uk, uv, cm, sm, cd, sd)
            o_mla = _flash(q, k, v, 2 * LANES)
            lamp = jnp.zeros((8, LANES), F32)
            for r, name in enumerate(('diff_lambda_q1', 'diff_lambda_k1', 'diff_lambda_q2', 'diff_lambda_k2')):
                lamp = lamp.at[r, :DIFF_HEAD_DIM].set(p[name][i])
            lam_init = 0.8 - 0.6 * math.exp(-0.3 * layer)
            o_diff = _flash(dq, dk, dv, LANES, lamp=lamp, subln=_row(p['diff_subln'][i]), lam_init=lam_init)
            a_list, w_list = [o_mla, o_diff], [wo_a, wo_b]
        else:
            q, k, v = _odd_in(x, _row(p['ln_mix'][layer]), p['w_in_odd'][i].astype(BF16))
            tbl, idx = _na_tables(p['na_rpb'][i])
            a_list, w_list = [_na_attention(q, k, v, tbl, idx)], [p['w_o_odd'][i].astype(BF16)]
        mk, mv = _mem_kv(mem, _row(p['ln_mem'][layer]), p['xa_w_kv'][layer].astype(BF16))
        wr_hi, wr_lo, br, wgu, wd = _prep_moe(p['moe_w_group'][layer], p['moe_b_group'][layer],
                                              p['moe_w_expert'][layer], p['moe_b_expert'][layer],
                                              p['moe_w_gate'][layer], p['moe_w_up'][layer], p['moe_w_down'][layer])
        x, route, counts = _post_mix(x, a_list, w_list, _row(p['ln_xattn'][layer]),
                                     p['xa_w_q'][layer].astype(BF16), mk, mv, p['xa_w_o'][layer].astype(BF16),
                                     _row(p['ln_ffn'][layer]), wr_hi, wr_lo, br)
        x = _moe(x.reshape(b * s, d), route.reshape(b * s, LANES), counts, _row(p['ln_ffn'][layer]), wgu, wd,
                 _row(p['ln_final']), final_norm=(layer == depth - 1)).reshape(b, s, d)
    return x


def kernel(x_prompt, x_sample, mem_prompt, mem_sample, ln_mix, w_in_even, mla_q_norm, mla_kv_norm, mla_w_uq, mla_w_ukv, diff_lambda_q1, diff_lambda_k1, diff_lambda_q2, diff_lambda_k2, diff_subln, w_o_even, w_in_odd, na_rpb, w_o_odd, ln_xattn, ln_mem, xa_w_q, xa_w_kv, xa_w_o, ln_ffn, moe_w_group, moe_b_group, moe_w_expert, moe_b_expert, moe_w_gate, moe_w_up, moe_w_down, ln_final):
    p = dict(ln_mix=ln_mix, w_in_even=w_in_even, mla_q_norm=mla_q_norm, mla_kv_norm=mla_kv_norm,
             mla_w_uq=mla_w_uq, mla_w_ukv=mla_w_ukv, diff_lambda_q1=diff_lambda_q1,
             diff_lambda_k1=diff_lambda_k1, diff_lambda_q2=diff_lambda_q2, diff_lambda_k2=diff_lambda_k2,
             diff_subln=diff_subln, w_o_even=w_o_even, w_in_odd=w_in_odd, na_rpb=na_rpb, w_o_odd=w_o_odd,
             ln_xattn=ln_xattn, ln_mem=ln_mem, xa_w_q=xa_w_q, xa_w_kv=xa_w_kv, xa_w_o=xa_w_o,
             ln_ffn=ln_ffn, moe_w_group=moe_w_group, moe_b_group=moe_b_group, moe_w_expert=moe_w_expert,
             moe_b_expert=moe_b_expert, moe_w_gate=moe_w_gate, moe_w_up=moe_w_up, moe_w_down=moe_w_down,
             ln_final=ln_final)
    tables = {}
    for s in {x_prompt.shape[1], x_sample.shape[1]}:
        cm, sm = _rope_tables(s, MLA_ROPE, (MLA_NOPE,))
        cd, sd = _rope_tables(s, DIFF_ROT, (0, DIFF_HEAD_DIM))
        tables[s] = (cm, sm, cd, sd)
    return (_trunk(x_prompt, mem_prompt, p, tables), _trunk(x_sample, mem_sample, p, tables))
```

```python
import functools
import math

import numpy as np
import jax
import jax.numpy as jnp
from jax import lax
from jax.experimental import pallas as pl
from jax.experimental.pallas import tpu as pltpu

F32 = jnp.float32
BF16 = jnp.bfloat16

D_MODEL = 1024
ROPE_THETA = 500000.0
NORM_EPS = 1e-6
NEG_INF = -1e30
LOG2E = math.log2(math.e)

MLA_HEADS = 8
MLA_Q_RANK = 384
MLA_KV_RANK = 256
MLA_NOPE = 64
MLA_ROPE = 32
MLA_V = 64
DIFF_HEADS = 4
DIFF_HEAD_DIM = 64
DIFF_ROT = DIFF_HEAD_DIM // 4
DIFF_QK = DIFF_HEADS * 2 * DIFF_HEAD_DIM
DIFF_VW = DIFF_HEADS * 2 * DIFF_HEAD_DIM
GRID_W = 64
NA_HEADS = 16
NA_HEAD_DIM = 64
WIN_ROWS = 8
WIN_COLS = 16
MEM_TOKENS = 256
XA_HEADS = 4
XA_HEAD_DIM = 128
XA_INNER = XA_HEADS * XA_HEAD_DIM
N_GROUPS = 4
EXPERTS_PER_GROUP = 4
N_EXPERTS = N_GROUPS * EXPERTS_PER_GROUP
PAIRS_PER_GROUP = EXPERTS_PER_GROUP * (EXPERTS_PER_GROUP - 1) // 2
D_EXPERT = 256

LANES = 128
SUBLANES = 8
VMEM_LIMIT = 56 * 1024 * 1024

TOKEN_TILE = 512
MOE_TILE = 256
MOE_IDX_SLOTS = 4
FLASH_TQ = 512
FLASH_TK = 512
FLASH_BOUND_SLACK = 1.01
FLASH_MIN_ROW_SUM = 2.0 ** -60
FLASH_UNROLL = 8
NA_ROW_BLOCK = 8
NA_KEY_ROWS = 2 * WIN_ROWS
NA_GROUP_ROWS = 4
NA_TILE_CODES = 3
NA_TILES_PER_HEAD = NA_TILE_CODES * 16 + 1


def _cparams(sem):
    return pltpu.CompilerParams(dimension_semantics=sem, vmem_limit_bytes=VMEM_LIMIT)


def _rms(xf, g):
    return xf * lax.rsqrt(jnp.mean(xf * xf, axis=-1, keepdims=True) + NORM_EPS) * g


def _full(shape):
    nd = len(shape)
    return pl.BlockSpec(shape, lambda *_: (0,) * nd)


def _rope_block(x, cos, sin, half):
    lane = lax.broadcasted_iota(jnp.int32, x.shape, 1)
    up = pltpu.roll(x, LANES - half, 1)
    down = pltpu.roll(x, half, 1)
    first = lax.bitwise_and(lane, 2 * half - 1) < half
    return x * cos + jnp.where(first, up, down) * sin


def _even_in_kernel(x_ref, g_ref, win_ref, qn_ref, kvn_ref, wuq_ref, wuk_ref, wuv_ref,
                    cm_ref, sm_ref, cd_ref, sd_ref,
                    q_ref, k_ref, v_ref, dq_ref, dk_ref, dv_ref):
    h = _rms(x_ref[0], g_ref[...]).astype(BF16)
    proj = jnp.dot(h, win_ref[...], preferred_element_type=F32)
    o_cq, o_ckv, o_dq, o_dk, o_dv, o_kpe = 0, 384, 640, 1152, 1664, 2176
    cm, sm, cd, sd = cm_ref[...], sm_ref[...], cd_ref[...], sd_ref[...]

    cq = _rms(proj[:, o_cq:o_cq + MLA_Q_RANK], qn_ref[...]).astype(BF16)
    q = jnp.dot(cq, wuq_ref[...], preferred_element_type=F32)
    mla_c = (MLA_NOPE + MLA_ROPE) ** -0.5 * LOG2E
    for hd in range(MLA_HEADS):
        sl = slice(hd * LANES, (hd + 1) * LANES)
        q_ref[0, :, sl] = (_rope_block(q[:, sl], cm, sm, MLA_ROPE // 2) * mla_c).astype(BF16)

    ckv = _rms(proj[:, o_ckv:o_ckv + MLA_KV_RANK], kvn_ref[...]).astype(BF16)
    kn = jnp.dot(ckv, wuk_ref[...], preferred_element_type=F32)
    kpe = _rope_block(proj[:, o_kpe:o_kpe + LANES], cm, sm, MLA_ROPE // 2)
    for hd in range(MLA_HEADS):
        sl = slice(hd * LANES, (hd + 1) * LANES)
        k_ref[0, :, sl] = (kn[:, sl] + kpe).astype(BF16)
    v_ref[0] = jnp.dot(ckv, wuv_ref[...], preferred_element_type=F32).astype(BF16)

    diff_c = DIFF_HEAD_DIM ** -0.5 * LOG2E
    for hd in range(DIFF_HEADS):
        sl = slice(hd * LANES, (hd + 1) * LANES)
        dq_ref[0, :, sl] = (_rope_block(proj[:, o_dq + hd * LANES:o_dq + (hd + 1) * LANES], cd, sd,
                                        DIFF_ROT // 2) * diff_c).astype(BF16)
        dk_ref[0, :, sl] = _rope_block(proj[:, o_dk + hd * LANES:o_dk + (hd + 1) * LANES], cd, sd,
                                       DIFF_ROT // 2).astype(BF16)
    dv_ref[0] = proj[:, o_dv:o_dv + DIFF_VW].astype(BF16)


def _even_in(x, g, win, qn, kvn, wuq, wuk, wuv, cm, sm, cd, sd):
    b, s, d = x.shape
    tm = TOKEN_TILE
    tok = lambda w: pl.BlockSpec((1, tm, w), lambda i, j: (i, j, 0))
    tab = pl.BlockSpec((tm, LANES), lambda i, j: (j, 0))
    outs = [jax.ShapeDtypeStruct((b, s, w), BF16) for w in (1024, 1024, 512, 512, 512, 512)]
    return pl.pallas_call(
        _even_in_kernel,
        out_shape=outs,
        grid=(b, s // tm),
        in_specs=[tok(d), _full(g.shape), _full(win.shape), _full(qn.shape), _full(kvn.shape),
                  _full(wuq.shape), _full(wuk.shape), _full(wuv.shape), tab, tab, tab, tab],
        out_specs=[tok(1024), tok(1024), tok(512), tok(512), tok(512), tok(512)],
        compiler_params=_cparams(("parallel", "parallel")),
        name="even_in",
    )(x, g, win, qn, kvn, wuq, wuk, wuv, cm, sm, cd, sd)


def _flash_kernel(*refs, tq, tk, n_k, width, diff, lam_init):
    if diff:
        q_ref, k_ref, v_ref, lamp_ref, g_ref, o_ref, qs_ref, m_ref, l_ref, acc_ref, kn_ref, kmax_ref = refs
    else:
        q_ref, k_ref, v_ref, o_ref, qs_ref, m_ref, l_ref, acc_ref, kn_ref, kmax_ref = refs
    half = width // 2
    n_j = tk // LANES
    q = q_ref[0]
    lane = lax.broadcasted_iota(jnp.int32, (tq, width), 1)
    zero = jnp.zeros_like(q)
    qs_ref[0:tq, :] = jnp.where(lane < half, q, zero)
    qs_ref[tq:2 * tq, :] = jnp.where(lane >= half, q, zero)

    @pl.when(pl.program_id(2) == 0)
    def _():
        r_i = lax.broadcasted_iota(jnp.int32, (width, LANES), 0)
        c_i = lax.broadcasted_iota(jnp.int32, (width, LANES), 1)
        sel = jnp.where(((c_i == 0) & (r_i < half)) | ((c_i == 1) & (r_i >= half)), 1.0, 0.0).astype(BF16)
        kn_ref[...] = jnp.zeros(kn_ref.shape, F32)

        def knorm(i, carry):
            kc = k_ref[0, pl.ds(pl.multiple_of(i * tk, tk), tk), :].astype(F32)
            kn_ref[...] = jnp.maximum(kn_ref[...], jnp.dot((kc * kc).astype(BF16), sel, preferred_element_type=F32))
            return carry

        lax.fori_loop(0, n_k, knorm, 0)
        kmax = jnp.sqrt(jnp.max(kn_ref[...], axis=0, keepdims=True))
        kmax_ref[0:1, :] = jnp.broadcast_to(kmax[:, 0:1], (1, LANES))
        kmax_ref[1:2, :] = jnp.broadcast_to(kmax[:, 1:2], (1, LANES))

    def reset(m):
        m_ref[...] = m
        l_ref[...] = jnp.zeros(l_ref.shape, F32)
        acc_ref[...] = jnp.zeros(acc_ref.shape, F32)

    def scores(i):
        off = pl.multiple_of(i * tk, tk)
        kc = k_ref[0, pl.ds(off, tk), :]
        return off, lax.dot_general(qs_ref[...], kc, (((1,), (1,)), ((), ())), preferred_element_type=F32)

    def max_pass(i, carry):
        _, s = scores(i)
        m = m_ref[...]
        for j in range(n_j):
            m = jnp.maximum(m, s[:, j * LANES:(j + 1) * LANES])
        m_ref[...] = m
        return carry

    def pv_pass(i, carry):
        off, s = scores(i)
        vc = v_ref[0, pl.ds(off, tk), :]
        m = m_ref[...]
        ps = [jnp.exp2(s[:, j * LANES:(j + 1) * LANES] - m) for j in range(n_j)]
        lsum = ps[0]
        for pj in ps[1:]:
            lsum = lsum + pj
        l_ref[...] += lsum
        p = jnp.concatenate([pj.astype(BF16) for pj in ps], axis=1)
        acc_ref[...] += jnp.dot(p, vc, preferred_element_type=F32)
        return carry

    qf = qs_ref[...].astype(F32)
    qn = jnp.sqrt(jnp.sum(qf * qf, axis=1, keepdims=True))
    bound_top = qn[0:tq] * kmax_ref[0:1, :] * FLASH_BOUND_SLACK
    bound_bot = qn[tq:2 * tq] * kmax_ref[1:2, :] * FLASH_BOUND_SLACK
    reset(jnp.concatenate([bound_top, bound_bot], axis=0))
    lax.fori_loop(0, n_k, pv_pass, 0, unroll=FLASH_UNROLL)
    l_min = jnp.min(jnp.sum(l_ref[...], axis=1, keepdims=True))

    @pl.when(jnp.logical_not(l_min >= FLASH_MIN_ROW_SUM))
    def _():
        reset(jnp.full(m_ref.shape, NEG_INF, F32))
        lax.fori_loop(0, n_k, max_pass, 0, unroll=FLASH_UNROLL)
        m_ref[...] = jnp.broadcast_to(jnp.max(m_ref[...], axis=1, keepdims=True), m_ref.shape)
        lax.fori_loop(0, n_k, pv_pass, 0, unroll=FLASH_UNROLL)

    o = acc_ref[...] / jnp.sum(l_ref[...], axis=1, keepdims=True)
    top, bot = o[0:tq], o[tq:2 * tq]
    if diff:
        lp = lamp_ref[...]
        lam = (jnp.exp(jnp.sum(lp[0:1] * lp[1:2], axis=1, keepdims=True))
               - jnp.exp(jnp.sum(lp[2:3] * lp[3:4], axis=1, keepdims=True)) + lam_init)
        od = top - lam * bot
        o_ref[0] = (_rms(od, g_ref[...]) * (1.0 - lam_init)).astype(BF16)
    else:
        lane_o = lax.broadcasted_iota(jnp.int32, (tq, LANES), 1)
        o_ref[0] = jnp.where(lane_o < LANES // 2, top, bot).astype(BF16)


def _flash(q, k, v, width, lamp=None, subln=None, lam_init=0.0):
    b, s, _ = q.shape
    n = q.shape[2] // width
    tq, tk = FLASH_TQ, FLASH_TK
    diff = lamp is not None
    kern = functools.partial(_flash_kernel, tq=tq, tk=tk, n_k=s // tk, width=width, diff=diff,
                             lam_init=lam_init)
    in_specs = [pl.BlockSpec((1, tq, width), lambda bi, p, qi: (bi, qi, p)),
                pl.BlockSpec((1, s, width), lambda bi, p, qi: (bi, 0, p)),
                pl.BlockSpec((1, s, LANES), lambda bi, p, qi: (bi, 0, p))]
    args = [q, k, v]
    if diff:
        in_specs += [_full(lamp.shape), _full(subln.shape)]
        args += [lamp, subln]
    return pl.pallas_call(
        kern,
        out_shape=jax.ShapeDtypeStruct((b, s, n * LANES), BF16),
        grid=(b, n, s // tq),
        in_specs=in_specs,
        out_specs=pl.BlockSpec((1, tq, LANES), lambda bi, p, qi: (bi, qi, p)),
        scratch_shapes=[pltpu.VMEM((2 * tq, width), BF16), pltpu.VMEM((2 * tq, LANES), F32),
                        pltpu.VMEM((2 * tq, LANES), F32), pltpu.VMEM((2 * tq, LANES), F32),
                        pltpu.VMEM((tk, LANES), F32), pltpu.VMEM((SUBLANES, LANES), F32)],
        compiler_params=_cparams(("parallel", "parallel", "arbitrary")),
        name="flash_diff" if diff else "flash_mla",
    )(*args)


def _odd_in_kernel(x_ref, g_ref, w_ref, q_ref, k_ref, v_ref):
    h = _rms(x_ref[0], g_ref[...]).astype(BF16)
    qkv = jnp.dot(h, w_ref[...], preferred_element_type=F32)
    c = NA_HEAD_DIM ** -0.5 * LOG2E
    q_ref[0] = (qkv[:, 0:1024] * c).astype(BF16)
    k_ref[0] = qkv[:, 1024:2048].astype(BF16)
    v_ref[0] = qkv[:, 2048:3072].astype(BF16)


def _odd_in(x, g, w):
    b, s, d = x.shape
    tm = TOKEN_TILE
    tok = pl.BlockSpec((1, tm, d), lambda i, j: (i, j, 0))
    return pl.pallas_call(
        _odd_in_kernel,
        out_shape=[jax.ShapeDtypeStruct((b, s, d), BF16)] * 3,
        grid=(b, s // tm),
        in_specs=[tok, _full(g.shape), _full(w.shape)],
        out_specs=[tok, tok, tok],
        compiler_params=_cparams(("parallel", "parallel")),
        name="odd_in",
    )(x, g, w)


def _na_kernel(idx_ref, q_ref, k_ref, v_ref, tbl_ref, o_ref, qs_ref, os_ref, *, rows, n_rb):
    nq = NA_ROW_BLOCK * GRID_W
    nk = NA_KEY_ROWS * GRID_W
    gq = NA_GROUP_ROWS * GRID_W
    n_kp = NA_KEY_ROWS // 2
    rb = pl.program_id(2)
    ks = jnp.clip(NA_ROW_BLOCK * rb - WIN_ROWS // 2, 0, rows - NA_KEY_ROWS)
    pat = jnp.where(rb == 0, 0, jnp.where(rb == n_rb - 1, 2, 1))
    off = pl.multiple_of(ks * GRID_W, (WIN_ROWS // 2) * GRID_W)
    q = q_ref[0]
    lane = lax.broadcasted_iota(jnp.int32, (nq, LANES), 1)
    zero = jnp.zeros_like(q)
    qs_ref[0:nq, :] = jnp.where(lane < LANES // 2, q, zero)
    qs_ref[nq:2 * nq, :] = jnp.where(lane >= LANES // 2, q, zero)

    def group(g, carry):
        r0 = pl.multiple_of(g * gq, gq)
        per_head = NA_ROW_BLOCK // NA_GROUP_ROWS
        hh = g // per_head
        i0 = (g % per_head) * NA_GROUP_ROWS
        kw = k_ref[0, pl.ds(off, nk), :]
        vw = v_ref[0, pl.ds(off, nk), :]
        s = lax.dot_general(qs_ref[pl.ds(r0, gq), :], kw, (((1,), (1,)), ((), ())), preferred_element_type=F32)
        bias_rows = []
        for ii in range(NA_GROUP_ROWS):
            base = pat * (NA_ROW_BLOCK * n_kp) + (i0 + ii) * n_kp
            bias_rows.append(jnp.concatenate(
                [tbl_ref[0, hh * NA_TILES_PER_HEAD + idx_ref[base + kp]] for kp in range(n_kp)], axis=1))
        s = s + jnp.concatenate(bias_rows, axis=0)
        m = jnp.max(s, axis=1, keepdims=True)
        p = jnp.exp2(s - m)
        linv = 1.0 / jnp.sum(p, axis=1, keepdims=True)
        os_ref[pl.ds(r0, gq), :] = jnp.dot(p.astype(BF16), vw, preferred_element_type=F32) * linv
        return carry

    lax.fori_loop(0, 2 * nq // gq, group, 0, unroll=True)
    o_ref[0] = jnp.where(lane < LANES // 2, os_ref[0:nq, :], os_ref[nq:2 * nq, :]).astype(BF16)


def _na_tables(rpb):
    col = np.arange(GRID_W)
    col_start = np.clip(col - WIN_COLS // 2, 0, GRID_W - WIN_COLS)
    col_valid = (col[None, :] >= col_start[:, None]) & (col[None, :] < col_start[:, None] + WIN_COLS)
    col_idx = np.clip(col[None, :] - col[:, None], -(WIN_COLS - 1), WIN_COLS - 1) + WIN_COLS - 1
    n_ri = 2 * WIN_ROWS - 1
    ri = np.zeros((NA_TILES_PER_HEAD, LANES), np.int32)
    ok = np.zeros((NA_TILES_PER_HEAD, LANES), bool)
    for code in range(NA_TILE_CODES):
        for a in range(-1, n_ri):
            t = code * 16 + a + 1
            ri[t, :GRID_W] = np.clip(a, 0, n_ri - 1)
            ri[t, GRID_W:] = np.clip(a + 1, 0, n_ri - 1)
            ok[t, :GRID_W] = code in (0, 1) and 0 <= a < n_ri
            ok[t, GRID_W:] = code in (0, 2) and 0 <= a + 1 < n_ri
    onehot = (col_idx.reshape(-1)[None, :] == np.arange(2 * WIN_COLS - 1)[:, None]).astype(np.float32)
    band = jnp.dot(rpb.reshape(NA_HEADS * n_ri, 2 * WIN_COLS - 1).astype(F32), jnp.asarray(onehot),
                   precision=lax.Precision.HIGHEST).reshape(NA_HEADS, n_ri, GRID_W, GRID_W)
    band = jnp.where(col_valid[None, None], band * LOG2E, NEG_INF)
    masked = jnp.full((NA_HEADS, GRID_W, GRID_W), NEG_INF, F32)
    tiles = []
    for t in range(NA_TILES_PER_HEAD):
        left = band[:, ri[t, 0]] if ok[t, 0] else masked
        right = band[:, ri[t, GRID_W]] if ok[t, GRID_W] else masked
        tiles.append(jnp.concatenate([left, right], axis=-1))
    tbl = jnp.stack(tiles, axis=1).reshape(NA_HEADS // 2, 2 * NA_TILES_PER_HEAD, GRID_W, LANES)

    idx = np.zeros((3, NA_ROW_BLOCK, NA_KEY_ROWS // 2), np.int32)
    for pat in range(3):
        for i in range(NA_ROW_BLOCK):
            rel = (i, i + WIN_ROWS // 2, i + WIN_ROWS)[pat]
            start = (max(i - WIN_ROWS // 2, 0), i, min(i + WIN_ROWS // 2, WIN_ROWS))[pat]
            for kp in range(NA_KEY_ROWS // 2):
                kk = 2 * kp
                a = kk - rel + WIN_ROWS - 1
                v0 = start <= kk < start + WIN_ROWS
                v1 = start <= kk + 1 < start + WIN_ROWS
                if v0:
                    assert 0 <= a < n_ri
                if v1:
                    assert 0 <= a + 1 < n_ri
                if v0 and v1:
                    idx[pat, i, kp] = a + 1
                elif v0:
                    idx[pat, i, kp] = 16 + a + 1
                elif v1:
                    idx[pat, i, kp] = 32 + a + 1
                else:
                    idx[pat, i, kp] = NA_TILES_PER_HEAD - 1
    return tbl, jnp.asarray(idx.reshape(-1))


def _na_attention(q, k, v, tbl, idx):
    b, s, d = q.shape
    rows = s // GRID_W
    n_rb = rows // NA_ROW_BLOCK
    nq = NA_ROW_BLOCK * GRID_W
    nk = NA_KEY_ROWS * GRID_W
    kern = functools.partial(_na_kernel, rows=rows, n_rb=n_rb)
    grid_spec = pltpu.PrefetchScalarGridSpec(
        num_scalar_prefetch=1,
        grid=(NA_HEADS // 2, b, n_rb),
        in_specs=[pl.BlockSpec((1, nq, LANES), lambda p, bi, r, idx_r: (bi, r, p)),
                  pl.BlockSpec((1, s, LANES), lambda p, bi, r, idx_r: (bi, 0, p)),
                  pl.BlockSpec((1, s, LANES), lambda p, bi, r, idx_r: (bi, 0, p)),
                  pl.BlockSpec((1, 2 * NA_TILES_PER_HEAD, GRID_W, LANES), lambda p, bi, r, idx_r: (p, 0, 0, 0))],
        out_specs=pl.BlockSpec((1, nq, LANES), lambda p, bi, r, idx_r: (bi, r, p)),
        scratch_shapes=[pltpu.VMEM((2 * nq, LANES), BF16), pltpu.VMEM((2 * nq, LANES), F32)],
    )
    return pl.pallas_call(
        kern,
        out_shape=jax.ShapeDtypeStruct((b, s, d), BF16),
        grid_spec=grid_spec,
        compiler_params=_cparams(("parallel", "parallel", "arbitrary")),
        name="na_attention",
    )(idx, q, k, v, tbl)


def _mem_kv_kernel(m_ref, g_ref, w_ref, k_ref, v_ref):
    h = _rms(m_ref[0], g_ref[...]).astype(BF16)
    kv = jnp.dot(h, w_ref[...], preferred_element_type=F32)
    k_ref[0] = kv[:, 0:XA_INNER].astype(BF16)
    v_ref[0] = kv[:, XA_INNER:2 * XA_INNER].astype(BF16)


def _mem_kv(mem, g, w):
    b, m, d = mem.shape
    return pl.pallas_call(
        _mem_kv_kernel,
        out_shape=[jax.ShapeDtypeStruct((b, m, XA_INNER), BF16)] * 2,
        grid=(b,),
        in_specs=[pl.BlockSpec((1, m, d), lambda i: (i, 0, 0)), _full(g.shape), _full(w.shape)],
        out_specs=[pl.BlockSpec((1, m, XA_INNER), lambda i: (i, 0, 0))] * 2,
        compiler_params=_cparams(("parallel",)),
        name="mem_kv",
    )(mem, g, w)


def _post_mix_kernel(*refs, n_a):
    x_ref = refs[0]
    a_refs = refs[1:1 + n_a]
    w_refs = refs[1 + n_a:1 + 2 * n_a]
    (g_ref, wq_ref, mk_ref, mv_ref, wo_ref, gffn_ref, wr_hi_ref, wr_lo_ref, br_ref, tri_ref,
     o_ref, route_ref, cnt_ref) = refs[1 + 2 * n_a:]

    @pl.when((pl.program_id(0) == 0) & (pl.program_id(1) == 0))
    def _():
        cnt_ref[...] = jnp.zeros(cnt_ref.shape, F32)

    x1 = x_ref[0]
    for a_ref, w_ref in zip(a_refs, w_refs):
        x1 = x1 + jnp.dot(a_ref[0], w_ref[...], preferred_element_type=F32)
    h = _rms(x1, g_ref[...]).astype(BF16)
    c = XA_HEAD_DIM ** -0.5 * LOG2E
    q = (jnp.dot(h, wq_ref[...], preferred_element_type=F32) * c).astype(BF16)
    outs = []
    for hd in range(XA_HEADS):
        sl = slice(hd * XA_HEAD_DIM, (hd + 1) * XA_HEAD_DIM)
        s = lax.dot_general(q[:, sl], mk_ref[0, :, sl], (((1,), (1,)), ((), ())), preferred_element_type=F32)
        p = jnp.exp2(s - jnp.max(s, axis=1, keepdims=True))
        linv = 1.0 / jnp.sum(p, axis=1, keepdims=True)
        o = jnp.dot(p.astype(BF16), mv_ref[0, :, sl], preferred_element_type=F32) * linv
        outs.append(o.astype(BF16))
    o_all = jnp.concatenate(outs, axis=1)
    x2 = x1 + jnp.dot(o_all, wo_ref[...], preferred_element_type=F32)
    o_ref[0] = x2

    bucket, w_a, w_b = _route(_rms(x2, gffn_ref[...]), wr_hi_ref, wr_lo_ref, br_ref)
    lane = lax.broadcasted_iota(jnp.int32, (x2.shape[0], LANES), 1)
    onehot = lane == bucket
    oh = jnp.where(onehot, 1.0, 0.0)
    before = jnp.dot(tri_ref[...], oh.astype(BF16), preferred_element_type=F32) + cnt_ref[0:1, :]
    rank = jnp.sum(jnp.where(onehot, before, 0.0), axis=1, keepdims=True)
    cnt_ref[0:1, :] += jnp.sum(oh, axis=0, keepdims=True)
    route_ref[0] = (jnp.where(lane == 0, bucket.astype(F32), 0.0) + jnp.where(lane == 1, rank, 0.0)
                    + jnp.where(lane == 2, w_a, 0.0) + jnp.where(lane == 3, w_b, 0.0))


def _post_mix(x, a_list, w_list, g, wq, mk, mv, wo, gffn, wr_hi, wr_lo, br):
    b, s, d = x.shape
    tm = TOKEN_TILE
    n_a = len(a_list)
    tok = lambda w: pl.BlockSpec((1, tm, w), lambda i, j: (i, j, 0))
    memspec = pl.BlockSpec((1, MEM_TOKENS, XA_INNER), lambda i, j: (i, 0, 0))
    tri = jnp.tril(jnp.ones((tm, tm), F32), -1).astype(BF16)
    return pl.pallas_call(
        functools.partial(_post_mix_kernel, n_a=n_a),
        out_shape=[jax.ShapeDtypeStruct((b, s, d), F32), jax.ShapeDtypeStruct((b, s, LANES), F32),
                   jax.ShapeDtypeStruct((8, LANES), F32)],
        grid=(b, s // tm),
        in_specs=([tok(d)] + [tok(a.shape[2]) for a in a_list] + [_full(w.shape) for w in w_list]
                  + [_full(g.shape), _full(wq.shape), memspec, memspec, _full(wo.shape), _full(gffn.shape),
                     _full(wr_hi.shape), _full(wr_lo.shape), _full(br.shape), _full(tri.shape)]),
        out_specs=[tok(d), tok(LANES), _full((8, LANES))],
        compiler_params=_cparams(("arbitrary", "arbitrary")),
        name="post_mix",
    )(x, *a_list, *w_list, g, wq, mk, mv, wo, gffn, wr_hi, wr_lo, br, tri)


def _route(tf, wr_hi_ref, wr_lo_ref, br_ref):
    t_hi = tf.astype(BF16)
    t_lo = (tf - t_hi.astype(F32)).astype(BF16)
    w_hi, w_lo = wr_hi_ref[...], wr_lo_ref[...]
    logits = (jnp.dot(t_hi, w_hi, preferred_element_type=F32) + jnp.dot(t_lo, w_hi, preferred_element_type=F32)
              + jnp.dot(t_hi, w_lo, preferred_element_type=F32)) + br_ref[...]
    lane = lax.broadcasted_iota(jnp.int32, logits.shape, 1)
    big = jnp.int32(LANES)
    is_g = lane < N_GROUPS
    lg = jnp.where(is_g, logits, NEG_INF)
    gmax = jnp.max(lg, axis=1, keepdims=True)
    g_p = 1.0 / jnp.sum(jnp.where(is_g, jnp.exp(lg - gmax), 0.0), axis=1, keepdims=True)
    g_i = jnp.min(jnp.where(lg == gmax, lane, big), axis=1, keepdims=True)
    lo = N_GROUPS + EXPERTS_PER_GROUP * g_i
    in_grp = (lane >= lo) & (lane < lo + EXPERTS_PER_GROUP)
    el = jnp.where(in_grp, logits, NEG_INF)
    v1 = jnp.max(el, axis=1, keepdims=True)
    i1 = jnp.min(jnp.where(el == v1, lane, big), axis=1, keepdims=True)
    el2 = jnp.where(lane == i1, NEG_INF, el)
    v2 = jnp.max(el2, axis=1, keepdims=True)
    i2 = jnp.min(jnp.where(el2 == v2, lane, big), axis=1, keepdims=True)
    e2 = jnp.exp(v2 - v1)
    w1 = g_p / (1.0 + e2)
    w2 = g_p * e2 / (1.0 + e2)
    first_low = i1 < i2
    e_lo = jnp.minimum(i1, i2) - lo
    e_hi = jnp.maximum(i1, i2) - lo
    pair = jnp.where(e_lo == 0, 0, jnp.where(e_lo == 1, 3, 5)) + e_hi - e_lo - 1
    bucket = g_i * PAIRS_PER_GROUP + pair
    return bucket, jnp.where(first_low, w1, w2), jnp.where(first_low, w2, w1)


def _moe_kernel(ea_ref, eb_ref, valid_ref, src_hbm, x_hbm, gate_ref, g_ref, wgu_a_ref, wgu_b_ref,
                wd_a_ref, wd_b_ref, gf_ref, out_hbm, idx_ref, xb0, xb1, ob0, ob1, isem, gsem, ssem, *, n_tiles,
                final_norm):
    n_grp, sub, d = xb0.shape
    tms = n_grp * sub
    i = pl.program_id(0)

    def idx_off(k):
        return lax.rem(k, MOE_IDX_SLOTS) * tms

    def idx_copy(k):
        return pltpu.make_async_copy(src_hbm.at[k], idx_ref.at[pl.ds(idx_off(k), tms)],
                                     isem.at[lax.rem(k, MOE_IDX_SLOTS)])

    def gather_row(tok, xb, s, j, u):
        return pltpu.make_async_copy(x_hbm.at[pl.ds(tok, 1)], xb.at[j, pl.ds(u, 1)], gsem.at[s])

    def scatter_row(tok, ob, s, j, u):
        return pltpu.make_async_copy(ob.at[j, pl.ds(u, 1)], out_hbm.at[pl.ds(tok, 1)], ssem.at[s])

    def start_gather(k, xb, s):
        base = idx_off(k)

        def body(j, carry):
            for u in range(sub):
                gather_row(idx_ref[base + j * sub + u], xb, s, j, u).start()
            return carry
        lax.fori_loop(0, n_grp, body, 0)

    def wait_gather(xb, s):
        def body(j, carry):
            for u in range(sub):
                gather_row(0, xb, s, j, u).wait()
            return carry
        lax.fori_loop(0, n_grp, body, 0)

    def for_rows(n, fn):
        full = lax.shift_right_logical(n, 3)

        def grp(j, carry):
            for u in range(sub):
                fn(j, u)
            return carry
        lax.fori_loop(0, full, grp, 0)

        def one(u, carry):
            fn(full, u)
            return carry
        lax.fori_loop(0, lax.bitwise_and(n, sub - 1), one, 0)

    def start_scatter(k, n, ob, s):
        base = idx_off(k)
        for_rows(n, lambda j, u: scatter_row(idx_ref[base + j * sub + u], ob, s, j, u).start())

    def wait_scatter(n, ob, s):
        for_rows(n, lambda j, u: scatter_row(0, ob, s, j, u).wait())

    def tile(s):
        xb, ob = (xb0, ob0) if s == 0 else (xb1, ob1)
        xo, obo = (xb1, ob1) if s == 0 else (xb0, ob0)
        if s == 0:
            @pl.when(i == 0)
            def _():
                idx_copy(0).start()
                idx_copy(0).wait()
                start_gather(0, xb0, 0)
                if n_tiles > 1:
                    idx_copy(1).start()

        @pl.when(i + 2 < n_tiles)
        def _():
            idx_copy(i + 2).start()

        @pl.when(i + 1 < n_tiles)
        def _():
            idx_copy(i + 1).wait()
            start_gather(i + 1, xo, 1 - s)

        wait_gather(xb, s)

        @pl.when(i >= 2)
        def _():
            wait_scatter(valid_ref[jnp.maximum(i - 2, 0)], ob, s)

        n_valid = valid_ref[i]

        @pl.when(n_valid > 0)
        def _():
            x = xb[...].reshape(tms, d)
            t = _rms(x, g_ref[...]).astype(BF16)
            gates = gate_ref[...]
            y = x
            for wgu_ref, wd_ref, col in ((wgu_a_ref, wd_a_ref, 0), (wgu_b_ref, wd_b_ref, 1)):
                hgu = jnp.dot(t, wgu_ref[0], preferred_element_type=F32)
                hg, hu = hgu[:, 0:D_EXPERT], hgu[:, D_EXPERT:2 * D_EXPERT]
                act = (hg * jax.nn.sigmoid(hg) * hu * gates[:, col:col + 1]).astype(BF16)
                y = y + jnp.dot(act, wd_ref[0], preferred_element_type=F32)
            if final_norm:
                y = _rms(y, gf_ref[...])
            ob[...] = y.reshape(n_grp, sub, d)

        start_scatter(i, n_valid, ob, s)

        @pl.when(i == n_tiles - 1)
        def _():
            if n_tiles > 1:
                wait_scatter(valid_ref[jnp.maximum(i - 1, 0)], obo, 1 - s)
            wait_scatter(n_valid, ob, s)

    parity = lax.rem(i, 2)
    pl.when(parity == 0)(lambda: tile(0))
    pl.when(parity == 1)(lambda: tile(1))


def _moe_plan(route, counts, n_tok):
    tms = MOE_TILE
    n_b = N_GROUPS * PAIRS_PER_GROUP
    n_tiles = n_tok // tms + n_b
    bucket = route[:, 0].astype(jnp.int32)
    rank = route[:, 1].astype(jnp.int32)
    cnt = counts[0, :n_b].astype(jnp.int32)
    padded = (cnt + tms - 1) // tms * tms
    pad_end = jnp.cumsum(padded)
    pad_off = pad_end - padded
    dest = pad_off[bucket] + rank
    payload = jnp.concatenate([jnp.arange(n_tok, dtype=F32)[:, None], route[:, 2:4]], axis=1)
    plan = jnp.zeros((n_tiles * tms, 3), F32).at[dest].set(payload, unique_indices=True)
    src = plan[:, 0].astype(jnp.int32).reshape(n_tiles, tms)
    gates = plan[:, 1:3]
    start = jnp.arange(n_tiles, dtype=jnp.int32) * tms
    tb = jnp.minimum(jnp.sum(pad_end[None, :] <= start[:, None], axis=1), n_b - 1).astype(jnp.int32)
    valid = jnp.clip(cnt[tb] - (start - pad_off[tb]), 0, tms).astype(jnp.int32)
    lo_hi = [(a, b) for a in range(EXPERTS_PER_GROUP) for b in range(a + 1, EXPERTS_PER_GROUP)]
    ea_tab = np.array([EXPERTS_PER_GROUP * g + a for g in range(N_GROUPS) for a, _ in lo_hi], np.int32)
    eb_tab = np.array([EXPERTS_PER_GROUP * g + b for g in range(N_GROUPS) for _, b in lo_hi], np.int32)
    return src, gates, jnp.asarray(ea_tab)[tb], jnp.asarray(eb_tab)[tb], valid


def _moe(x2d, route, counts, g, wgu, wd, gf, final_norm):
    n_tok, d = x2d.shape
    tms = MOE_TILE
    src, gates, ea, eb, valid = _moe_plan(route, counts, n_tok)
    n_tiles = src.shape[0]
    any_spec = pl.BlockSpec(memory_space=pl.ANY)
    grid_spec = pltpu.PrefetchScalarGridSpec(
        num_scalar_prefetch=3,
        grid=(n_tiles,),
        in_specs=[any_spec, any_spec,
                  pl.BlockSpec((tms, 2), lambda i, ea_r, eb_r, v_r: (i, 0)),
                  pl.BlockSpec(g.shape, lambda i, ea_r, eb_r, v_r: (0, 0)),
                  pl.BlockSpec((1, d, 2 * D_EXPERT), lambda i, ea_r, eb_r, v_r: (ea_r[i], 0, 0)),
                  pl.BlockSpec((1, d, 2 * D_EXPERT), lambda i, ea_r, eb_r, v_r: (eb_r[i], 0, 0)),
                  pl.BlockSpec((1, D_EXPERT, d), lambda i, ea_r, eb_r, v_r: (ea_r[i], 0, 0)),
                  pl.BlockSpec((1, D_EXPERT, d), lambda i, ea_r, eb_r, v_r: (eb_r[i], 0, 0)),
                  pl.BlockSpec(gf.shape, lambda i, ea_r, eb_r, v_r: (0, 0))],
        out_specs=any_spec,
        scratch_shapes=[pltpu.SMEM((MOE_IDX_SLOTS * tms,), jnp.int32)]
        + [pltpu.VMEM((tms // SUBLANES, SUBLANES, d), F32)] * 4
        + [pltpu.SemaphoreType.DMA((MOE_IDX_SLOTS,)), pltpu.SemaphoreType.DMA((2,)),
           pltpu.SemaphoreType.DMA((2,))],
    )
    return pl.pallas_call(
        functools.partial(_moe_kernel, n_tiles=n_tiles, final_norm=final_norm),
        out_shape=jax.ShapeDtypeStruct((n_tok, d), F32),
        grid_spec=grid_spec,
        compiler_params=_cparams(("arbitrary",)),
        name="moe",
    )(ea, eb, valid, src, x2d, gates, g, wgu, wgu, wd, wd, gf)


def _row(v):
    return v.reshape(1, -1).astype(F32)


def _rope_tables(seq_len, rot_dim, lane_starts):
    inv = ROPE_THETA ** (-jnp.arange(0, rot_dim, 2, dtype=F32) / rot_dim)
    ang = jnp.arange(seq_len, dtype=F32)[:, None] * inv[None, :]
    cos, sin = jnp.cos(ang), jnp.sin(ang)
    ct = jnp.ones((seq_len, LANES), F32)
    st = jnp.zeros((seq_len, LANES), F32)
    for l0 in lane_starts:
        ct = ct.at[:, l0:l0 + rot_dim].set(jnp.concatenate([cos, cos], axis=1))
        st = st.at[:, l0:l0 + rot_dim].set(jnp.concatenate([-sin, sin], axis=1))
    return ct, st


def _prep_even(w_in, w_uq, w_ukv, w_o):
    c = np.cumsum([0, MLA_Q_RANK, MLA_KV_RANK, MLA_ROPE, DIFF_QK, DIFF_QK, DIFF_VW])
    kpe = jnp.zeros((D_MODEL, LANES), F32).at[:, MLA_NOPE:MLA_NOPE + MLA_ROPE].set(w_in[:, c[2]:c[3]])
    win = jnp.concatenate([w_in[:, c[0]:c[2]], w_in[:, c[3]:c[6]], kpe], axis=1).astype(BF16)
    uq = w_uq.reshape(MLA_Q_RANK, MLA_HEADS, MLA_NOPE + MLA_ROPE)
    uq = jnp.pad(uq, ((0, 0), (0, 0), (0, LANES - MLA_NOPE - MLA_ROPE))).reshape(MLA_Q_RANK, MLA_HEADS * LANES)
    ukv = w_ukv.reshape(MLA_KV_RANK, MLA_HEADS, MLA_NOPE + MLA_V)
    uk = jnp.pad(ukv[:, :, :MLA_NOPE], ((0, 0), (0, 0), (0, LANES - MLA_NOPE))).reshape(MLA_KV_RANK,
                                                                                       MLA_HEADS * LANES)
    uv = ukv[:, :, MLA_NOPE:].reshape(MLA_KV_RANK, MLA_HEADS * MLA_V)
    n_mla = MLA_HEADS * MLA_V
    return (win, uq.astype(BF16), uk.astype(BF16), uv.astype(BF16),
            w_o[:n_mla].astype(BF16), w_o[n_mla:].astype(BF16))


def _prep_moe(w_rg, b_rg, w_re, b_re, w_gate, w_up, w_down):
    wr = jnp.zeros((D_MODEL, LANES), F32).at[:, :N_GROUPS].set(w_rg).at[:, N_GROUPS:N_GROUPS + N_EXPERTS].set(w_re)
    br = jnp.zeros((1, LANES), F32).at[0, :N_GROUPS].set(b_rg).at[0, N_GROUPS:N_GROUPS + N_EXPERTS].set(b_re)
    wr_hi = wr.astype(BF16)
    wr_lo = (wr - wr_hi.astype(F32)).astype(BF16)
    wgu = jnp.concatenate([w_gate, w_up], axis=-1).reshape(N_EXPERTS, D_MODEL, 2 * D_EXPERT).astype(BF16)
    wd = w_down.reshape(N_EXPERTS, D_EXPERT, D_MODEL).astype(BF16)
    return wr_hi, wr_lo, br, wgu, wd


def _trunk(x, mem, p, tables):
    b, s, d = x.shape
    depth = p['ln_mix'].shape[0]
    for layer in range(depth):
        i = layer // 2
        if layer % 2 == 0:
            win, uq, uk, uv, wo_a, wo_b = _prep_even(p['w_in_even'][i], p['mla_w_uq'][i], p['mla_w_ukv'][i],
                                                     p['w_o_even'][i])
            cm, sm, cd, sd = tables[s]
            q, k, v, dq, dk, dv = _even_in(x, _row(p['ln_mix'][layer]), win, _row(p['mla_q_norm'][i]),
                                           _row(p['mla_kv_norm'][i]), uq, uk, uv, cm, sm, cd, sd)
            o_mla = _flash(q, k, v, 2 * LANES)
            lamp = jnp.zeros((8, LANES), F32)
            for r, name in enumerate(('diff_lambda_q1', 'diff_lambda_k1', 'diff_lambda_q2', 'diff_lambda_k2')):
                lamp = lamp.at[r, :DIFF_HEAD_DIM].set(p[name][i])
            lam_init = 0.8 - 0.6 * math.exp(-0.3 * layer)
            o_diff = _flash(dq, dk, dv, LANES, lamp=lamp, subln=_row(p['diff_subln'][i]), lam_init=lam_init)
            a_list, w_list = [o_mla, o_diff], [wo_a, wo_b]
        else:
            q, k, v = _odd_in(x, _row(p['ln_mix'][layer]), p['w_in_odd'][i].astype(BF16))
            tbl, idx = _na_tables(p['na_rpb'][i])
            a_list, w_list = [_na_attention(q, k, v, tbl, idx)], [p['w_o_odd'][i].astype(BF16)]
        mk, mv = _mem_kv(mem, _row(p['ln_mem'][layer]), p['xa_w_kv'][layer].astype(BF16))
        wr_hi, wr_lo, br, wgu, wd = _prep_moe(p['moe_w_group'][layer], p['moe_b_group'][layer],
                                              p['moe_w_expert'][layer], p['moe_b_expert'][layer],
                                              p['moe_w_gate'][layer], p['moe_w_up'][layer], p['moe_w_down'][layer])
        x, route, counts = _post_mix(x, a_list, w_list, _row(p['ln_xattn'][layer]),
                                     p['xa_w_q'][layer].astype(BF16), mk, mv, p['xa_w_o'][layer].astype(BF16),
                                     _row(p['ln_ffn'][layer]), wr_hi, wr_lo, br)
        x = _moe(x.reshape(b * s, d), route.reshape(b * s, LANES), counts, _row(p['ln_ffn'][layer]), wgu, wd,
                 _row(p['ln_final']), final_norm=(layer == depth - 1)).reshape(b, s, d)
    return x


def kernel(x_prompt, x_sample, mem_prompt, mem_sample, ln_mix, w_in_even, mla_q_norm, mla_kv_norm, mla_w_uq, mla_w_ukv, diff_lambda_q1, diff_lambda_k1, diff_lambda_q2, diff_lambda_k2, diff_subln, w_o_even, w_in_odd, na_rpb, w_o_odd, ln_xattn, ln_mem, xa_w_q, xa_w_kv, xa_w_o, ln_ffn, moe_w_group, moe_b_group, moe_w_expert, moe_b_expert, moe_w_gate, moe_w_up, moe_w_down, ln_final):
    p = dict(ln_mix=ln_mix, w_in_even=w_in_even, mla_q_norm=mla_q_norm, mla_kv_norm=mla_kv_norm,
             mla_w_uq=mla_w_uq, mla_w_ukv=mla_w_ukv, diff_lambda_q1=diff_lambda_q1,
             diff_lambda_k1=diff_lambda_k1, diff_lambda_q2=diff_lambda_q2, diff_lambda_k2=diff_lambda_k2,
             diff_subln=diff_subln, w_o_even=w_o_even, w_in_odd=w_in_odd, na_rpb=na_rpb, w_o_odd=w_o_odd,
             ln_xattn=ln_xattn, ln_mem=ln_mem, xa_w_q=xa_w_q, xa_w_kv=xa_w_kv, xa_w_o=xa_w_o,
             ln_ffn=ln_ffn, moe_w_group=moe_w_group, moe_b_group=moe_b_group, moe_w_expert=moe_w_expert,
             moe_b_expert=moe_b_expert, moe_w_gate=moe_w_gate, moe_w_up=moe_w_up, moe_w_down=moe_w_down,
             ln_final=ln_final)
    tables = {}
    for s in {x_prompt.shape[1], x_sample.shape[1]}:
        cm, sm = _rope_tables(s, MLA_ROPE, (MLA_NOPE,))
        cd, sd = _rope_tables(s, DIFF_ROT, (0, DIFF_HEAD_DIM))
        tables[s] = (cm, sm, cd, sd)
    return (_trunk(x_prompt, mem_prompt, p, tables), _trunk(x_sample, mem_sample, p, tables))
```

```python
import functools
import math

import numpy as np
import jax
import jax.numpy as jnp
from jax import lax
from jax.experimental import pallas as pl
from jax.experimental.pallas import tpu as pltpu

F32 = jnp.float32
BF16 = jnp.bfloat16

D_MODEL = 1024
ROPE_THETA = 500000.0
NORM_EPS = 1e-6
NEG_INF = -1e30
LOG2E = math.log2(math.e)

MLA_HEADS = 8
MLA_Q_RANK = 384
MLA_KV_RANK = 256
MLA_NOPE = 64
MLA_ROPE = 32
MLA_V = 64
DIFF_HEADS = 4
DIFF_HEAD_DIM = 64
DIFF_ROT = DIFF_HEAD_DIM // 4
DIFF_QK = DIFF_HEADS * 2 * DIFF_HEAD_DIM
DIFF_VW = DIFF_HEADS * 2 * DIFF_HEAD_DIM
GRID_W = 64
NA_HEADS = 16
NA_HEAD_DIM = 64
WIN_ROWS = 8
WIN_COLS = 16
MEM_TOKENS = 256
XA_HEADS = 4
XA_HEAD_DIM = 128
XA_INNER = XA_HEADS * XA_HEAD_DIM
N_GROUPS = 4
EXPERTS_PER_GROUP = 4
N_EXPERTS = N_GROUPS * EXPERTS_PER_GROUP
PAIRS_PER_GROUP = EXPERTS_PER_GROUP * (EXPERTS_PER_GROUP - 1) // 2
D_EXPERT = 256

LANES = 128
SUBLANES = 8
VMEM_LIMIT = 56 * 1024 * 1024

TOKEN_TILE = 512
MOE_TILE = 256
MOE_GATHER_CHUNKS = 8
MOE_IDX_SLOTS = 4
FLASH_TQ = 512
FLASH_TK = 512
FLASH_BOUND_SLACK = 1.01
FLASH_MIN_ROW_SUM = 2.0 ** -60
FLASH_UNROLL = 8
NA_ROW_BLOCK = 8
NA_KEY_ROWS = 2 * WIN_ROWS
NA_GROUP_ROWS = 4
NA_GROUP_KEY_ROWS = NA_GROUP_ROWS + WIN_ROWS
NA_TILE_CODES = 3
NA_TILES_PER_HEAD = NA_TILE_CODES * 16 + 1


def _cparams(sem):
    return pltpu.CompilerParams(dimension_semantics=sem, vmem_limit_bytes=VMEM_LIMIT)


def _rms(xf, g):
    return xf * lax.rsqrt(jnp.mean(xf * xf, axis=-1, keepdims=True) + NORM_EPS) * g


def _full(shape):
    nd = len(shape)
    return pl.BlockSpec(shape, lambda *_: (0,) * nd)


def _rope_block(x, cos, sin, half):
    lane = lax.broadcasted_iota(jnp.int32, x.shape, 1)
    up = pltpu.roll(x, LANES - half, 1)
    down = pltpu.roll(x, half, 1)
    first = lax.bitwise_and(lane, 2 * half - 1) < half
    return x * cos + jnp.where(first, up, down) * sin


def _even_in_kernel(x_ref, g_ref, win_ref, qn_ref, kvn_ref, wuq_ref, wuk_ref, wuv_ref,
                    cm_ref, sm_ref, cd_ref, sd_ref,
                    q_ref, k_ref, v_ref, dq_ref, dk_ref, dv_ref):
    h = _rms(x_ref[0], g_ref[...]).astype(BF16)
    proj = jnp.dot(h, win_ref[...], preferred_element_type=F32)
    o_cq, o_ckv, o_dq, o_dk, o_dv, o_kpe = 0, 384, 640, 1152, 1664, 2176
    cm, sm, cd, sd = cm_ref[...], sm_ref[...], cd_ref[...], sd_ref[...]

    cq = _rms(proj[:, o_cq:o_cq + MLA_Q_RANK], qn_ref[...]).astype(BF16)
    q = jnp.dot(cq, wuq_ref[...], preferred_element_type=F32)
    mla_c = (MLA_NOPE + MLA_ROPE) ** -0.5 * LOG2E
    for hd in range(MLA_HEADS):
        sl = slice(hd * LANES, (hd + 1) * LANES)
        q_ref[0, :, sl] = (_rope_block(q[:, sl], cm, sm, MLA_ROPE // 2) * mla_c).astype(BF16)

    ckv = _rms(proj[:, o_ckv:o_ckv + MLA_KV_RANK], kvn_ref[...]).astype(BF16)
    kn = jnp.dot(ckv, wuk_ref[...], preferred_element_type=F32)
    kpe = _rope_block(proj[:, o_kpe:o_kpe + LANES], cm, sm, MLA_ROPE // 2)
    for hd in range(MLA_HEADS):
        sl = slice(hd * LANES, (hd + 1) * LANES)
        k_ref[0, :, sl] = (kn[:, sl] + kpe).astype(BF16)
    v_ref[0] = jnp.dot(ckv, wuv_ref[...], preferred_element_type=F32).astype(BF16)

    diff_c = DIFF_HEAD_DIM ** -0.5 * LOG2E
    for hd in range(DIFF_HEADS):
        sl = slice(hd * LANES, (hd + 1) * LANES)
        dq_ref[0, :, sl] = (_rope_block(proj[:, o_dq + hd * LANES:o_dq + (hd + 1) * LANES], cd, sd,
                                        DIFF_ROT // 2) * diff_c).astype(BF16)
        dk_ref[0, :, sl] = _rope_block(proj[:, o_dk + hd * LANES:o_dk + (hd + 1) * LANES], cd, sd,
                                       DIFF_ROT // 2).astype(BF16)
    dv_ref[0] = proj[:, o_dv:o_dv + DIFF_VW].astype(BF16)


def _even_in(x, g, win, qn, kvn, wuq, wuk, wuv, cm, sm, cd, sd):
    b, s, d = x.shape
    tm = TOKEN_TILE
    tok = lambda w: pl.BlockSpec((1, tm, w), lambda i, j: (i, j, 0))
    tab = pl.BlockSpec((tm, LANES), lambda i, j: (j, 0))
    outs = [jax.ShapeDtypeStruct((b, s, w), BF16) for w in (1024, 1024, 512, 512, 512, 512)]
    return pl.pallas_call(
        _even_in_kernel,
        out_shape=outs,
        grid=(b, s // tm),
        in_specs=[tok(d), _full(g.shape), _full(win.shape), _full(qn.shape), _full(kvn.shape),
                  _full(wuq.shape), _full(wuk.shape), _full(wuv.shape), tab, tab, tab, tab],
        out_specs=[tok(1024), tok(1024), tok(512), tok(512), tok(512), tok(512)],
        compiler_params=_cparams(("parallel", "parallel")),
        name="even_in",
    )(x, g, win, qn, kvn, wuq, wuk, wuv, cm, sm, cd, sd)


def _flash_kernel(*refs, tq, tk, n_k, width, diff, lam_init):
    if diff:
        q_ref, k_ref, v_ref, lamp_ref, g_ref, o_ref, qs_ref, m_ref, l_ref, acc_ref, kn_ref, kmax_ref = refs
    else:
        q_ref, k_ref, v_ref, o_ref, qs_ref, m_ref, l_ref, acc_ref, kn_ref, kmax_ref = refs
    half = width // 2
    n_j = tk // LANES
    q = q_ref[0]
    lane = lax.broadcasted_iota(jnp.int32, (tq, width), 1)
    zero = jnp.zeros_like(q)
    qs_ref[0:tq, :] = jnp.where(lane < half, q, zero)
    qs_ref[tq:2 * tq, :] = jnp.where(lane >= half, q, zero)

    @pl.when(pl.program_id(2) == 0)
    def _():
        r_i = lax.broadcasted_iota(jnp.int32, (width, LANES), 0)
        c_i = lax.broadcasted_iota(jnp.int32, (width, LANES), 1)
        sel = jnp.where(((c_i == 0) & (r_i < half)) | ((c_i == 1) & (r_i >= half)), 1.0, 0.0).astype(BF16)
        kn_ref[...] = jnp.zeros(kn_ref.shape, F32)

        def knorm(i, carry):
            kc = k_ref[0, pl.ds(pl.multiple_of(i * tk, tk), tk), :].astype(F32)
            kn_ref[...] = jnp.maximum(kn_ref[...], jnp.dot((kc * kc).astype(BF16), sel, preferred_element_type=F32))
            return carry

        lax.fori_loop(0, n_k, knorm, 0)
        kmax = jnp.sqrt(jnp.max(kn_ref[...], axis=0, keepdims=True))
        kmax_ref[0:1, :] = jnp.broadcast_to(kmax[:, 0:1], (1, LANES))
        kmax_ref[1:2, :] = jnp.broadcast_to(kmax[:, 1:2], (1, LANES))

    def reset(m):
        m_ref[...] = m
        l_ref[...] = jnp.zeros(l_ref.shape, F32)
        acc_ref[...] = jnp.zeros(acc_ref.shape, F32)

    def scores(i):
        off = pl.multiple_of(i * tk, tk)
        kc = k_ref[0, pl.ds(off, tk), :]
        return off, lax.dot_general(qs_ref[...], kc, (((1,), (1,)), ((), ())), preferred_element_type=F32)

    def max_pass(i, carry):
        _, s = scores(i)
        m = m_ref[...]
        for j in range(n_j):
            m = jnp.maximum(m, s[:, j * LANES:(j + 1) * LANES])
        m_ref[...] = m
        return carry

    def pv_pass(i, carry):
        off, s = scores(i)
        vc = v_ref[0, pl.ds(off, tk), :]
        m = m_ref[...]
        ps = [jnp.exp2(s[:, j * LANES:(j + 1) * LANES] - m) for j in range(n_j)]
        lsum = ps[0]
        for pj in ps[1:]:
            lsum = lsum + pj
        l_ref[...] += lsum
        p = jnp.concatenate([pj.astype(BF16) for pj in ps], axis=1)
        acc_ref[...] += jnp.dot(p, vc, preferred_element_type=F32)
        return carry

    qf = qs_ref[...].astype(F32)
    qn = jnp.sqrt(jnp.sum(qf * qf, axis=1, keepdims=True))
    bound_top = qn[0:tq] * kmax_ref[0:1, :] * FLASH_BOUND_SLACK
    bound_bot = qn[tq:2 * tq] * kmax_ref[1:2, :] * FLASH_BOUND_SLACK
    reset(jnp.concatenate([bound_top, bound_bot], axis=0))
    lax.fori_loop(0, n_k, pv_pass, 0, unroll=FLASH_UNROLL)
    l_min = jnp.min(jnp.sum(l_ref[...], axis=1, keepdims=True))

    @pl.when(jnp.logical_not(l_min >= FLASH_MIN_ROW_SUM))
    def _():
        reset(jnp.full(m_ref.shape, NEG_INF, F32))
        lax.fori_loop(0, n_k, max_pass, 0, unroll=FLASH_UNROLL)
        m_ref[...] = jnp.broadcast_to(jnp.max(m_ref[...], axis=1, keepdims=True), m_ref.shape)
        lax.fori_loop(0, n_k, pv_pass, 0, unroll=FLASH_UNROLL)

    o = acc_ref[...] / jnp.sum(l_ref[...], axis=1, keepdims=True)
    top, bot = o[0:tq], o[tq:2 * tq]
    if diff:
        lp = lamp_ref[...]
        lam = (jnp.exp(jnp.sum(lp[0:1] * lp[1:2], axis=1, keepdims=True))
               - jnp.exp(jnp.sum(lp[2:3] * lp[3:4], axis=1, keepdims=True)) + lam_init)
        od = top - lam * bot
        o_ref[0] = (_rms(od, g_ref[...]) * (1.0 - lam_init)).astype(BF16)
    else:
        lane_o = lax.broadcasted_iota(jnp.int32, (tq, LANES), 1)
        o_ref[0] = jnp.where(lane_o < LANES // 2, top, bot).astype(BF16)


def _flash(q, k, v, width, lamp=None, subln=None, lam_init=0.0):
    b, s, _ = q.shape
    n = q.shape[2] // width
    tq, tk = FLASH_TQ, FLASH_TK
    diff = lamp is not None
    kern = functools.partial(_flash_kernel, tq=tq, tk=tk, n_k=s // tk, width=width, diff=diff,
                             lam_init=lam_init)
    in_specs = [pl.BlockSpec((1, tq, width), lambda bi, p, qi: (bi, qi, p)),
                pl.BlockSpec((1, s, width), lambda bi, p, qi: (bi, 0, p)),
                pl.BlockSpec((1, s, LANES), lambda bi, p, qi: (bi, 0, p))]
    args = [q, k, v]
    if diff:
        in_specs += [_full(lamp.shape), _full(subln.shape)]
        args += [lamp, subln]
    return pl.pallas_call(
        kern,
        out_shape=jax.ShapeDtypeStruct((b, s, n * LANES), BF16),
        grid=(b, n, s // tq),
        in_specs=in_specs,
        out_specs=pl.BlockSpec((1, tq, LANES), lambda bi, p, qi: (bi, qi, p)),
        scratch_shapes=[pltpu.VMEM((2 * tq, width), BF16), pltpu.VMEM((2 * tq, LANES), F32),
                        pltpu.VMEM((2 * tq, LANES), F32), pltpu.VMEM((2 * tq, LANES), F32),
                        pltpu.VMEM((tk, LANES), F32), pltpu.VMEM((SUBLANES, LANES), F32)],
        compiler_params=_cparams(("parallel", "parallel", "arbitrary")),
        name="flash_diff" if diff else "flash_mla",
    )(*args)


def _odd_in_kernel(x_ref, g_ref, w_ref, q_ref, k_ref, v_ref):
    h = _rms(x_ref[0], g_ref[...]).astype(BF16)
    qkv = jnp.dot(h, w_ref[...], preferred_element_type=F32)
    c = NA_HEAD_DIM ** -0.5 * LOG2E
    q_ref[0] = (qkv[:, 0:1024] * c).astype(BF16)
    k_ref[0] = qkv[:, 1024:2048].astype(BF16)
    v_ref[0] = qkv[:, 2048:3072].astype(BF16)


def _odd_in(x, g, w):
    b, s, d = x.shape
    tm = TOKEN_TILE
    tok = pl.BlockSpec((1, tm, d), lambda i, j: (i, j, 0))
    return pl.pallas_call(
        _odd_in_kernel,
        out_shape=[jax.ShapeDtypeStruct((b, s, d), BF16)] * 3,
        grid=(b, s // tm),
        in_specs=[tok, _full(g.shape), _full(w.shape)],
        out_specs=[tok, tok, tok],
        compiler_params=_cparams(("parallel", "parallel")),
        name="odd_in",
    )(x, g, w)


def _na_kernel(idx_ref, q_ref, k_ref, v_ref, tbl_ref, o_ref, qs_ref, os_ref, *, rows, n_rb):
    nq = NA_ROW_BLOCK * GRID_W
    nk = NA_KEY_ROWS * GRID_W
    gq = NA_GROUP_ROWS * GRID_W
    n_kp = NA_KEY_ROWS // 2
    rb = pl.program_id(2)
    ks = jnp.clip(NA_ROW_BLOCK * rb - WIN_ROWS // 2, 0, rows - NA_KEY_ROWS)
    pat = jnp.where(rb == 0, 0, jnp.where(rb == n_rb - 1, 2, 1))
    off = pl.multiple_of(ks * GRID_W, (WIN_ROWS // 2) * GRID_W)
    q = q_ref[0]
    lane = lax.broadcasted_iota(jnp.int32, (nq, LANES), 1)
    zero = jnp.zeros_like(q)
    qs_ref[0:nq, :] = jnp.where(lane < LANES // 2, q, zero)
    qs_ref[nq:2 * nq, :] = jnp.where(lane >= LANES // 2, q, zero)

    def group(g, carry):
        r0 = pl.multiple_of(g * gq, gq)
        per_head = NA_ROW_BLOCK // NA_GROUP_ROWS
        hh = g // per_head
        i0 = (g % per_head) * NA_GROUP_ROWS
        w0 = jnp.clip(i0 + (pat - 1) * (WIN_ROWS // 2), 0, NA_KEY_ROWS - NA_GROUP_KEY_ROWS)
        goff = pl.multiple_of(off + w0 * GRID_W, (WIN_ROWS // 2) * GRID_W)
        kw = k_ref[0, pl.ds(goff, NA_GROUP_KEY_ROWS * GRID_W), :]
        vw = v_ref[0, pl.ds(goff, NA_GROUP_KEY_ROWS * GRID_W), :]
        s = lax.dot_general(qs_ref[pl.ds(r0, gq), :], kw, (((1,), (1,)), ((), ())), preferred_element_type=F32)
        bias_rows = []
        for ii in range(NA_GROUP_ROWS):
            base = pat * (NA_ROW_BLOCK * n_kp) + (i0 + ii) * n_kp + lax.shift_right_logical(w0, 1)
            bias_rows.append(jnp.concatenate(
                [tbl_ref[0, hh * NA_TILES_PER_HEAD + idx_ref[base + kp]]
                 for kp in range(NA_GROUP_KEY_ROWS // 2)], axis=1))
        s = s + jnp.concatenate(bias_rows, axis=0)
        m = jnp.max(s, axis=1, keepdims=True)
        p = jnp.exp2(s - m)
        linv = 1.0 / jnp.sum(p, axis=1, keepdims=True)
        os_ref[pl.ds(r0, gq), :] = jnp.dot(p.astype(BF16), vw, preferred_element_type=F32) * linv
        return carry

    lax.fori_loop(0, 2 * nq // gq, group, 0, unroll=True)
    o_ref[0] = jnp.where(lane < LANES // 2, os_ref[0:nq, :], os_ref[nq:2 * nq, :]).astype(BF16)


def _na_tables(rpb):
    col = np.arange(GRID_W)
    col_start = np.clip(col - WIN_COLS // 2, 0, GRID_W - WIN_COLS)
    col_valid = (col[None, :] >= col_start[:, None]) & (col[None, :] < col_start[:, None] + WIN_COLS)
    col_idx = np.clip(col[None, :] - col[:, None], -(WIN_COLS - 1), WIN_COLS - 1) + WIN_COLS - 1
    n_ri = 2 * WIN_ROWS - 1
    ri = np.zeros((NA_TILES_PER_HEAD, LANES), np.int32)
    ok = np.zeros((NA_TILES_PER_HEAD, LANES), bool)
    for code in range(NA_TILE_CODES):
        for a in range(-1, n_ri):
            t = code * 16 + a + 1
            ri[t, :GRID_W] = np.clip(a, 0, n_ri - 1)
            ri[t, GRID_W:] = np.clip(a + 1, 0, n_ri - 1)
            ok[t, :GRID_W] = code in (0, 1) and 0 <= a < n_ri
            ok[t, GRID_W:] = code in (0, 2) and 0 <= a + 1 < n_ri
    onehot = (col_idx.reshape(-1)[None, :] == np.arange(2 * WIN_COLS - 1)[:, None]).astype(np.float32)
    band = jnp.dot(rpb.reshape(NA_HEADS * n_ri, 2 * WIN_COLS - 1).astype(F32), jnp.asarray(onehot),
                   precision=lax.Precision.HIGHEST).reshape(NA_HEADS, n_ri, GRID_W, GRID_W)
    band = jnp.where(col_valid[None, None], band * LOG2E, NEG_INF)
    masked = jnp.full((NA_HEADS, GRID_W, GRID_W), NEG_INF, F32)
    tiles = []
    for t in range(NA_TILES_PER_HEAD):
        left = band[:, ri[t, 0]] if ok[t, 0] else masked
        right = band[:, ri[t, GRID_W]] if ok[t, GRID_W] else masked
        tiles.append(jnp.concatenate([left, right], axis=-1))
    tbl = jnp.stack(tiles, axis=1).reshape(NA_HEADS // 2, 2 * NA_TILES_PER_HEAD, GRID_W, LANES)

    idx = np.zeros((3, NA_ROW_BLOCK, NA_KEY_ROWS // 2), np.int32)
    for pat in range(3):
        for i in range(NA_ROW_BLOCK):
            rel = (i, i + WIN_ROWS // 2, i + WIN_ROWS)[pat]
            start = (max(i - WIN_ROWS // 2, 0), i, min(i + WIN_ROWS // 2, WIN_ROWS))[pat]
            for kp in range(NA_KEY_ROWS // 2):
                kk = 2 * kp
                a = kk - rel + WIN_ROWS - 1
                v0 = start <= kk < start + WIN_ROWS
                v1 = start <= kk + 1 < start + WIN_ROWS
                if v0:
                    assert 0 <= a < n_ri
                if v1:
                    assert 0 <= a + 1 < n_ri
                if v0 and v1:
                    idx[pat, i, kp] = a + 1
                elif v0:
                    idx[pat, i, kp] = 16 + a + 1
                elif v1:
                    idx[pat, i, kp] = 32 + a + 1
                else:
                    idx[pat, i, kp] = NA_TILES_PER_HEAD - 1
    return tbl, jnp.asarray(idx.reshape(-1))


def _na_attention(q, k, v, tbl, idx):
    b, s, d = q.shape
    rows = s // GRID_W
    n_rb = rows // NA_ROW_BLOCK
    nq = NA_ROW_BLOCK * GRID_W
    nk = NA_KEY_ROWS * GRID_W
    kern = functools.partial(_na_kernel, rows=rows, n_rb=n_rb)
    grid_spec = pltpu.PrefetchScalarGridSpec(
        num_scalar_prefetch=1,
        grid=(NA_HEADS // 2, b, n_rb),
        in_specs=[pl.BlockSpec((1, nq, LANES), lambda p, bi, r, idx_r: (bi, r, p)),
                  pl.BlockSpec((1, s, LANES), lambda p, bi, r, idx_r: (bi, 0, p)),
                  pl.BlockSpec((1, s, LANES), lambda p, bi, r, idx_r: (bi, 0, p)),
                  pl.BlockSpec((1, 2 * NA_TILES_PER_HEAD, GRID_W, LANES), lambda p, bi, r, idx_r: (p, 0, 0, 0))],
        out_specs=pl.BlockSpec((1, nq, LANES), lambda p, bi, r, idx_r: (bi, r, p)),
        scratch_shapes=[pltpu.VMEM((2 * nq, LANES), BF16), pltpu.VMEM((2 * nq, LANES), F32)],
    )
    return pl.pallas_call(
        kern,
        out_shape=jax.ShapeDtypeStruct((b, s, d), BF16),
        grid_spec=grid_spec,
        compiler_params=_cparams(("parallel", "parallel", "arbitrary")),
        name="na_attention",
    )(idx, q, k, v, tbl)


def _mem_kv_kernel(m_ref, g_ref, w_ref, k_ref, v_ref):
    h = _rms(m_ref[0], g_ref[...]).astype(BF16)
    kv = jnp.dot(h, w_ref[...], preferred_element_type=F32)
    k_ref[0] = kv[:, 0:XA_INNER].astype(BF16)
    v_ref[0] = kv[:, XA_INNER:2 * XA_INNER].astype(BF16)


def _mem_kv(mem, g, w):
    b, m, d = mem.shape
    return pl.pallas_call(
        _mem_kv_kernel,
        out_shape=[jax.ShapeDtypeStruct((b, m, XA_INNER), BF16)] * 2,
        grid=(b,),
        in_specs=[pl.BlockSpec((1, m, d), lambda i: (i, 0, 0)), _full(g.shape), _full(w.shape)],
        out_specs=[pl.BlockSpec((1, m, XA_INNER), lambda i: (i, 0, 0))] * 2,
        compiler_params=_cparams(("parallel",)),
        name="mem_kv",
    )(mem, g, w)


def _post_mix_kernel(*refs, n_a):
    x_ref = refs[0]
    a_refs = refs[1:1 + n_a]
    w_refs = refs[1 + n_a:1 + 2 * n_a]
    (g_ref, wq_ref, mk_ref, mv_ref, wo_ref, gffn_ref, wr_hi_ref, wr_lo_ref, br_ref, tri_ref,
     o_ref, route_ref, cnt_ref) = refs[1 + 2 * n_a:]

    @pl.when((pl.program_id(0) == 0) & (pl.program_id(1) == 0))
    def _():
        cnt_ref[...] = jnp.zeros(cnt_ref.shape, F32)

    x1 = x_ref[0]
    for a_ref, w_ref in zip(a_refs, w_refs):
        x1 = x1 + jnp.dot(a_ref[0], w_ref[...], preferred_element_type=F32)
    h = _rms(x1, g_ref[...]).astype(BF16)
    c = XA_HEAD_DIM ** -0.5 * LOG2E
    q = (jnp.dot(h, wq_ref[...], preferred_element_type=F32) * c).astype(BF16)
    outs = []
    for hd in range(XA_HEADS):
        sl = slice(hd * XA_HEAD_DIM, (hd + 1) * XA_HEAD_DIM)
        s = lax.dot_general(q[:, sl], mk_ref[0, :, sl], (((1,), (1,)), ((), ())), preferred_element_type=F32)
        p = jnp.exp2(s - jnp.max(s, axis=1, keepdims=True))
        linv = 1.0 / jnp.sum(p, axis=1, keepdims=True)
        o = jnp.dot(p.astype(BF16), mv_ref[0, :, sl], preferred_element_type=F32) * linv
        outs.append(o.astype(BF16))
    o_all = jnp.concatenate(outs, axis=1)
    x2 = x1 + jnp.dot(o_all, wo_ref[...], preferred_element_type=F32)
    o_ref[0] = x2

    bucket, w_a, w_b = _route(_rms(x2, gffn_ref[...]), wr_hi_ref, wr_lo_ref, br_ref)
    lane = lax.broadcasted_iota(jnp.int32, (x2.shape[0], LANES), 1)
    onehot = lane == bucket
    oh = jnp.where(onehot, 1.0, 0.0)
    before = jnp.dot(tri_ref[...], oh.astype(BF16), preferred_element_type=F32) + cnt_ref[0:1, :]
    rank = jnp.sum(jnp.where(onehot, before, 0.0), axis=1, keepdims=True)
    cnt_ref[0:1, :] += jnp.sum(oh, axis=0, keepdims=True)
    route_ref[0] = (jnp.where(lane == 0, bucket.astype(F32), 0.0) + jnp.where(lane == 1, rank, 0.0)
                    + jnp.where(lane == 2, w_a, 0.0) + jnp.where(lane == 3, w_b, 0.0))


def _post_mix(x, a_list, w_list, g, wq, mk, mv, wo, gffn, wr_hi, wr_lo, br):
    b, s, d = x.shape
    tm = TOKEN_TILE
    n_a = len(a_list)
    tok = lambda w: pl.BlockSpec((1, tm, w), lambda i, j: (i, j, 0))
    memspec = pl.BlockSpec((1, MEM_TOKENS, XA_INNER), lambda i, j: (i, 0, 0))
    tri = jnp.tril(jnp.ones((tm, tm), F32), -1).astype(BF16)
    return pl.pallas_call(
        functools.partial(_post_mix_kernel, n_a=n_a),
        out_shape=[jax.ShapeDtypeStruct((b, s, d), F32), jax.ShapeDtypeStruct((b, s, LANES), F32),
                   jax.ShapeDtypeStruct((8, LANES), F32)],
        grid=(b, s // tm),
        in_specs=([tok(d)] + [tok(a.shape[2]) for a in a_list] + [_full(w.shape) for w in w_list]
                  + [_full(g.shape), _full(wq.shape), memspec, memspec, _full(wo.shape), _full(gffn.shape),
                     _full(wr_hi.shape), _full(wr_lo.shape), _full(br.shape), _full(tri.shape)]),
        out_specs=[tok(d), tok(LANES), _full((8, LANES))],
        compiler_params=_cparams(("arbitrary", "arbitrary")),
        name="post_mix",
    )(x, *a_list, *w_list, g, wq, mk, mv, wo, gffn, wr_hi, wr_lo, br, tri)


def _route(tf, wr_hi_ref, wr_lo_ref, br_ref):
    t_hi = tf.astype(BF16)
    t_lo = (tf - t_hi.astype(F32)).astype(BF16)
    w_hi, w_lo = wr_hi_ref[...], wr_lo_ref[...]
    logits = (jnp.dot(t_hi, w_hi, preferred_element_type=F32) + jnp.dot(t_lo, w_hi, preferred_element_type=F32)
              + jnp.dot(t_hi, w_lo, preferred_element_type=F32)) + br_ref[...]
    lane = lax.broadcasted_iota(jnp.int32, logits.shape, 1)
    big = jnp.int32(LANES)
    is_g = lane < N_GROUPS
    lg = jnp.where(is_g, logits, NEG_INF)
    gmax = jnp.max(lg, axis=1, keepdims=True)
    g_p = 1.0 / jnp.sum(jnp.where(is_g, jnp.exp(lg - gmax), 0.0), axis=1, keepdims=True)
    g_i = jnp.min(jnp.where(lg == gmax, lane, big), axis=1, keepdims=True)
    lo = N_GROUPS + EXPERTS_PER_GROUP * g_i
    in_grp = (lane >= lo) & (lane < lo + EXPERTS_PER_GROUP)
    el = jnp.where(in_grp, logits, NEG_INF)
    v1 = jnp.max(el, axis=1, keepdims=True)
    i1 = jnp.min(jnp.where(el == v1, lane, big), axis=1, keepdims=True)
    el2 = jnp.where(lane == i1, NEG_INF, el)
    v2 = jnp.max(el2, axis=1, keepdims=True)
    i2 = jnp.min(jnp.where(el2 == v2, lane, big), axis=1, keepdims=True)
    e2 = jnp.exp(v2 - v1)
    w1 = g_p / (1.0 + e2)
    w2 = g_p * e2 / (1.0 + e2)
    first_low = i1 < i2
    e_lo = jnp.minimum(i1, i2) - lo
    e_hi = jnp.maximum(i1, i2) - lo
    pair = jnp.where(e_lo == 0, 0, jnp.where(e_lo == 1, 3, 5)) + e_hi - e_lo - 1
    bucket = g_i * PAIRS_PER_GROUP + pair
    return bucket, jnp.where(first_low, w1, w2), jnp.where(first_low, w2, w1)


def _moe_kernel(ea_ref, eb_ref, valid_ref, src_hbm, x_hbm, gate_ref, g_ref, wgu_a_ref, wgu_b_ref,
                wd_a_ref, wd_b_ref, gf_ref, out_hbm, idx_ref, xb0, xb1, ob0, ob1, isem, gsem, ssem, *, n_tiles,
                final_norm):
    n_grp, sub, d = xb0.shape
    tms = n_grp * sub
    i = pl.program_id(0)

    def idx_off(k):
        return lax.rem(k, MOE_IDX_SLOTS) * tms

    def idx_copy(k):
        return pltpu.make_async_copy(src_hbm.at[k], idx_ref.at[pl.ds(idx_off(k), tms)],
                                     isem.at[lax.rem(k, MOE_IDX_SLOTS)])

    def gather_row(tok, xb, s, j, u):
        return pltpu.make_async_copy(x_hbm.at[pl.ds(tok, 1)], xb.at[j, pl.ds(u, 1)], gsem.at[s])

    def scatter_row(tok, ob, s, j, u):
        return pltpu.make_async_copy(ob.at[j, pl.ds(u, 1)], out_hbm.at[pl.ds(tok, 1)], ssem.at[s])

    def start_gather(k, xb, s):
        base = idx_off(k)
        for r in range(tms):
            gather_row(idx_ref[base + r], xb, s, r // sub, r % sub).start(priority=r % 2)

    def wait_gather(xb, s):
        for r in range(tms):
            gather_row(0, xb, s, r // sub, r % sub).wait()

    def for_rows(n, fn):
        @pl.when(n == tms)
        def _():
            for r in range(tms):
                fn(r // sub, r % sub, r)

        @pl.when(n < tms)
        def _():
            full = lax.shift_right_logical(n, 3)

            def grp(j, carry):
                for u in range(sub):
                    fn(j, u, j * sub + u)
                return carry
            lax.fori_loop(0, full, grp, 0)

            def one(u, carry):
                fn(full, u, full * sub + u)
                return carry
            lax.fori_loop(0, lax.bitwise_and(n, sub - 1), one, 0)

    def start_scatter(k, n, ob, s):
        base = idx_off(k)

        def fn(j, u, r):
            prio = r % 2 if isinstance(r, int) else 0
            scatter_row(idx_ref[base + r], ob, s, j, u).start(priority=prio)
        for_rows(n, fn)

    def wait_scatter(n, ob, s):
        for_rows(n, lambda j, u, r: scatter_row(0, ob, s, j, u).wait())

    def tile(s):
        xb, ob = (xb0, ob0) if s == 0 else (xb1, ob1)
        xo, obo = (xb1, ob1) if s == 0 else (xb0, ob0)
        if s == 0:
            @pl.when(i == 0)
            def _():
                idx_copy(0).start()
                idx_copy(0).wait()
                start_gather(0, xb0, 0)
                idx_copy(1).start()

        @pl.when(i + 2 <= n_tiles)
        def _():
            idx_copy(i + 2).start()

        wait_gather(xb, s)

        @pl.when(i >= 2)
        def _():
            wait_scatter(valid_ref[jnp.maximum(i - 2, 0)], ob, s)

        idx_copy(i + 1).wait()
        n_valid = valid_ref[i]
        next_base = idx_off(i + 1)
        chunk = tms // MOE_GATHER_CHUNKS

        def prefetch_rows(c):
            for r in range(c * chunk, (c + 1) * chunk):
                gather_row(idx_ref[next_base + r], xo, 1 - s, r // sub, r % sub).start(priority=r % 2)

        @pl.when(n_valid > 0)
        def _():
            x = xb[...].reshape(tms, d)
            prefetch_rows(0)
            t = _rms(x, g_ref[...]).astype(BF16)
            gates = gate_ref[...]
            prefetch_rows(1)
            y = x
            for e, (wgu_ref, wd_ref) in enumerate(((wgu_a_ref, wd_a_ref), (wgu_b_ref, wd_b_ref))):
                hgu = jnp.dot(t, wgu_ref[0], preferred_element_type=F32)
                prefetch_rows(2 + 3 * e)
                hg, hu = hgu[:, 0:D_EXPERT], hgu[:, D_EXPERT:2 * D_EXPERT]
                act = (hg * jax.nn.sigmoid(hg) * hu * gates[:, e:e + 1]).astype(BF16)
                prefetch_rows(3 + 3 * e)
                y = y + jnp.dot(act, wd_ref[0], preferred_element_type=F32)
                prefetch_rows(4 + 3 * e)
            if final_norm:
                y = _rms(y, gf_ref[...])
            ob[...] = y.reshape(n_grp, sub, d)

        @pl.when(n_valid <= 0)
        def _():
            for c in range(MOE_GATHER_CHUNKS):
                prefetch_rows(c)

        start_scatter(i, n_valid, ob, s)

        @pl.when(i == n_tiles - 1)
        def _():
            wait_gather(xo, 1 - s)
            if n_tiles > 1:
                wait_scatter(valid_ref[jnp.maximum(i - 1, 0)], obo, 1 - s)
            wait_scatter(n_valid, ob, s)

    parity = lax.rem(i, 2)
    pl.when(parity == 0)(lambda: tile(0))
    pl.when(parity == 1)(lambda: tile(1))


def _moe_plan(route, counts, n_tok):
    tms = MOE_TILE
    n_b = N_GROUPS * PAIRS_PER_GROUP
    n_tiles = n_tok // tms + n_b
    bucket = route[:, 0].astype(jnp.int32)
    rank = route[:, 1].astype(jnp.int32)
    cnt = counts[0, :n_b].astype(jnp.int32)
    padded = (cnt + tms - 1) // tms * tms
    pad_end = jnp.cumsum(padded)
    pad_off = pad_end - padded
    dest = pad_off[bucket] + rank
    payload = jnp.concatenate([jnp.arange(n_tok, dtype=F32)[:, None], route[:, 2:4]], axis=1)
    plan = jnp.zeros(((n_tiles + 1) * tms, 3), F32).at[dest].set(payload, unique_indices=True)
    src = plan[:, 0].astype(jnp.int32).reshape(n_tiles + 1, tms)
    gates = plan[:n_tiles * tms, 1:3]
    start = jnp.arange(n_tiles, dtype=jnp.int32) * tms
    tb = jnp.minimum(jnp.sum(pad_end[None, :] <= start[:, None], axis=1), n_b - 1).astype(jnp.int32)
    valid = jnp.clip(cnt[tb] - (start - pad_off[tb]), 0, tms).astype(jnp.int32)
    lo_hi = [(a, b) for a in range(EXPERTS_PER_GROUP) for b in range(a + 1, EXPERTS_PER_GROUP)]
    ea_tab = np.array([EXPERTS_PER_GROUP * g + a for g in range(N_GROUPS) for a, _ in lo_hi], np.int32)
    eb_tab = np.array([EXPERTS_PER_GROUP * g + b for g in range(N_GROUPS) for _, b in lo_hi], np.int32)
    return src, gates, jnp.asarray(ea_tab)[tb], jnp.asarray(eb_tab)[tb], valid


def _moe(x2d, route, counts, g, wgu, wd, gf, final_norm):
    n_tok, d = x2d.shape
    tms = MOE_TILE
    src, gates, ea, eb, valid = _moe_plan(route, counts, n_tok)
    n_tiles = src.shape[0] - 1
    any_spec = pl.BlockSpec(memory_space=pl.ANY)
    grid_spec = pltpu.PrefetchScalarGridSpec(
        num_scalar_prefetch=3,
        grid=(n_tiles,),
        in_specs=[any_spec, any_spec,
                  pl.BlockSpec((tms, 2), lambda i, ea_r, eb_r, v_r: (i, 0)),
                  pl.BlockSpec(g.shape, lambda i, ea_r, eb_r, v_r: (0, 0)),
                  pl.BlockSpec((1, d, 2 * D_EXPERT), lambda i, ea_r, eb_r, v_r: (ea_r[i], 0, 0)),
                  pl.BlockSpec((1, d, 2 * D_EXPERT), lambda i, ea_r, eb_r, v_r: (eb_r[i], 0, 0)),
                  pl.BlockSpec((1, D_EXPERT, d), lambda i, ea_r, eb_r, v_r: (ea_r[i], 0, 0)),
                  pl.BlockSpec((1, D_EXPERT, d), lambda i, ea_r, eb_r, v_r: (eb_r[i], 0, 0)),
                  pl.BlockSpec(gf.shape, lambda i, ea_r, eb_r, v_r: (0, 0))],
        out_specs=any_spec,
        scratch_shapes=[pltpu.SMEM((MOE_IDX_SLOTS * tms,), jnp.int32)]
        + [pltpu.VMEM((tms // SUBLANES, SUBLANES, d), F32)] * 4
        + [pltpu.SemaphoreType.DMA((MOE_IDX_SLOTS,)), pltpu.SemaphoreType.DMA((2,)),
           pltpu.SemaphoreType.DMA((2,))],
    )
    return pl.pallas_call(
        functools.partial(_moe_kernel, n_tiles=n_tiles, final_norm=final_norm),
        out_shape=jax.ShapeDtypeStruct((n_tok, d), F32),
        grid_spec=grid_spec,
        compiler_params=_cparams(("arbitrary",)),
        name="moe",
    )(ea, eb, valid, src, x2d, gates, g, wgu, wgu, wd, wd, gf)


def _row(v):
    return v.reshape(1, -1).astype(F32)


def _rope_tables(seq_len, rot_dim, lane_starts):
    inv = ROPE_THETA ** (-jnp.arange(0, rot_dim, 2, dtype=F32) / rot_dim)
    ang = jnp.arange(seq_len, dtype=F32)[:, None] * inv[None, :]
    cos, sin = jnp.cos(ang), jnp.sin(ang)
    ct = jnp.ones((seq_len, LANES), F32)
    st = jnp.zeros((seq_len, LANES), F32)
    for l0 in lane_starts:
        ct = ct.at[:, l0:l0 + rot_dim].set(jnp.concatenate([cos, cos], axis=1))
        st = st.at[:, l0:l0 + rot_dim].set(jnp.concatenate([-sin, sin], axis=1))
    return ct, st


def _prep_even(w_in, w_uq, w_ukv, w_o):
    c = np.cumsum([0, MLA_Q_RANK, MLA_KV_RANK, MLA_ROPE, DIFF_QK, DIFF_QK, DIFF_VW])
    kpe = jnp.zeros((D_MODEL, LANES), F32).at[:, MLA_NOPE:MLA_NOPE + MLA_ROPE].set(w_in[:, c[2]:c[3]])
    win = jnp.concatenate([w_in[:, c[0]:c[2]], w_in[:, c[3]:c[6]], kpe], axis=1).astype(BF16)
    uq = w_uq.reshape(MLA_Q_RANK, MLA_HEADS, MLA_NOPE + MLA_ROPE)
    uq = jnp.pad(uq, ((0, 0), (0, 0), (0, LANES - MLA_NOPE - MLA_ROPE))).reshape(MLA_Q_RANK, MLA_HEADS * LANES)
    ukv = w_ukv.reshape(MLA_KV_RANK, MLA_HEADS, MLA_NOPE + MLA_V)
    uk = jnp.pad(ukv[:, :, :MLA_NOPE], ((0, 0), (0, 0), (0, LANES - MLA_NOPE))).reshape(MLA_KV_RANK,
                                                                                       MLA_HEADS * LANES)
    uv = ukv[:, :, MLA_NOPE:].reshape(MLA_KV_RANK, MLA_HEADS * MLA_V)
    n_mla = MLA_HEADS * MLA_V
    return (win, uq.astype(BF16), uk.astype(BF16), uv.astype(BF16),
            w_o[:n_mla].astype(BF16), w_o[n_mla:].astype(BF16))


def _prep_moe(w_rg, b_rg, w_re, b_re, w_gate, w_up, w_down):
    wr = jnp.zeros((D_MODEL, LANES), F32).at[:, :N_GROUPS].set(w_rg).at[:, N_GROUPS:N_GROUPS + N_EXPERTS].set(w_re)
    br = jnp.zeros((1, LANES), F32).at[0, :N_GROUPS].set(b_rg).at[0, N_GROUPS:N_GROUPS + N_EXPERTS].set(b_re)
    wr_hi = wr.astype(BF16)
    wr_lo = (wr - wr_hi.astype(F32)).astype(BF16)
    wgu = jnp.concatenate([w_gate, w_up], axis=-1).reshape(N_EXPERTS, D_MODEL, 2 * D_EXPERT).astype(BF16)
    wd = w_down.reshape(N_EXPERTS, D_EXPERT, D_MODEL).astype(BF16)
    return wr_hi, wr_lo, br, wgu, wd


def _trunk(x, mem, p, tables):
    b, s, d = x.shape
    depth = p['ln_mix'].shape[0]
    for layer in range(depth):
        i = layer // 2
        if layer % 2 == 0:
            win, uq, uk, uv, wo_a, wo_b = _prep_even(p['w_in_even'][i], p['mla_w_uq'][i], p['mla_w_ukv'][i],
                                                     p['w_o_even'][i])
            cm, sm, cd, sd = tables[s]
            q, k, v, dq, dk, dv = _even_in(x, _row(p['ln_mix'][layer]), win, _row(p['mla_q_norm'][i]),
                                           _row(p['mla_kv_norm'][i]), uq, uk, uv, cm, sm, cd, sd)
            o_mla = _flash(q, k, v, 2 * LANES)
            lamp = jnp.zeros((8, LANES), F32)
            for r, name in enumerate(('diff_lambda_q1', 'diff_lambda_k1', 'diff_lambda_q2', 'diff_lambda_k2')):
                lamp = lamp.at[r, :DIFF_HEAD_DIM].set(p[name][i])
            lam_init = 0.8 - 0.6 * math.exp(-0.3 * layer)
            o_diff = _flash(dq, dk, dv, LANES, lamp=lamp, subln=_row(p['diff_subln'][i]), lam_init=lam_init)
            a_list, w_list = [o_mla, o_diff], [wo_a, wo_b]
        else:
            q, k, v = _odd_in(x, _row(p['ln_mix'][layer]), p['w_in_odd'][i].astype(BF16))
            tbl, idx = _na_tables(p['na_rpb'][i])
            a_list, w_list = [_na_attention(q, k, v, tbl, idx)], [p['w_o_odd'][i].astype(BF16)]
        mk, mv = _mem_kv(mem, _row(p['ln_mem'][layer]), p['xa_w_kv'][layer].astype(BF16))
        wr_hi, wr_lo, br, wgu, wd = _prep_moe(p['moe_w_group'][layer], p['moe_b_group'][layer],
                                              p['moe_w_expert'][layer], p['moe_b_expert'][layer],
                                              p['moe_w_gate'][layer], p['moe_w_up'][layer], p['moe_w_down'][layer])
        x, route, counts = _post_mix(x, a_list, w_list, _row(p['ln_xattn'][layer]),
                                     p['xa_w_q'][layer].astype(BF16), mk, mv, p['xa_w_o'][layer].astype(BF16),
                                     _row(p['ln_ffn'][layer]), wr_hi, wr_lo, br)
        x = _moe(x.reshape(b * s, d), route.reshape(b * s, LANES), counts, _row(p['ln_ffn'][layer]), wgu, wd,
                 _row(p['ln_final']), final_norm=(layer == depth - 1)).reshape(b, s, d)
    return x


def kernel(x_prompt, x_sample, mem_prompt, mem_sample, ln_mix, w_in_even, mla_q_norm, mla_kv_norm, mla_w_uq, mla_w_ukv, diff_lambda_q1, diff_lambda_k1, diff_lambda_q2, diff_lambda_k2, diff_subln, w_o_even, w_in_odd, na_rpb, w_o_odd, ln_xattn, ln_mem, xa_w_q, xa_w_kv, xa_w_o, ln_ffn, moe_w_group, moe_b_group, moe_w_expert, moe_b_expert, moe_w_gate, moe_w_up, moe_w_down, ln_final):
    p = dict(ln_mix=ln_mix, w_in_even=w_in_even, mla_q_norm=mla_q_norm, mla_kv_norm=mla_kv_norm,
             mla_w_uq=mla_w_uq, mla_w_ukv=mla_w_ukv, diff_lambda_q1=diff_lambda_q1,
             diff_lambda_k1=diff_lambda_k1, diff_lambda_q2=diff_lambda_q2, diff_lambda_k2=diff_lambda_k2,
             diff_subln=diff_subln, w_o_even=w_o_even, w_in_odd=w_in_odd, na_rpb=na_rpb, w_o_odd=w_o_odd,
             ln_xattn=ln_xattn, ln_mem=ln_mem, xa_w_q=xa_w_q, xa_w_kv=xa_w_kv, xa_w_o=xa_w_o,
             ln_ffn=ln_ffn, moe_w_group=moe_w_group, moe_b_group=moe_b_group, moe_w_expert=moe_w_expert,
             moe_b_expert=moe_b_expert, moe_w_gate=moe_w_gate, moe_w_up=moe_w_up, moe_w_down=moe_w_down,
             ln_final=ln_final)
    tables = {}
    for s in {x_prompt.shape[1], x_sample.shape[1]}:
        cm, sm = _rope_tables(s, MLA_ROPE, (MLA_NOPE,))
        cd, sd = _rope_tables(s, DIFF_ROT, (0, DIFF_HEAD_DIM))
        tables[s] = (cm, sm, cd, sd)
    return (_trunk(x_prompt, mem_prompt, p, tables), _trunk(x_sample, mem_sample, p, tables))
```

```python
import functools
import math

import numpy as np
import jax
import jax.numpy as jnp
from jax import lax
from jax.experimental import pallas as pl
from jax.experimental.pallas import tpu as pltpu

F32 = jnp.float32
BF16 = jnp.bfloat16

D_MODEL = 1024
ROPE_THETA = 500000.0
NORM_EPS = 1e-6
NEG_INF = -1e30
LOG2E = math.log2(math.e)

MLA_HEADS = 8
MLA_Q_RANK = 384
MLA_KV_RANK = 256
MLA_NOPE = 64
MLA_ROPE = 32
MLA_V = 64
DIFF_HEADS = 4
DIFF_HEAD_DIM = 64
DIFF_ROT = DIFF_HEAD_DIM // 4
DIFF_QK = DIFF_HEADS * 2 * DIFF_HEAD_DIM
DIFF_VW = DIFF_HEADS * 2 * DIFF_HEAD_DIM
GRID_W = 64
NA_HEADS = 16
NA_HEAD_DIM = 64
WIN_ROWS = 8
WIN_COLS = 16
MEM_TOKENS = 256
XA_HEADS = 4
XA_HEAD_DIM = 128
XA_INNER = XA_HEADS * XA_HEAD_DIM
N_GROUPS = 4
EXPERTS_PER_GROUP = 4
N_EXPERTS = N_GROUPS * EXPERTS_PER_GROUP
PAIRS_PER_GROUP = EXPERTS_PER_GROUP * (EXPERTS_PER_GROUP - 1) // 2
ROUTE_ROWS = 32
ROUTE_EXPERT_ROW0 = 8
D_EXPERT = 256

LANES = 128
SUBLANES = 8
VMEM_LIMIT = 56 * 1024 * 1024

TOKEN_TILE = 512
MOE_TILE = 256
MOE_GATHER_CHUNKS = 8
MOE_IDX_SLOTS = 4
FLASH_TQ = 512
FLASH_TK = 512
FLASH_BOUND_SLACK = 1.01
FLASH_MIN_ROW_SUM = 2.0 ** -60
FLASH_UNROLL = 8
NA_ROW_BLOCK = 8
NA_KEY_ROWS = 2 * WIN_ROWS
NA_GROUP_ROWS = 4
NA_GROUP_KEY_ROWS = NA_GROUP_ROWS + WIN_ROWS
NA_TILE_CODES = 3
NA_TILES_PER_HEAD = NA_TILE_CODES * 16 + 1


def _cparams(sem):
    return pltpu.CompilerParams(dimension_semantics=sem, vmem_limit_bytes=VMEM_LIMIT)


def _rms(xf, g):
    return xf * lax.rsqrt(jnp.mean(xf * xf, axis=-1, keepdims=True) + NORM_EPS) * g


def _full(shape):
    nd = len(shape)
    return pl.BlockSpec(shape, lambda *_: (0,) * nd)


def _rope_block(x, cos, sin, half):
    lane = lax.broadcasted_iota(jnp.int32, x.shape, 1)
    up = pltpu.roll(x, LANES - half, 1)
    down = pltpu.roll(x, half, 1)
    first = lax.bitwise_and(lane, 2 * half - 1) < half
    return x * cos + jnp.where(first, up, down) * sin


def _even_in_kernel(x_ref, g_ref, win_ref, qn_ref, kvn_ref, wuq_ref, wuk_ref, wuv_ref,
                    cm_ref, sm_ref, cd_ref, sd_ref,
                    q_ref, k_ref, v_ref, dq_ref, dk_ref, dv_ref):
    h = _rms(x_ref[0], g_ref[...]).astype(BF16)
    proj = jnp.dot(h, win_ref[...], preferred_element_type=F32)
    o_cq, o_ckv, o_dq, o_dk, o_dv, o_kpe = 0, 384, 640, 1152, 1664, 2176
    cm, sm, cd, sd = cm_ref[...], sm_ref[...], cd_ref[...], sd_ref[...]

    cq = _rms(proj[:, o_cq:o_cq + MLA_Q_RANK], qn_ref[...]).astype(BF16)
    q = jnp.dot(cq, wuq_ref[...], preferred_element_type=F32)
    mla_c = (MLA_NOPE + MLA_ROPE) ** -0.5 * LOG2E
    for hd in range(MLA_HEADS):
        sl = slice(hd * LANES, (hd + 1) * LANES)
        q_ref[0, :, sl] = (_rope_block(q[:, sl], cm, sm, MLA_ROPE // 2) * mla_c).astype(BF16)

    ckv = _rms(proj[:, o_ckv:o_ckv + MLA_KV_RANK], kvn_ref[...]).astype(BF16)
    kn = jnp.dot(ckv, wuk_ref[...], preferred_element_type=F32)
    kpe = _rope_block(proj[:, o_kpe:o_kpe + LANES], cm, sm, MLA_ROPE // 2)
    for hd in range(MLA_HEADS):
        sl = slice(hd * LANES, (hd + 1) * LANES)
        k_ref[0, :, sl] = (kn[:, sl] + kpe).astype(BF16)
    v_ref[0] = jnp.dot(ckv, wuv_ref[...], preferred_element_type=F32).astype(BF16)

    diff_c = DIFF_HEAD_DIM ** -0.5 * LOG2E
    for hd in range(DIFF_HEADS):
        sl = slice(hd * LANES, (hd + 1) * LANES)
        dq_ref[0, :, sl] = (_rope_block(proj[:, o_dq + hd * LANES:o_dq + (hd + 1) * LANES], cd, sd,
                                        DIFF_ROT // 2) * diff_c).astype(BF16)
        dk_ref[0, :, sl] = _rope_block(proj[:, o_dk + hd * LANES:o_dk + (hd + 1) * LANES], cd, sd,
                                       DIFF_ROT // 2).astype(BF16)
    dv_ref[0] = proj[:, o_dv:o_dv + DIFF_VW].astype(BF16)


def _even_in(x, g, win, qn, kvn, wuq, wuk, wuv, cm, sm, cd, sd):
    b, s, d = x.shape
    tm = TOKEN_TILE
    tok = lambda w: pl.BlockSpec((1, tm, w), lambda i, j: (i, j, 0))
    tab = pl.BlockSpec((tm, LANES), lambda i, j: (j, 0))
    outs = [jax.ShapeDtypeStruct((b, s, w), BF16) for w in (1024, 1024, 512, 512, 512, 512)]
    return pl.pallas_call(
        _even_in_kernel,
        out_shape=outs,
        grid=(b, s // tm),
        in_specs=[tok(d), _full(g.shape), _full(win.shape), _full(qn.shape), _full(kvn.shape),
                  _full(wuq.shape), _full(wuk.shape), _full(wuv.shape), tab, tab, tab, tab],
        out_specs=[tok(1024), tok(1024), tok(512), tok(512), tok(512), tok(512)],
        compiler_params=_cparams(("parallel", "parallel")),
        name="even_in",
    )(x, g, win, qn, kvn, wuq, wuk, wuv, cm, sm, cd, sd)


def _flash_kernel(*refs, tq, tk, n_k, width, diff, lam_init):
    if diff:
        q_ref, k_ref, v_ref, lamp_ref, g_ref, o_ref, qs_ref, m_ref, l_ref, acc_ref, kn_ref, kmax_ref = refs
    else:
        q_ref, k_ref, v_ref, o_ref, qs_ref, m_ref, l_ref, acc_ref, kn_ref, kmax_ref = refs
    half = width // 2
    n_j = tk // LANES
    q = q_ref[0]
    lane = lax.broadcasted_iota(jnp.int32, (tq, width), 1)
    zero = jnp.zeros_like(q)
    qs_ref[0:tq, :] = jnp.where(lane < half, q, zero)
    qs_ref[tq:2 * tq, :] = jnp.where(lane >= half, q, zero)

    @pl.when(pl.program_id(2) == 0)
    def _():
        r_i = lax.broadcasted_iota(jnp.int32, (width, LANES), 0)
        c_i = lax.broadcasted_iota(jnp.int32, (width, LANES), 1)
        sel = jnp.where(((c_i == 0) & (r_i < half)) | ((c_i == 1) & (r_i >= half)), 1.0, 0.0).astype(BF16)
        kn_ref[...] = jnp.zeros(kn_ref.shape, F32)

        def knorm(i, carry):
            kc = k_ref[0, pl.ds(pl.multiple_of(i * tk, tk), tk), :].astype(F32)
            kn_ref[...] = jnp.maximum(kn_ref[...], jnp.dot((kc * kc).astype(BF16), sel, preferred_element_type=F32))
            return carry

        lax.fori_loop(0, n_k, knorm, 0)
        kmax = jnp.sqrt(jnp.max(kn_ref[...], axis=0, keepdims=True))
        kmax_ref[0:1, :] = jnp.broadcast_to(kmax[:, 0:1], (1, LANES))
        kmax_ref[1:2, :] = jnp.broadcast_to(kmax[:, 1:2], (1, LANES))

    def reset(m):
        m_ref[...] = m
        l_ref[...] = jnp.zeros(l_ref.shape, F32)
        acc_ref[...] = jnp.zeros(acc_ref.shape, F32)

    def scores(i):
        off = pl.multiple_of(i * tk, tk)
        kc = k_ref[0, pl.ds(off, tk), :]
        return off, lax.dot_general(qs_ref[...], kc, (((1,), (1,)), ((), ())), preferred_element_type=F32)

    def max_pass(i, carry):
        _, s = scores(i)
        m = m_ref[...]
        for j in range(n_j):
            m = jnp.maximum(m, s[:, j * LANES:(j + 1) * LANES])
        m_ref[...] = m
        return carry

    def pv_pass(i, carry):
        off, s = scores(i)
        vc = v_ref[0, pl.ds(off, tk), :]
        m = m_ref[...]
        ps = [jnp.exp2(s[:, j * LANES:(j + 1) * LANES] - m) for j in range(n_j)]
        lsum = ps[0]
        for pj in ps[1:]:
            lsum = lsum + pj
        l_ref[...] += lsum
        p = jnp.concatenate([pj.astype(BF16) for pj in ps], axis=1)
        acc_ref[...] += jnp.dot(p, vc, preferred_element_type=F32)
        return carry

    qf = qs_ref[...].astype(F32)
    qn = jnp.sqrt(jnp.sum(qf * qf, axis=1, keepdims=True))
    bound_top = qn[0:tq] * kmax_ref[0:1, :] * FLASH_BOUND_SLACK
    bound_bot = qn[tq:2 * tq] * kmax_ref[1:2, :] * FLASH_BOUND_SLACK
    reset(jnp.concatenate([bound_top, bound_bot], axis=0))
    lax.fori_loop(0, n_k, pv_pass, 0, unroll=FLASH_UNROLL)
    l_min = jnp.min(jnp.sum(l_ref[...], axis=1, keepdims=True))

    @pl.when(jnp.logical_not(l_min >= FLASH_MIN_ROW_SUM))
    def _():
        reset(jnp.full(m_ref.shape, NEG_INF, F32))
        lax.fori_loop(0, n_k, max_pass, 0, unroll=FLASH_UNROLL)
        m_ref[...] = jnp.broadcast_to(jnp.max(m_ref[...], axis=1, keepdims=True), m_ref.shape)
        lax.fori_loop(0, n_k, pv_pass, 0, unroll=FLASH_UNROLL)

    o = acc_ref[...] / jnp.sum(l_ref[...], axis=1, keepdims=True)
    top, bot = o[0:tq], o[tq:2 * tq]
    if diff:
        lp = lamp_ref[...]
        lam = (jnp.exp(jnp.sum(lp[0:1] * lp[1:2], axis=1, keepdims=True))
               - jnp.exp(jnp.sum(lp[2:3] * lp[3:4], axis=1, keepdims=True)) + lam_init)
        od = top - lam * bot
        o_ref[0] = (_rms(od, g_ref[...]) * (1.0 - lam_init)).astype(BF16)
    else:
        lane_o = lax.broadcasted_iota(jnp.int32, (tq, LANES), 1)
        o_ref[0] = jnp.where(lane_o < LANES // 2, top, bot).astype(BF16)


def _flash(q, k, v, width, lamp=None, subln=None, lam_init=0.0):
    b, s, _ = q.shape
    n = q.shape[2] // width
    tq, tk = FLASH_TQ, FLASH_TK
    diff = lamp is not None
    kern = functools.partial(_flash_kernel, tq=tq, tk=tk, n_k=s // tk, width=width, diff=diff,
                             lam_init=lam_init)
    in_specs = [pl.BlockSpec((1, tq, width), lambda bi, p, qi: (bi, qi, p)),
                pl.BlockSpec((1, s, width), lambda bi, p, qi: (bi, 0, p)),
                pl.BlockSpec((1, s, LANES), lambda bi, p, qi: (bi, 0, p))]
    args = [q, k, v]
    if diff:
        in_specs += [_full(lamp.shape), _full(subln.shape)]
        args += [lamp, subln]
    return pl.pallas_call(
        kern,
        out_shape=jax.ShapeDtypeStruct((b, s, n * LANES), BF16),
        grid=(b, n, s // tq),
        in_specs=in_specs,
        out_specs=pl.BlockSpec((1, tq, LANES), lambda bi, p, qi: (bi, qi, p)),
        scratch_shapes=[pltpu.VMEM((2 * tq, width), BF16), pltpu.VMEM((2 * tq, LANES), F32),
                        pltpu.VMEM((2 * tq, LANES), F32), pltpu.VMEM((2 * tq, LANES), F32),
                        pltpu.VMEM((tk, LANES), F32), pltpu.VMEM((SUBLANES, LANES), F32)],
        compiler_params=_cparams(("parallel", "parallel", "arbitrary")),
        name="flash_diff" if diff else "flash_mla",
    )(*args)


def _odd_in_kernel(x_ref, g_ref, w_ref, q_ref, k_ref, v_ref):
    h = _rms(x_ref[0], g_ref[...]).astype(BF16)
    qkv = jnp.dot(h, w_ref[...], preferred_element_type=F32)
    c = NA_HEAD_DIM ** -0.5 * LOG2E
    q_ref[0] = (qkv[:, 0:1024] * c).astype(BF16)
    k_ref[0] = qkv[:, 1024:2048].astype(BF16)
    v_ref[0] = qkv[:, 2048:3072].astype(BF16)


def _odd_in(x, g, w):
    b, s, d = x.shape
    tm = TOKEN_TILE
    tok = pl.BlockSpec((1, tm, d), lambda i, j: (i, j, 0))
    return pl.pallas_call(
        _odd_in_kernel,
        out_shape=[jax.ShapeDtypeStruct((b, s, d), BF16)] * 3,
        grid=(b, s // tm),
        in_specs=[tok, _full(g.shape), _full(w.shape)],
        out_specs=[tok, tok, tok],
        compiler_params=_cparams(("parallel", "parallel")),
        name="odd_in",
    )(x, g, w)


def _na_kernel(idx_ref, q_ref, k_ref, v_ref, tbl_ref, o_ref, qs_ref, os_ref, *, rows, n_rb):
    nq = NA_ROW_BLOCK * GRID_W
    nk = NA_KEY_ROWS * GRID_W
    gq = NA_GROUP_ROWS * GRID_W
    n_kp = NA_KEY_ROWS // 2
    rb = pl.program_id(2)
    ks = jnp.clip(NA_ROW_BLOCK * rb - WIN_ROWS // 2, 0, rows - NA_KEY_ROWS)
    pat = jnp.where(rb == 0, 0, jnp.where(rb == n_rb - 1, 2, 1))
    off = pl.multiple_of(ks * GRID_W, (WIN_ROWS // 2) * GRID_W)
    q = q_ref[0]
    lane = lax.broadcasted_iota(jnp.int32, (nq, LANES), 1)
    zero = jnp.zeros_like(q)
    qs_ref[0:nq, :] = jnp.where(lane < LANES // 2, q, zero)
    qs_ref[nq:2 * nq, :] = jnp.where(lane >= LANES // 2, q, zero)

    def group(g, carry):
        r0 = pl.multiple_of(g * gq, gq)
        per_head = NA_ROW_BLOCK // NA_GROUP_ROWS
        hh = g // per_head
        i0 = (g % per_head) * NA_GROUP_ROWS
        w0 = jnp.clip(i0 + (pat - 1) * (WIN_ROWS // 2), 0, NA_KEY_ROWS - NA_GROUP_KEY_ROWS)
        goff = pl.multiple_of(off + w0 * GRID_W, (WIN_ROWS // 2) * GRID_W)
        kw = k_ref[0, pl.ds(goff, NA_GROUP_KEY_ROWS * GRID_W), :]
        vw = v_ref[0, pl.ds(goff, NA_GROUP_KEY_ROWS * GRID_W), :]
        s = lax.dot_general(qs_ref[pl.ds(r0, gq), :], kw, (((1,), (1,)), ((), ())), preferred_element_type=F32)
        bias_rows = []
        for ii in range(NA_GROUP_ROWS):
            base = pat * (NA_ROW_BLOCK * n_kp) + (i0 + ii) * n_kp + lax.shift_right_logical(w0, 1)
            bias_rows.append(jnp.concatenate(
                [tbl_ref[0, hh * NA_TILES_PER_HEAD + idx_ref[base + kp]]
                 for kp in range(NA_GROUP_KEY_ROWS // 2)], axis=1))
        s = s + jnp.concatenate(bias_rows, axis=0)
        m = jnp.max(s, axis=1, keepdims=True)
        p = jnp.exp2(s - m)
        linv = 1.0 / jnp.sum(p, axis=1, keepdims=True)
        os_ref[pl.ds(r0, gq), :] = jnp.dot(p.astype(BF16), vw, preferred_element_type=F32) * linv
        return carry

    lax.fori_loop(0, 2 * nq // gq, group, 0, unroll=True)
    o_ref[0] = jnp.where(lane < LANES // 2, os_ref[0:nq, :], os_ref[nq:2 * nq, :]).astype(BF16)


def _na_tables(rpb):
    col = np.arange(GRID_W)
    col_start = np.clip(col - WIN_COLS // 2, 0, GRID_W - WIN_COLS)
    col_valid = (col[None, :] >= col_start[:, None]) & (col[None, :] < col_start[:, None] + WIN_COLS)
    col_idx = np.clip(col[None, :] - col[:, None], -(WIN_COLS - 1), WIN_COLS - 1) + WIN_COLS - 1
    n_ri = 2 * WIN_ROWS - 1
    ri = np.zeros((NA_TILES_PER_HEAD, LANES), np.int32)
    ok = np.zeros((NA_TILES_PER_HEAD, LANES), bool)
    for code in range(NA_TILE_CODES):
        for a in range(-1, n_ri):
            t = code * 16 + a + 1
            ri[t, :GRID_W] = np.clip(a, 0, n_ri - 1)
            ri[t, GRID_W:] = np.clip(a + 1, 0, n_ri - 1)
            ok[t, :GRID_W] = code in (0, 1) and 0 <= a < n_ri
            ok[t, GRID_W:] = code in (0, 2) and 0 <= a + 1 < n_ri
    onehot = (col_idx.reshape(-1)[None, :] == np.arange(2 * WIN_COLS - 1)[:, None]).astype(np.float32)
    band = jnp.dot(rpb.reshape(NA_HEADS * n_ri, 2 * WIN_COLS - 1).astype(F32), jnp.asarray(onehot),
                   precision=lax.Precision.HIGHEST).reshape(NA_HEADS, n_ri, GRID_W, GRID_W)
    band = jnp.where(col_valid[None, None], band * LOG2E, NEG_INF)
    masked = jnp.full((NA_HEADS, GRID_W, GRID_W), NEG_INF, F32)
    tiles = []
    for t in range(NA_TILES_PER_HEAD):
        left = band[:, ri[t, 0]] if ok[t, 0] else masked
        right = band[:, ri[t, GRID_W]] if ok[t, GRID_W] else masked
        tiles.append(jnp.concatenate([left, right], axis=-1))
    tbl = jnp.stack(tiles, axis=1).reshape(NA_HEADS // 2, 2 * NA_TILES_PER_HEAD, GRID_W, LANES)

    idx = np.zeros((3, NA_ROW_BLOCK, NA_KEY_ROWS // 2), np.int32)
    for pat in range(3):
        for i in range(NA_ROW_BLOCK):
            rel = (i, i + WIN_ROWS // 2, i + WIN_ROWS)[pat]
            start = (max(i - WIN_ROWS // 2, 0), i, min(i + WIN_ROWS // 2, WIN_ROWS))[pat]
            for kp in range(NA_KEY_ROWS // 2):
                kk = 2 * kp
                a = kk - rel + WIN_ROWS - 1
                v0 = start <= kk < start + WIN_ROWS
                v1 = start <= kk + 1 < start + WIN_ROWS
                if v0:
                    assert 0 <= a < n_ri
                if v1:
                    assert 0 <= a + 1 < n_ri
                if v0 and v1:
                    idx[pat, i, kp] = a + 1
                elif v0:
                    idx[pat, i, kp] = 16 + a + 1
                elif v1:
                    idx[pat, i, kp] = 32 + a + 1
                else:
                    idx[pat, i, kp] = NA_TILES_PER_HEAD - 1
    return tbl, jnp.asarray(idx.reshape(-1))


def _na_attention(q, k, v, tbl, idx):
    b, s, d = q.shape
    rows = s // GRID_W
    n_rb = rows // NA_ROW_BLOCK
    nq = NA_ROW_BLOCK * GRID_W
    nk = NA_KEY_ROWS * GRID_W
    kern = functools.partial(_na_kernel, rows=rows, n_rb=n_rb)
    grid_spec = pltpu.PrefetchScalarGridSpec(
        num_scalar_prefetch=1,
        grid=(NA_HEADS // 2, b, n_rb),
        in_specs=[pl.BlockSpec((1, nq, LANES), lambda p, bi, r, idx_r: (bi, r, p)),
                  pl.BlockSpec((1, s, LANES), lambda p, bi, r, idx_r: (bi, 0, p)),
                  pl.BlockSpec((1, s, LANES), lambda p, bi, r, idx_r: (bi, 0, p)),
                  pl.BlockSpec((1, 2 * NA_TILES_PER_HEAD, GRID_W, LANES), lambda p, bi, r, idx_r: (p, 0, 0, 0))],
        out_specs=pl.BlockSpec((1, nq, LANES), lambda p, bi, r, idx_r: (bi, r, p)),
        scratch_shapes=[pltpu.VMEM((2 * nq, LANES), BF16), pltpu.VMEM((2 * nq, LANES), F32)],
    )
    return pl.pallas_call(
        kern,
        out_shape=jax.ShapeDtypeStruct((b, s, d), BF16),
        grid_spec=grid_spec,
        compiler_params=_cparams(("parallel", "parallel", "arbitrary")),
        name="na_attention",
    )(idx, q, k, v, tbl)


def _mem_kv_kernel(m_ref, g_ref, w_ref, k_ref, v_ref):
    h = _rms(m_ref[0], g_ref[...]).astype(BF16)
    kv = jnp.dot(h, w_ref[...], preferred_element_type=F32)
    k_ref[0] = kv[:, 0:XA_INNER].astype(BF16)
    v_ref[0] = kv[:, XA_INNER:2 * XA_INNER].astype(BF16)


def _mem_kv(mem, g, w):
    b, m, d = mem.shape
    return pl.pallas_call(
        _mem_kv_kernel,
        out_shape=[jax.ShapeDtypeStruct((b, m, XA_INNER), BF16)] * 2,
        grid=(b,),
        in_specs=[pl.BlockSpec((1, m, d), lambda i: (i, 0, 0)), _full(g.shape), _full(w.shape)],
        out_specs=[pl.BlockSpec((1, m, XA_INNER), lambda i: (i, 0, 0))] * 2,
        compiler_params=_cparams(("parallel",)),
        name="mem_kv",
    )(mem, g, w)


def _post_mix_kernel(*refs, n_a):
    x_ref = refs[0]
    a_refs = refs[1:1 + n_a]
    w_refs = refs[1 + n_a:1 + 2 * n_a]
    (g_ref, wq_ref, mk_ref, mv_ref, wo_ref, gffn_ref, wr_hi_ref, wr_lo_ref, br_ref, tri_ref,
     o_ref, route_ref, cnt_ref) = refs[1 + 2 * n_a:]

    @pl.when((pl.program_id(0) == 0) & (pl.program_id(1) == 0))
    def _():
        cnt_ref[...] = jnp.zeros(cnt_ref.shape, F32)

    x1 = x_ref[0]
    for a_ref, w_ref in zip(a_refs, w_refs):
        x1 = x1 + jnp.dot(a_ref[0], w_ref[...], preferred_element_type=F32)
    h = _rms(x1, g_ref[...]).astype(BF16)
    c = XA_HEAD_DIM ** -0.5 * LOG2E
    q = (jnp.dot(h, wq_ref[...], preferred_element_type=F32) * c).astype(BF16)
    outs = []
    for hd in range(XA_HEADS):
        sl = slice(hd * XA_HEAD_DIM, (hd + 1) * XA_HEAD_DIM)
        s = lax.dot_general(q[:, sl], mk_ref[0, :, sl], (((1,), (1,)), ((), ())), preferred_element_type=F32)
        p = jnp.exp2(s - jnp.max(s, axis=1, keepdims=True))
        linv = 1.0 / jnp.sum(p, axis=1, keepdims=True)
        o = jnp.dot(p.astype(BF16), mv_ref[0, :, sl], preferred_element_type=F32) * linv
        outs.append(o.astype(BF16))
    o_all = jnp.concatenate(outs, axis=1)
    x2 = x1 + jnp.dot(o_all, wo_ref[...], preferred_element_type=F32)
    o_ref[0] = x2

    bucket, w_a, w_b = _route(_rms(x2, gffn_ref[...]), wr_hi_ref, wr_lo_ref, br_ref)
    n = x2.shape[0]
    row = lax.broadcasted_iota(jnp.int32, (ROUTE_ROWS, n), 0)
    onehot = row == bucket
    oh = jnp.where(onehot, 1.0, 0.0)
    before = jnp.dot(oh.astype(BF16), tri_ref[...], preferred_element_type=F32) + cnt_ref[:, 0:1]
    rank = jnp.sum(jnp.where(onehot, before, 0.0), axis=0, keepdims=True)
    cnt_ref[...] += jnp.broadcast_to(jnp.sum(oh, axis=1, keepdims=True), cnt_ref.shape)
    row8 = lax.broadcasted_iota(jnp.int32, (SUBLANES, n), 0)
    route_ref[...] = (jnp.where(row8 == 0, bucket.astype(F32), 0.0) + jnp.where(row8 == 1, rank, 0.0)
                      + jnp.where(row8 == 2, w_a, 0.0) + jnp.where(row8 == 3, w_b, 0.0))


def _post_mix(x, a_list, w_list, g, wq, mk, mv, wo, gffn, wr_hi, wr_lo, br):
    b, s, d = x.shape
    tm = TOKEN_TILE
    n_a = len(a_list)
    nt = s // tm
    tok = lambda w: pl.BlockSpec((1, tm, w), lambda i, j: (i, j, 0))
    memspec = pl.BlockSpec((1, MEM_TOKENS, XA_INNER), lambda i, j: (i, 0, 0))
    tri = jnp.triu(jnp.ones((tm, tm), F32), 1).astype(BF16)
    return pl.pallas_call(
        functools.partial(_post_mix_kernel, n_a=n_a),
        out_shape=[jax.ShapeDtypeStruct((b, s, d), F32), jax.ShapeDtypeStruct((SUBLANES, b * s), F32),
                   jax.ShapeDtypeStruct((ROUTE_ROWS, LANES), F32)],
        grid=(b, nt),
        in_specs=([tok(d)] + [tok(a.shape[2]) for a in a_list] + [_full(w.shape) for w in w_list]
                  + [_full(g.shape), _full(wq.shape), memspec, memspec, _full(wo.shape), _full(gffn.shape),
                     _full(wr_hi.shape), _full(wr_lo.shape), _full(br.shape), _full(tri.shape)]),
        out_specs=[tok(d), pl.BlockSpec((SUBLANES, tm), lambda i, j: (0, i * nt + j)),
                   _full((ROUTE_ROWS, LANES))],
        compiler_params=_cparams(("arbitrary", "arbitrary")),
        name="post_mix",
    )(x, *a_list, *w_list, g, wq, mk, mv, wo, gffn, wr_hi, wr_lo, br, tri)


def _route(tf, wr_hi_ref, wr_lo_ref, br_ref):
    t_hi = tf.astype(BF16)
    t_lo = (tf - t_hi.astype(F32)).astype(BF16)
    w_hi, w_lo = wr_hi_ref[...], wr_lo_ref[...]
    nt = (((1,), (1,)), ((), ()))
    logits = (lax.dot_general(w_hi, t_hi, nt, preferred_element_type=F32)
              + lax.dot_general(w_hi, t_lo, nt, preferred_element_type=F32)
              + lax.dot_general(w_lo, t_hi, nt, preferred_element_type=F32)) + br_ref[...]
    row = lax.broadcasted_iota(jnp.int32, logits.shape, 0)
    big = jnp.int32(ROUTE_ROWS)
    is_g = row < N_GROUPS
    lg = jnp.where(is_g, logits, NEG_INF)
    gmax = jnp.max(lg, axis=0, keepdims=True)
    g_p = 1.0 / jnp.sum(jnp.where(is_g, jnp.exp(lg - gmax), 0.0), axis=0, keepdims=True)
    g_i = jnp.min(jnp.where(lg == gmax, row, big), axis=0, keepdims=True)
    lo = ROUTE_EXPERT_ROW0 + EXPERTS_PER_GROUP * g_i
    in_grp = (row >= lo) & (row < lo + EXPERTS_PER_GROUP)
    el = jnp.where(in_grp, logits, NEG_INF)
    v1 = jnp.max(el, axis=0, keepdims=True)
    i1 = jnp.min(jnp.where(el == v1, row, big), axis=0, keepdims=True)
    el2 = jnp.where(row == i1, NEG_INF, el)
    v2 = jnp.max(el2, axis=0, keepdims=True)
    i2 = jnp.min(jnp.where(el2 == v2, row, big), axis=0, keepdims=True)
    e2 = jnp.exp(v2 - v1)
    w1 = g_p / (1.0 + e2)
    w2 = g_p * e2 / (1.0 + e2)
    first_low = i1 < i2
    e_lo = jnp.minimum(i1, i2) - lo
    e_hi = jnp.maximum(i1, i2) - lo
    pair = jnp.where(e_lo == 0, 0, jnp.where(e_lo == 1, 3, 5)) + e_hi - e_lo - 1
    bucket = g_i * PAIRS_PER_GROUP + pair
    return bucket, jnp.where(first_low, w1, w2), jnp.where(first_low, w2, w1)


def _moe_kernel(ea_ref, eb_ref, valid_ref, src_hbm, x_hbm, gate_ref, g_ref, wgu_a_ref, wgu_b_ref,
                wd_a_ref, wd_b_ref, gf_ref, out_hbm, idx_ref, xb0, xb1, ob0, ob1, isem, gsem, ssem, *, n_tiles,
                final_norm):
    n_grp, sub, d = xb0.shape
    tms = n_grp * sub
    i = pl.program_id(0)

    def idx_off(k):
        return lax.rem(k, MOE_IDX_SLOTS) * tms

    def idx_copy(k):
        return pltpu.make_async_copy(src_hbm.at[k], idx_ref.at[pl.ds(idx_off(k), tms)],
                                     isem.at[lax.rem(k, MOE_IDX_SLOTS)])

    def gather_row(tok, xb, s, j, u):
        return pltpu.make_async_copy(x_hbm.at[pl.ds(tok, 1)], xb.at[j, pl.ds(u, 1)], gsem.at[s])

    def scatter_row(tok, ob, s, j, u):
        return pltpu.make_async_copy(ob.at[j, pl.ds(u, 1)], out_hbm.at[pl.ds(tok, 1)], ssem.at[s])

    def start_gather(k, xb, s):
        base = idx_off(k)
        for r in range(tms):
            gather_row(idx_ref[base + r], xb, s, r // sub, r % sub).start(priority=r % 2)

    def wait_gather(xb, s):
        for r in range(tms):
            gather_row(0, xb, s, r // sub, r % sub).wait()

    def for_rows(n, fn):
        @pl.when(n == tms)
        def _():
            for r in range(tms):
                fn(r // sub, r % sub, r)

        @pl.when(n < tms)
        def _():
            full = lax.shift_right_logical(n, 3)

            def grp(j, carry):
                for u in range(sub):
                    fn(j, u, j * sub + u)
                return carry
            lax.fori_loop(0, full, grp, 0)

            def one(u, carry):
                fn(full, u, full * sub + u)
                return carry
            lax.fori_loop(0, lax.bitwise_and(n, sub - 1), one, 0)

    def start_scatter(k, n, ob, s):
        base = idx_off(k)

        def fn(j, u, r):
            prio = r % 2 if isinstance(r, int) else 0
            scatter_row(idx_ref[base + r], ob, s, j, u).start(priority=prio)
        for_rows(n, fn)

    def wait_scatter(n, ob, s):
        for_rows(n, lambda j, u, r: scatter_row(0, ob, s, j, u).wait())

    def tile(s):
        xb, ob = (xb0, ob0) if s == 0 else (xb1, ob1)
        xo, obo = (xb1, ob1) if s == 0 else (xb0, ob0)
        if s == 0:
            @pl.when(i == 0)
            def _():
                idx_copy(0).start()
                idx_copy(0).wait()
                start_gather(0, xb0, 0)
                idx_copy(1).start()

        @pl.when(i + 2 <= n_tiles)
        def _():
            idx_copy(i + 2).start()

        wait_gather(xb, s)

        @pl.when(i >= 2)
        def _():
            wait_scatter(valid_ref[jnp.maximum(i - 2, 0)], ob, s)

        idx_copy(i + 1).wait()
        n_valid = valid_ref[i]
        next_base = idx_off(i + 1)
        chunk = tms // MOE_GATHER_CHUNKS

        def prefetch_rows(c):
            for r in range(c * chunk, (c + 1) * chunk):
                gather_row(idx_ref[next_base + r], xo, 1 - s, r // sub, r % sub).start(priority=r % 2)

        @pl.when(n_valid > 0)
        def _():
            x = xb[...].reshape(tms, d)
            prefetch_rows(0)
            t = _rms(x, g_ref[...]).astype(BF16)
            gates = gate_ref[...]
            prefetch_rows(1)
            y = x
            for e, (wgu_ref, wd_ref) in enumerate(((wgu_a_ref, wd_a_ref), (wgu_b_ref, wd_b_ref))):
                hgu = jnp.dot(t, wgu_ref[0], preferred_element_type=F32)
                prefetch_rows(2 + 3 * e)
                hg, hu = hgu[:, 0:D_EXPERT], hgu[:, D_EXPERT:2 * D_EXPERT]
                act = (hg * jax.nn.sigmoid(hg) * hu * gates[:, e:e + 1]).astype(BF16)
                prefetch_rows(3 + 3 * e)
                y = y + jnp.dot(act, wd_ref[0], preferred_element_type=F32)
                prefetch_rows(4 + 3 * e)
            if final_norm:
                y = _rms(y, gf_ref[...])
            ob[...] = y.reshape(n_grp, sub, d)

        @pl.when(n_valid <= 0)
        def _():
            for c in range(MOE_GATHER_CHUNKS):
                prefetch_rows(c)

        start_scatter(i, n_valid, ob, s)

        @pl.when(i == n_tiles - 1)
        def _():
            wait_gather(xo, 1 - s)
            if n_tiles > 1:
                wait_scatter(valid_ref[jnp.maximum(i - 1, 0)], obo, 1 - s)
            wait_scatter(n_valid, ob, s)

    parity = lax.rem(i, 2)
    pl.when(parity == 0)(lambda: tile(0))
    pl.when(parity == 1)(lambda: tile(1))


def _moe_plan(route, counts, n_tok):
    tms = MOE_TILE
    n_b = N_GROUPS * PAIRS_PER_GROUP
    n_tiles = n_tok // tms + n_b
    bucket = route[0].astype(jnp.int32)
    rank = route[1].astype(jnp.int32)
    cnt = counts[:n_b, 0].astype(jnp.int32)
    padded = (cnt + tms - 1) // tms * tms
    pad_end = jnp.cumsum(padded)
    pad_off = pad_end - padded
    dest = pad_off[bucket] + rank
    payload = jnp.stack([jnp.arange(n_tok, dtype=F32), route[2], route[3]], axis=1)
    plan = jnp.zeros(((n_tiles + 1) * tms, 3), F32).at[dest].set(payload, unique_indices=True)
    src = plan[:, 0].astype(jnp.int32).reshape(n_tiles + 1, tms)
    gates = plan[:n_tiles * tms, 1:3]
    start = jnp.arange(n_tiles, dtype=jnp.int32) * tms
    tb = jnp.minimum(jnp.sum(pad_end[None, :] <= start[:, None], axis=1), n_b - 1).astype(jnp.int32)
    valid = jnp.clip(cnt[tb] - (start - pad_off[tb]), 0, tms).astype(jnp.int32)
    lo_hi = [(a, b) for a in range(EXPERTS_PER_GROUP) for b in range(a + 1, EXPERTS_PER_GROUP)]
    ea_tab = np.array([EXPERTS_PER_GROUP * g + a for g in range(N_GROUPS) for a, _ in lo_hi], np.int32)
    eb_tab = np.array([EXPERTS_PER_GROUP * g + b for g in range(N_GROUPS) for _, b in lo_hi], np.int32)
    return src, gates, jnp.asarray(ea_tab)[tb], jnp.asarray(eb_tab)[tb], valid


def _moe(x2d, route, counts, g, wgu, wd, gf, final_norm):
    n_tok, d = x2d.shape
    tms = MOE_TILE
    src, gates, ea, eb, valid = _moe_plan(route, counts, n_tok)
    n_tiles = src.shape[0] - 1
    any_spec = pl.BlockSpec(memory_space=pl.ANY)
    grid_spec = pltpu.PrefetchScalarGridSpec(
        num_scalar_prefetch=3,
        grid=(n_tiles,),
        in_specs=[any_spec, any_spec,
                  pl.BlockSpec((tms, 2), lambda i, ea_r, eb_r, v_r: (i, 0)),
                  pl.BlockSpec(g.shape, lambda i, ea_r, eb_r, v_r: (0, 0)),
                  pl.BlockSpec((1, d, 2 * D_EXPERT), lambda i, ea_r, eb_r, v_r: (ea_r[i], 0, 0)),
                  pl.BlockSpec((1, d, 2 * D_EXPERT), lambda i, ea_r, eb_r, v_r: (eb_r[i], 0, 0)),
                  pl.BlockSpec((1, D_EXPERT, d), lambda i, ea_r, eb_r, v_r: (ea_r[i], 0, 0)),
                  pl.BlockSpec((1, D_EXPERT, d), lambda i, ea_r, eb_r, v_r: (eb_r[i], 0, 0)),
                  pl.BlockSpec(gf.shape, lambda i, ea_r, eb_r, v_r: (0, 0))],
        out_specs=any_spec,
        scratch_shapes=[pltpu.SMEM((MOE_IDX_SLOTS * tms,), jnp.int32)]
        + [pltpu.VMEM((tms // SUBLANES, SUBLANES, d), F32)] * 4
        + [pltpu.SemaphoreType.DMA((MOE_IDX_SLOTS,)), pltpu.SemaphoreType.DMA((2,)),
           pltpu.SemaphoreType.DMA((2,))],
    )
    return pl.pallas_call(
        functools.partial(_moe_kernel, n_tiles=n_tiles, final_norm=final_norm),
        out_shape=jax.ShapeDtypeStruct((n_tok, d), F32),
        grid_spec=grid_spec,
        compiler_params=_cparams(("arbitrary",)),
        name="moe",
    )(ea, eb, valid, src, x2d, gates, g, wgu, wgu, wd, wd, gf)


def _row(v):
    return v.reshape(1, -1).astype(F32)


def _rope_tables(seq_len, rot_dim, lane_starts):
    inv = ROPE_THETA ** (-jnp.arange(0, rot_dim, 2, dtype=F32) / rot_dim)
    ang = jnp.arange(seq_len, dtype=F32)[:, None] * inv[None, :]
    cos, sin = jnp.cos(ang), jnp.sin(ang)
    ct = jnp.ones((seq_len, LANES), F32)
    st = jnp.zeros((seq_len, LANES), F32)
    for l0 in lane_starts:
        ct = ct.at[:, l0:l0 + rot_dim].set(jnp.concatenate([cos, cos], axis=1))
        st = st.at[:, l0:l0 + rot_dim].set(jnp.concatenate([-sin, sin], axis=1))
    return ct, st


def _prep_even(w_in, w_uq, w_ukv, w_o):
    c = np.cumsum([0, MLA_Q_RANK, MLA_KV_RANK, MLA_ROPE, DIFF_QK, DIFF_QK, DIFF_VW])
    kpe = jnp.zeros((D_MODEL, LANES), F32).at[:, MLA_NOPE:MLA_NOPE + MLA_ROPE].set(w_in[:, c[2]:c[3]])
    win = jnp.concatenate([w_in[:, c[0]:c[2]], w_in[:, c[3]:c[6]], kpe], axis=1).astype(BF16)
    uq = w_uq.reshape(MLA_Q_RANK, MLA_HEADS, MLA_NOPE + MLA_ROPE)
    uq = jnp.pad(uq, ((0, 0), (0, 0), (0, LANES - MLA_NOPE - MLA_ROPE))).reshape(MLA_Q_RANK, MLA_HEADS * LANES)
    ukv = w_ukv.reshape(MLA_KV_RANK, MLA_HEADS, MLA_NOPE + MLA_V)
    uk = jnp.pad(ukv[:, :, :MLA_NOPE], ((0, 0), (0, 0), (0, LANES - MLA_NOPE))).reshape(MLA_KV_RANK,
                                                                                       MLA_HEADS * LANES)
    uv = ukv[:, :, MLA_NOPE:].reshape(MLA_KV_RANK, MLA_HEADS * MLA_V)
    n_mla = MLA_HEADS * MLA_V
    return (win, uq.astype(BF16), uk.astype(BF16), uv.astype(BF16),
            w_o[:n_mla].astype(BF16), w_o[n_mla:].astype(BF16))


def _prep_moe(w_rg, b_rg, w_re, b_re, w_gate, w_up, w_down):
    e0 = ROUTE_EXPERT_ROW0
    wr = jnp.zeros((ROUTE_ROWS, D_MODEL), F32).at[:N_GROUPS].set(w_rg.T).at[e0:e0 + N_EXPERTS].set(w_re.T)
    br = jnp.zeros((ROUTE_ROWS, 1), F32).at[:N_GROUPS, 0].set(b_rg).at[e0:e0 + N_EXPERTS, 0].set(b_re)
    wr_hi = wr.astype(BF16)
    wr_lo = (wr - wr_hi.astype(F32)).astype(BF16)
    wgu = jnp.concatenate([w_gate, w_up], axis=-1).reshape(N_EXPERTS, D_MODEL, 2 * D_EXPERT).astype(BF16)
    wd = w_down.reshape(N_EXPERTS, D_EXPERT, D_MODEL).astype(BF16)
    return wr_hi, wr_lo, br, wgu, wd


def _trunk(x, mem, p, tables):
    b, s, d = x.shape
    depth = p['ln_mix'].shape[0]
    for layer in range(depth):
        i = layer // 2
        if layer % 2 == 0:
            win, uq, uk, uv, wo_a, wo_b = _prep_even(p['w_in_even'][i], p['mla_w_uq'][i], p['mla_w_ukv'][i],
                                                     p['w_o_even'][i])
            cm, sm, cd, sd = tables[s]
            q, k, v, dq, dk, dv = _even_in(x, _row(p['ln_mix'][layer]), win, _row(p['mla_q_norm'][i]),
                                           _row(p['mla_kv_norm'][i]), uq, uk, uv, cm, sm, cd, sd)
            o_mla = _flash(q, k, v, 2 * LANES)
            lamp = jnp.zeros((8, LANES), F32)
            for r, name in enumerate(('diff_lambda_q1', 'diff_lambda_k1', 'diff_lambda_q2', 'diff_lambda_k2')):
                lamp = lamp.at[r, :DIFF_HEAD_DIM].set(p[name][i])
            lam_init = 0.8 - 0.6 * math.exp(-0.3 * layer)
            o_diff = _flash(dq, dk, dv, LANES, lamp=lamp, subln=_row(p['diff_subln'][i]), lam_init=lam_init)
            a_list, w_list = [o_mla, o_diff], [wo_a, wo_b]
        else:
            q, k, v = _odd_in(x, _row(p['ln_mix'][layer]), p['w_in_odd'][i].astype(BF16))
            tbl, idx = _na_tables(p['na_rpb'][i])
            a_list, w_list = [_na_attention(q, k, v, tbl, idx)], [p['w_o_odd'][i].astype(BF16)]
        mk, mv = _mem_kv(mem, _row(p['ln_mem'][layer]), p['xa_w_kv'][layer].astype(BF16))
        wr_hi, wr_lo, br, wgu, wd = _prep_moe(p['moe_w_group'][layer], p['moe_b_group'][layer],
                                              p['moe_w_expert'][layer], p['moe_b_expert'][layer],
                                              p['moe_w_gate'][layer], p['moe_w_up'][layer], p['moe_w_down'][layer])
        x, route, counts = _post_mix(x, a_list, w_list, _row(p['ln_xattn'][layer]),
                                     p['xa_w_q'][layer].astype(BF16), mk, mv, p['xa_w_o'][layer].astype(BF16),
                                     _row(p['ln_ffn'][layer]), wr_hi, wr_lo, br)
        x = _moe(x.reshape(b * s, d), route, counts, _row(p['ln_ffn'][layer]), wgu, wd,
                 _row(p['ln_final']), final_norm=(layer == depth - 1)).reshape(b, s, d)
    return x


def kernel(x_prompt, x_sample, mem_prompt, mem_sample, ln_mix, w_in_even, mla_q_norm, mla_kv_norm, mla_w_uq, mla_w_ukv, diff_lambda_q1, diff_lambda_k1, diff_lambda_q2, diff_lambda_k2, diff_subln, w_o_even, w_in_odd, na_rpb, w_o_odd, ln_xattn, ln_mem, xa_w_q, xa_w_kv, xa_w_o, ln_ffn, moe_w_group, moe_b_group, moe_w_expert, moe_b_expert, moe_w_gate, moe_w_up, moe_w_down, ln_final):
    p = dict(ln_mix=ln_mix, w_in_even=w_in_even, mla_q_norm=mla_q_norm, mla_kv_norm=mla_kv_norm,
             mla_w_uq=mla_w_uq, mla_w_ukv=mla_w_ukv, diff_lambda_q1=diff_lambda_q1,
             diff_lambda_k1=diff_lambda_k1, diff_lambda_q2=diff_lambda_q2, diff_lambda_k2=diff_lambda_k2,
             diff_subln=diff_subln, w_o_even=w_o_even, w_in_odd=w_in_odd, na_rpb=na_rpb, w_o_odd=w_o_odd,
             ln_xattn=ln_xattn, ln_mem=ln_mem, xa_w_q=xa_w_q, xa_w_kv=xa_w_kv, xa_w_o=xa_w_o,
             ln_ffn=ln_ffn, moe_w_group=moe_w_group, moe_b_group=moe_b_group, moe_w_expert=moe_w_expert,
             moe_b_expert=moe_b_expert, moe_w_gate=moe_w_gate, moe_w_up=moe_w_up, moe_w_down=moe_w_down,
             ln_final=ln_final)
    tables = {}
    for s in {x_prompt.shape[1], x_sample.shape[1]}:
        cm, sm = _rope_tables(s, MLA_ROPE, (MLA_NOPE,))
        cd, sd = _rope_tables(s, DIFF_ROT, (0, DIFF_HEAD_DIM))
        tables[s] = (cm, sm, cd, sd)
    return (_trunk(x_prompt, mem_prompt, p, tables), _trunk(x_sample, mem_sample, p, tables))
```

```python
import functools
import math

import numpy as np
import jax
import jax.numpy as jnp
from jax import lax
from jax.experimental import pallas as pl
from jax.experimental.pallas import tpu as pltpu

F32 = jnp.float32
BF16 = jnp.bfloat16

D_MODEL = 1024
ROPE_THETA = 500000.0
NORM_EPS = 1e-6
NEG_INF = -1e30
LOG2E = math.log2(math.e)

MLA_HEADS = 8
MLA_Q_RANK = 384
MLA_KV_RANK = 256
MLA_NOPE = 64
MLA_ROPE = 32
MLA_V = 64
DIFF_HEADS = 4
DIFF_HEAD_DIM = 64
DIFF_ROT = DIFF_HEAD_DIM // 4
DIFF_QK = DIFF_HEADS * 2 * DIFF_HEAD_DIM
DIFF_VW = DIFF_HEADS * 2 * DIFF_HEAD_DIM
GRID_W = 64
NA_HEADS = 16
NA_HEAD_DIM = 64
WIN_ROWS = 8
WIN_COLS = 16
MEM_TOKENS = 256
XA_HEADS = 4
XA_HEAD_DIM = 128
XA_INNER = XA_HEADS * XA_HEAD_DIM
N_GROUPS = 4
EXPERTS_PER_GROUP = 4
N_EXPERTS = N_GROUPS * EXPERTS_PER_GROUP
PAIRS_PER_GROUP = EXPERTS_PER_GROUP * (EXPERTS_PER_GROUP - 1) // 2
ROUTE_ROWS = 32
ROUTE_EXPERT_ROW0 = 8
D_EXPERT = 256

LANES = 128
SUBLANES = 8
VMEM_LIMIT = 56 * 1024 * 1024

TOKEN_TILE = 512
MOE_TILE = 256
MOE_GATHER_CHUNKS = 8
MOE_IDX_SLOTS = 4
FLASH_TQ = 1024
FLASH_TK = 512
FLASH_BOUND_SLACK = 1.01
FLASH_MIN_ROW_SUM = 2.0 ** -60
FLASH_UNROLL = 8
NA_ROW_BLOCK = 16
NA_KEY_ROWS = NA_ROW_BLOCK + WIN_ROWS
NA_GROUP_ROWS = 4
NA_GROUP_KEY_ROWS = NA_GROUP_ROWS + WIN_ROWS
NA_TILE_CODES = 3
NA_TILES_PER_HEAD = NA_TILE_CODES * 16 + 1


def _cparams(sem):
    return pltpu.CompilerParams(dimension_semantics=sem, vmem_limit_bytes=VMEM_LIMIT)


def _rms(xf, g):
    return xf * lax.rsqrt(jnp.mean(xf * xf, axis=-1, keepdims=True) + NORM_EPS) * g


def _full(shape):
    nd = len(shape)
    return pl.BlockSpec(shape, lambda *_: (0,) * nd)


def _rope_block(x, cos, sin, half):
    lane = lax.broadcasted_iota(jnp.int32, x.shape, 1)
    up = pltpu.roll(x, LANES - half, 1)
    down = pltpu.roll(x, half, 1)
    first = lax.bitwise_and(lane, 2 * half - 1) < half
    return x * cos + jnp.where(first, up, down) * sin


def _even_in_kernel(x_ref, g_ref, win_ref, qn_ref, kvn_ref, wuq_ref, wuk_ref, wuv_ref,
                    cm_ref, sm_ref, cd_ref, sd_ref,
                    q_ref, k_ref, v_ref, dq_ref, dk_ref, dv_ref):
    h = _rms(x_ref[0], g_ref[...]).astype(BF16)
    proj = jnp.dot(h, win_ref[...], preferred_element_type=F32)
    o_cq, o_ckv, o_dq, o_dk, o_dv, o_kpe = 0, 384, 640, 1152, 1664, 2176
    cm, sm, cd, sd = cm_ref[...], sm_ref[...], cd_ref[...], sd_ref[...]

    cq = _rms(proj[:, o_cq:o_cq + MLA_Q_RANK], qn_ref[...]).astype(BF16)
    q = jnp.dot(cq, wuq_ref[...], preferred_element_type=F32)
    mla_c = (MLA_NOPE + MLA_ROPE) ** -0.5 * LOG2E
    for hd in range(MLA_HEADS):
        sl = slice(hd * LANES, (hd + 1) * LANES)
        q_ref[0, :, sl] = (_rope_block(q[:, sl], cm, sm, MLA_ROPE // 2) * mla_c).astype(BF16)

    ckv = _rms(proj[:, o_ckv:o_ckv + MLA_KV_RANK], kvn_ref[...]).astype(BF16)
    kn = jnp.dot(ckv, wuk_ref[...], preferred_element_type=F32)
    kpe = _rope_block(proj[:, o_kpe:o_kpe + LANES], cm, sm, MLA_ROPE // 2)
    for hd in range(MLA_HEADS):
        sl = slice(hd * LANES, (hd + 1) * LANES)
        k_ref[0, :, sl] = (kn[:, sl] + kpe).astype(BF16)
    v_ref[0] = jnp.dot(ckv, wuv_ref[...], preferred_element_type=F32).astype(BF16)

    diff_c = DIFF_HEAD_DIM ** -0.5 * LOG2E
    for hd in range(DIFF_HEADS):
        sl = slice(hd * LANES, (hd + 1) * LANES)
        dq_ref[0, :, sl] = (_rope_block(proj[:, o_dq + hd * LANES:o_dq + (hd + 1) * LANES], cd, sd,
                                        DIFF_ROT // 2) * diff_c).astype(BF16)
        dk_ref[0, :, sl] = _rope_block(proj[:, o_dk + hd * LANES:o_dk + (hd + 1) * LANES], cd, sd,
                                       DIFF_ROT // 2).astype(BF16)
    dv_ref[0] = proj[:, o_dv:o_dv + DIFF_VW].astype(BF16)


def _even_in(x, g, win, qn, kvn, wuq, wuk, wuv, cm, sm, cd, sd):
    b, s, d = x.shape
    tm = TOKEN_TILE
    tok = lambda w: pl.BlockSpec((1, tm, w), lambda i, j: (i, j, 0))
    tab = pl.BlockSpec((tm, LANES), lambda i, j: (j, 0))
    outs = [jax.ShapeDtypeStruct((b, s, w), BF16) for w in (1024, 1024, 512, 512, 512, 512)]
    return pl.pallas_call(
        _even_in_kernel,
        out_shape=outs,
        grid=(b, s // tm),
        in_specs=[tok(d), _full(g.shape), _full(win.shape), _full(qn.shape), _full(kvn.shape),
                  _full(wuq.shape), _full(wuk.shape), _full(wuv.shape), tab, tab, tab, tab],
        out_specs=[tok(1024), tok(1024), tok(512), tok(512), tok(512), tok(512)],
        compiler_params=_cparams(("parallel", "parallel")),
        name="even_in",
    )(x, g, win, qn, kvn, wuq, wuk, wuv, cm, sm, cd, sd)


def _flash_kernel(*refs, tq, tk, n_k, width, diff, lam_init):
    if diff:
        q_ref, k_ref, v_ref, lamp_ref, g_ref, o_ref, qs_ref, m_ref, l_ref, acc_ref, kn_ref, kmax_ref = refs
    else:
        q_ref, k_ref, v_ref, o_ref, qs_ref, m_ref, l_ref, acc_ref, kn_ref, kmax_ref = refs
    half = width // 2
    n_j = tk // LANES
    q = q_ref[0]
    lane = lax.broadcasted_iota(jnp.int32, (tq, width), 1)
    zero = jnp.zeros_like(q)
    qs_ref[0:tq, :] = jnp.where(lane < half, q, zero)
    qs_ref[tq:2 * tq, :] = jnp.where(lane >= half, q, zero)

    @pl.when(pl.program_id(2) == 0)
    def _():
        r_i = lax.broadcasted_iota(jnp.int32, (width, LANES), 0)
        c_i = lax.broadcasted_iota(jnp.int32, (width, LANES), 1)
        sel = jnp.where(((c_i == 0) & (r_i < half)) | ((c_i == 1) & (r_i >= half)), 1.0, 0.0).astype(BF16)
        kn_ref[...] = jnp.zeros(kn_ref.shape, F32)

        def knorm(i, carry):
            kc = k_ref[0, pl.ds(pl.multiple_of(i * tk, tk), tk), :].astype(F32)
            kn_ref[...] = jnp.maximum(kn_ref[...], jnp.dot((kc * kc).astype(BF16), sel, preferred_element_type=F32))
            return carry

        lax.fori_loop(0, n_k, knorm, 0)
        kmax = jnp.sqrt(jnp.max(kn_ref[...], axis=0, keepdims=True))
        kmax_ref[0:1, :] = jnp.broadcast_to(kmax[:, 0:1], (1, LANES))
        kmax_ref[1:2, :] = jnp.broadcast_to(kmax[:, 1:2], (1, LANES))

    def reset(m):
        m_ref[...] = m
        l_ref[...] = jnp.zeros(l_ref.shape, F32)
        acc_ref[...] = jnp.zeros(acc_ref.shape, F32)

    def scores(i):
        off = pl.multiple_of(i * tk, tk)
        kc = k_ref[0, pl.ds(off, tk), :]
        return off, lax.dot_general(qs_ref[...], kc, (((1,), (1,)), ((), ())), preferred_element_type=F32)

    def max_pass(i, carry):
        _, s = scores(i)
        m = m_ref[...]
        for j in range(n_j):
            m = jnp.maximum(m, s[:, j * LANES:(j + 1) * LANES])
        m_ref[...] = m
        return carry

    def pv_pass(i, carry):
        off, s = scores(i)
        vc = v_ref[0, pl.ds(off, tk), :]
        m = m_ref[...]
        ps = [jnp.exp2(s[:, j * LANES:(j + 1) * LANES] - m) for j in range(n_j)]
        lsum = ps[0]
        for pj in ps[1:]:
            lsum = lsum + pj
        l_ref[...] += lsum
        p = jnp.concatenate([pj.astype(BF16) for pj in ps], axis=1)
        acc_ref[...] += jnp.dot(p, vc, preferred_element_type=F32)
        return carry

    qf = qs_ref[...].astype(F32)
    qn = jnp.sqrt(jnp.sum(qf * qf, axis=1, keepdims=True))
    bound_top = qn[0:tq] * kmax_ref[0:1, :] * FLASH_BOUND_SLACK
    bound_bot = qn[tq:2 * tq] * kmax_ref[1:2, :] * FLASH_BOUND_SLACK
    reset(jnp.concatenate([bound_top, bound_bot], axis=0))
    lax.fori_loop(0, n_k, pv_pass, 0, unroll=FLASH_UNROLL)
    l_min = jnp.min(jnp.sum(l_ref[...], axis=1, keepdims=True))

    @pl.when(jnp.logical_not(l_min >= FLASH_MIN_ROW_SUM))
    def _():
        reset(jnp.full(m_ref.shape, NEG_INF, F32))
        lax.fori_loop(0, n_k, max_pass, 0, unroll=FLASH_UNROLL)
        m_ref[...] = jnp.broadcast_to(jnp.max(m_ref[...], axis=1, keepdims=True), m_ref.shape)
        lax.fori_loop(0, n_k, pv_pass, 0, unroll=FLASH_UNROLL)

    o = acc_ref[...] / jnp.sum(l_ref[...], axis=1, keepdims=True)
    top, bot = o[0:tq], o[tq:2 * tq]
    if diff:
        lp = lamp_ref[...]
        lam = (jnp.exp(jnp.sum(lp[0:1] * lp[1:2], axis=1, keepdims=True))
               - jnp.exp(jnp.sum(lp[2:3] * lp[3:4], axis=1, keepdims=True)) + lam_init)
        od = top - lam * bot
        o_ref[0] = (_rms(od, g_ref[...]) * (1.0 - lam_init)).astype(BF16)
    else:
        lane_o = lax.broadcasted_iota(jnp.int32, (tq, LANES), 1)
        o_ref[0] = jnp.where(lane_o < LANES // 2, top, bot).astype(BF16)


def _flash(q, k, v, width, lamp=None, subln=None, lam_init=0.0):
    b, s, _ = q.shape
    n = q.shape[2] // width
    tq, tk = FLASH_TQ, FLASH_TK
    diff = lamp is not None
    kern = functools.partial(_flash_kernel, tq=tq, tk=tk, n_k=s // tk, width=width, diff=diff,
                             lam_init=lam_init)
    in_specs = [pl.BlockSpec((1, tq, width), lambda bi, p, qi: (bi, qi, p)),
                pl.BlockSpec((1, s, width), lambda bi, p, qi: (bi, 0, p)),
                pl.BlockSpec((1, s, LANES), lambda bi, p, qi: (bi, 0, p))]
    args = [q, k, v]
    if diff:
        in_specs += [_full(lamp.shape), _full(subln.shape)]
        args += [lamp, subln]
    return pl.pallas_call(
        kern,
        out_shape=jax.ShapeDtypeStruct((b, s, n * LANES), BF16),
        grid=(b, n, s // tq),
        in_specs=in_specs,
        out_specs=pl.BlockSpec((1, tq, LANES), lambda bi, p, qi: (bi, qi, p)),
        scratch_shapes=[pltpu.VMEM((2 * tq, width), BF16), pltpu.VMEM((2 * tq, LANES), F32),
                        pltpu.VMEM((2 * tq, LANES), F32), pltpu.VMEM((2 * tq, LANES), F32),
                        pltpu.VMEM((tk, LANES), F32), pltpu.VMEM((SUBLANES, LANES), F32)],
        compiler_params=_cparams(("parallel", "parallel", "arbitrary")),
        name="flash_diff" if diff else "flash_mla",
    )(*args)


def _odd_in_kernel(x_ref, g_ref, w_ref, q_ref, k_ref, v_ref):
    h = _rms(x_ref[0], g_ref[...]).astype(BF16)
    qkv = jnp.dot(h, w_ref[...], preferred_element_type=F32)
    c = NA_HEAD_DIM ** -0.5 * LOG2E
    q_ref[0] = (qkv[:, 0:1024] * c).astype(BF16)
    k_ref[0] = qkv[:, 1024:2048].astype(BF16)
    v_ref[0] = qkv[:, 2048:3072].astype(BF16)


def _odd_in(x, g, w):
    b, s, d = x.shape
    tm = TOKEN_TILE
    tok = pl.BlockSpec((1, tm, d), lambda i, j: (i, j, 0))
    return pl.pallas_call(
        _odd_in_kernel,
        out_shape=[jax.ShapeDtypeStruct((b, s, d), BF16)] * 3,
        grid=(b, s // tm),
        in_specs=[tok, _full(g.shape), _full(w.shape)],
        out_specs=[tok, tok, tok],
        compiler_params=_cparams(("parallel", "parallel")),
        name="odd_in",
    )(x, g, w)


def _na_kernel(idx_ref, q_ref, k_ref, v_ref, tbl_ref, o_ref, qs_ref, os_ref, *, rows, n_rb):
    nq = NA_ROW_BLOCK * GRID_W
    nk = NA_KEY_ROWS * GRID_W
    gq = NA_GROUP_ROWS * GRID_W
    n_kp = NA_KEY_ROWS // 2
    rb = pl.program_id(2)
    ks = jnp.clip(NA_ROW_BLOCK * rb - WIN_ROWS // 2, 0, rows - NA_KEY_ROWS)
    pat = jnp.where(rb == 0, 0, jnp.where(rb == n_rb - 1, 2, 1))
    off = pl.multiple_of(ks * GRID_W, (WIN_ROWS // 2) * GRID_W)
    q = q_ref[0]
    lane = lax.broadcasted_iota(jnp.int32, (nq, LANES), 1)
    zero = jnp.zeros_like(q)
    qs_ref[0:nq, :] = jnp.where(lane < LANES // 2, q, zero)
    qs_ref[nq:2 * nq, :] = jnp.where(lane >= LANES // 2, q, zero)

    def group(g, carry):
        r0 = pl.multiple_of(g * gq, gq)
        per_head = NA_ROW_BLOCK // NA_GROUP_ROWS
        hh = g // per_head
        i0 = (g % per_head) * NA_GROUP_ROWS
        w0 = jnp.clip(i0 + (pat - 1) * (WIN_ROWS // 2), 0, NA_KEY_ROWS - NA_GROUP_KEY_ROWS)
        goff = pl.multiple_of(off + w0 * GRID_W, (WIN_ROWS // 2) * GRID_W)
        kw = k_ref[0, pl.ds(goff, NA_GROUP_KEY_ROWS * GRID_W), :]
        vw = v_ref[0, pl.ds(goff, NA_GROUP_KEY_ROWS * GRID_W), :]
        s = lax.dot_general(qs_ref[pl.ds(r0, gq), :], kw, (((1,), (1,)), ((), ())), preferred_element_type=F32)
        bias_rows = []
        for ii in range(NA_GROUP_ROWS):
            base = pat * (NA_ROW_BLOCK * n_kp) + (i0 + ii) * n_kp + lax.shift_right_logical(w0, 1)
            bias_rows.append(jnp.concatenate(
                [tbl_ref[0, hh * NA_TILES_PER_HEAD + idx_ref[base + kp]]
                 for kp in range(NA_GROUP_KEY_ROWS // 2)], axis=1))
        s = s + jnp.concatenate(bias_rows, axis=0)
        m = jnp.max(s, axis=1, keepdims=True)
        p = jnp.exp2(s - m)
        linv = 1.0 / jnp.sum(p, axis=1, keepdims=True)
        os_ref[pl.ds(r0, gq), :] = jnp.dot(p.astype(BF16), vw, preferred_element_type=F32) * linv
        return carry

    lax.fori_loop(0, 2 * nq // gq, group, 0, unroll=True)
    o_ref[0] = jnp.where(lane < LANES // 2, os_ref[0:nq, :], os_ref[nq:2 * nq, :]).astype(BF16)


def _na_tables(rpb):
    col = np.arange(GRID_W)
    col_start = np.clip(col - WIN_COLS // 2, 0, GRID_W - WIN_COLS)
    col_valid = (col[None, :] >= col_start[:, None]) & (col[None, :] < col_start[:, None] + WIN_COLS)
    col_idx = np.clip(col[None, :] - col[:, None], -(WIN_COLS - 1), WIN_COLS - 1) + WIN_COLS - 1
    n_ri = 2 * WIN_ROWS - 1
    ri = np.zeros((NA_TILES_PER_HEAD, LANES), np.int32)
    ok = np.zeros((NA_TILES_PER_HEAD, LANES), bool)
    for code in range(NA_TILE_CODES):
        for a in range(-1, n_ri):
            t = code * 16 + a + 1
            ri[t, :GRID_W] = np.clip(a, 0, n_ri - 1)
            ri[t, GRID_W:] = np.clip(a + 1, 0, n_ri - 1)
            ok[t, :GRID_W] = code in (0, 1) and 0 <= a < n_ri
            ok[t, GRID_W:] = code in (0, 2) and 0 <= a + 1 < n_ri
    onehot = (col_idx.reshape(-1)[None, :] == np.arange(2 * WIN_COLS - 1)[:, None]).astype(np.float32)
    band = jnp.dot(rpb.reshape(NA_HEADS * n_ri, 2 * WIN_COLS - 1).astype(F32), jnp.asarray(onehot),
                   precision=lax.Precision.HIGHEST).reshape(NA_HEADS, n_ri, GRID_W, GRID_W)
    band = jnp.where(col_valid[None, None], band * LOG2E, NEG_INF)
    masked = jnp.full((NA_HEADS, GRID_W, GRID_W), NEG_INF, F32)
    tiles = []
    for t in range(NA_TILES_PER_HEAD):
        left = band[:, ri[t, 0]] if ok[t, 0] else masked
        right = band[:, ri[t, GRID_W]] if ok[t, GRID_W] else masked
        tiles.append(jnp.concatenate([left, right], axis=-1))
    tbl = jnp.stack(tiles, axis=1).reshape(NA_HEADS // 2, 2 * NA_TILES_PER_HEAD, GRID_W, LANES)

    idx = np.zeros((3, NA_ROW_BLOCK, NA_KEY_ROWS // 2), np.int32)
    for pat in range(3):
        for i in range(NA_ROW_BLOCK):
            rel = (i, i + WIN_ROWS // 2, i + WIN_ROWS)[pat]
            start = (max(i - WIN_ROWS // 2, 0), i, min(i + WIN_ROWS // 2, NA_KEY_ROWS - WIN_ROWS))[pat]
            for kp in range(NA_KEY_ROWS // 2):
                kk = 2 * kp
                a = kk - rel + WIN_ROWS - 1
                v0 = start <= kk < start + WIN_ROWS
                v1 = start <= kk + 1 < start + WIN_ROWS
                if v0:
                    assert 0 <= a < n_ri
                if v1:
                    assert 0 <= a + 1 < n_ri
                if v0 and v1:
                    idx[pat, i, kp] = a + 1
                elif v0:
                    idx[pat, i, kp] = 16 + a + 1
                elif v1:
                    idx[pat, i, kp] = 32 + a + 1
                else:
                    idx[pat, i, kp] = NA_TILES_PER_HEAD - 1
    return tbl, jnp.asarray(idx.reshape(-1))


def _na_attention(q, k, v, tbl, idx):
    b, s, d = q.shape
    rows = s // GRID_W
    n_rb = rows // NA_ROW_BLOCK
    nq = NA_ROW_BLOCK * GRID_W
    nk = NA_KEY_ROWS * GRID_W
    kern = functools.partial(_na_kernel, rows=rows, n_rb=n_rb)
    grid_spec = pltpu.PrefetchScalarGridSpec(
        num_scalar_prefetch=1,
        grid=(NA_HEADS // 2, b, n_rb),
        in_specs=[pl.BlockSpec((1, nq, LANES), lambda p, bi, r, idx_r: (bi, r, p)),
                  pl.BlockSpec((1, s, LANES), lambda p, bi, r, idx_r: (bi, 0, p)),
                  pl.BlockSpec((1, s, LANES), lambda p, bi, r, idx_r: (bi, 0, p)),
                  pl.BlockSpec((1, 2 * NA_TILES_PER_HEAD, GRID_W, LANES), lambda p, bi, r, idx_r: (p, 0, 0, 0))],
        out_specs=pl.BlockSpec((1, nq, LANES), lambda p, bi, r, idx_r: (bi, r, p)),
        scratch_shapes=[pltpu.VMEM((2 * nq, LANES), BF16), pltpu.VMEM((2 * nq, LANES), F32)],
    )
    return pl.pallas_call(
        kern,
        out_shape=jax.ShapeDtypeStruct((b, s, d), BF16),
        grid_spec=grid_spec,
        compiler_params=_cparams(("parallel", "parallel", "arbitrary")),
        name="na_attention",
    )(idx, q, k, v, tbl)


def _mem_kv_kernel(m_ref, g_ref, w_ref, k_ref, v_ref):
    h = _rms(m_ref[0], g_ref[...]).astype(BF16)
    kv = jnp.dot(h, w_ref[...], preferred_element_type=F32)
    k_ref[0] = kv[:, 0:XA_INNER].astype(BF16)
    v_ref[0] = kv[:, XA_INNER:2 * XA_INNER].astype(BF16)


def _mem_kv(mem, g, w):
    b, m, d = mem.shape
    return pl.pallas_call(
        _mem_kv_kernel,
        out_shape=[jax.ShapeDtypeStruct((b, m, XA_INNER), BF16)] * 2,
        grid=(b,),
        in_specs=[pl.BlockSpec((1, m, d), lambda i: (i, 0, 0)), _full(g.shape), _full(w.shape)],
        out_specs=[pl.BlockSpec((1, m, XA_INNER), lambda i: (i, 0, 0))] * 2,
        compiler_params=_cparams(("parallel",)),
        name="mem_kv",
    )(mem, g, w)


def _post_mix_kernel(*refs, n_a):
    x_ref = refs[0]
    a_refs = refs[1:1 + n_a]
    w_refs = refs[1 + n_a:1 + 2 * n_a]
    (g_ref, wq_ref, mk_ref, mv_ref, wo_ref, gffn_ref, wr_hi_ref, wr_lo_ref, br_ref, tri_ref,
     o_ref, route_ref, cnt_ref) = refs[1 + 2 * n_a:]

    @pl.when((pl.program_id(0) == 0) & (pl.program_id(1) == 0))
    def _():
        cnt_ref[...] = jnp.zeros(cnt_ref.shape, F32)

    x1 = x_ref[0]
    for a_ref, w_ref in zip(a_refs, w_refs):
        x1 = x1 + jnp.dot(a_ref[0], w_ref[...], preferred_element_type=F32)
    h = _rms(x1, g_ref[...]).astype(BF16)
    c = XA_HEAD_DIM ** -0.5 * LOG2E
    q = (jnp.dot(h, wq_ref[...], preferred_element_type=F32) * c).astype(BF16)
    outs = []
    for hd in range(XA_HEADS):
        sl = slice(hd * XA_HEAD_DIM, (hd + 1) * XA_HEAD_DIM)
        s = lax.dot_general(q[:, sl], mk_ref[0, :, sl], (((1,), (1,)), ((), ())), preferred_element_type=F32)
        p = jnp.exp2(s - jnp.max(s, axis=1, keepdims=True))
        linv = 1.0 / jnp.sum(p, axis=1, keepdims=True)
        o = jnp.dot(p.astype(BF16), mv_ref[0, :, sl], preferred_element_type=F32) * linv
        outs.append(o.astype(BF16))
    o_all = jnp.concatenate(outs, axis=1)
    x2 = x1 + jnp.dot(o_all, wo_ref[...], preferred_element_type=F32)
    o_ref[0] = x2

    bucket, w_a, w_b = _route(_rms(x2, gffn_ref[...]), wr_hi_ref, wr_lo_ref, br_ref)
    n = x2.shape[0]
    row = lax.broadcasted_iota(jnp.int32, (ROUTE_ROWS, n), 0)
    onehot = row == bucket
    oh = jnp.where(onehot, 1.0, 0.0)
    before = jnp.dot(oh.astype(BF16), tri_ref[...], preferred_element_type=F32) + cnt_ref[:, 0:1]
    rank = jnp.sum(jnp.where(onehot, before, 0.0), axis=0, keepdims=True)
    cnt_ref[...] += jnp.broadcast_to(jnp.sum(oh, axis=1, keepdims=True), cnt_ref.shape)
    row8 = lax.broadcasted_iota(jnp.int32, (SUBLANES, n), 0)
    route_ref[...] = (jnp.where(row8 == 0, bucket.astype(F32), 0.0) + jnp.where(row8 == 1, rank, 0.0)
                      + jnp.where(row8 == 2, w_a, 0.0) + jnp.where(row8 == 3, w_b, 0.0))


def _post_mix(x, a_list, w_list, g, wq, mk, mv, wo, gffn, wr_hi, wr_lo, br):
    b, s, d = x.shape
    tm = TOKEN_TILE
    n_a = len(a_list)
    nt = s // tm
    tok = lambda w: pl.BlockSpec((1, tm, w), lambda i, j: (i, j, 0))
    memspec = pl.BlockSpec((1, MEM_TOKENS, XA_INNER), lambda i, j: (i, 0, 0))
    tri = jnp.triu(jnp.ones((tm, tm), F32), 1).astype(BF16)
    return pl.pallas_call(
        functools.partial(_post_mix_kernel, n_a=n_a),
        out_shape=[jax.ShapeDtypeStruct((b, s, d), F32), jax.ShapeDtypeStruct((SUBLANES, b * s), F32),
                   jax.ShapeDtypeStruct((ROUTE_ROWS, LANES), F32)],
        grid=(b, nt),
        in_specs=([tok(d)] + [tok(a.shape[2]) for a in a_list] + [_full(w.shape) for w in w_list]
                  + [_full(g.shape), _full(wq.shape), memspec, memspec, _full(wo.shape), _full(gffn.shape),
                     _full(wr_hi.shape), _full(wr_lo.shape), _full(br.shape), _full(tri.shape)]),
        out_specs=[tok(d), pl.BlockSpec((SUBLANES, tm), lambda i, j: (0, i * nt + j)),
                   _full((ROUTE_ROWS, LANES))],
        compiler_params=_cparams(("arbitrary", "arbitrary")),
        name="post_mix",
    )(x, *a_list, *w_list, g, wq, mk, mv, wo, gffn, wr_hi, wr_lo, br, tri)


def _route(tf, wr_hi_ref, wr_lo_ref, br_ref):
    t_hi = tf.astype(BF16)
    t_lo = (tf - t_hi.astype(F32)).astype(BF16)
    w_hi, w_lo = wr_hi_ref[...], wr_lo_ref[...]
    nt = (((1,), (1,)), ((), ()))
    logits = (lax.dot_general(w_hi, t_hi, nt, preferred_element_type=F32)
              + lax.dot_general(w_hi, t_lo, nt, preferred_element_type=F32)
              + lax.dot_general(w_lo, t_hi, nt, preferred_element_type=F32)) + br_ref[...]
    row = lax.broadcasted_iota(jnp.int32, logits.shape, 0)
    big = jnp.int32(ROUTE_ROWS)
    is_g = row < N_GROUPS
    lg = jnp.where(is_g, logits, NEG_INF)
    gmax = jnp.max(lg, axis=0, keepdims=True)
    g_p = 1.0 / jnp.sum(jnp.where(is_g, jnp.exp(lg - gmax), 0.0), axis=0, keepdims=True)
    g_i = jnp.min(jnp.where(lg == gmax, row, big), axis=0, keepdims=True)
    lo = ROUTE_EXPERT_ROW0 + EXPERTS_PER_GROUP * g_i
    in_grp = (row >= lo) & (row < lo + EXPERTS_PER_GROUP)
    el = jnp.where(in_grp, logits, NEG_INF)
    v1 = jnp.max(el, axis=0, keepdims=True)
    i1 = jnp.min(jnp.where(el == v1, row, big), axis=0, keepdims=True)
    el2 = jnp.where(row == i1, NEG_INF, el)
    v2 = jnp.max(el2, axis=0, keepdims=True)
    i2 = jnp.min(jnp.where(el2 == v2, row, big), axis=0, keepdims=True)
    e2 = jnp.exp(v2 - v1)
    w1 = g_p / (1.0 + e2)
    w2 = g_p * e2 / (1.0 + e2)
    first_low = i1 < i2
    e_lo = jnp.minimum(i1, i2) - lo
    e_hi = jnp.maximum(i1, i2) - lo
    pair = jnp.where(e_lo == 0, 0, jnp.where(e_lo == 1, 3, 5)) + e_hi - e_lo - 1
    bucket = g_i * PAIRS_PER_GROUP + pair
    return bucket, jnp.where(first_low, w1, w2), jnp.where(first_low, w2, w1)


def _moe_kernel(ea_ref, eb_ref, valid_ref, src_hbm, x_hbm, gate_ref, g_ref, wgu_a_ref, wgu_b_ref,
                wd_a_ref, wd_b_ref, gf_ref, out_hbm, idx_ref, xb0, xb1, ob0, ob1, isem, gsem, ssem, *, n_tiles,
                final_norm):
    n_grp, sub, d = xb0.shape
    tms = n_grp * sub
    i = pl.program_id(0)

    def idx_off(k):
        return lax.rem(k, MOE_IDX_SLOTS) * tms

    def idx_copy(k):
        return pltpu.make_async_copy(src_hbm.at[k], idx_ref.at[pl.ds(idx_off(k), tms)],
                                     isem.at[lax.rem(k, MOE_IDX_SLOTS)])

    def gather_row(tok, xb, s, j, u):
        return pltpu.make_async_copy(x_hbm.at[pl.ds(tok, 1)], xb.at[j, pl.ds(u, 1)], gsem.at[s])

    def scatter_row(tok, ob, s, j, u):
        return pltpu.make_async_copy(ob.at[j, pl.ds(u, 1)], out_hbm.at[pl.ds(tok, 1)], ssem.at[s])

    def start_gather(k, xb, s):
        base = idx_off(k)
        for r in range(tms):
            gather_row(idx_ref[base + r], xb, s, r // sub, r % sub).start(priority=r % 2)

    def wait_gather(xb, s):
        for r in range(tms):
            gather_row(0, xb, s, r // sub, r % sub).wait()

    def for_rows(n, fn):
        @pl.when(n == tms)
        def _():
            for r in range(tms):
                fn(r // sub, r % sub, r)

        @pl.when(n < tms)
        def _():
            full = lax.shift_right_logical(n, 3)

            def grp(j, carry):
                for u in range(sub):
                    fn(j, u, j * sub + u)
                return carry
            lax.fori_loop(0, full, grp, 0)

            def one(u, carry):
                fn(full, u, full * sub + u)
                return carry
            lax.fori_loop(0, lax.bitwise_and(n, sub - 1), one, 0)

    def start_scatter(k, n, ob, s):
        base = idx_off(k)

        def fn(j, u, r):
            prio = r % 2 if isinstance(r, int) else 0
            scatter_row(idx_ref[base + r], ob, s, j, u).start(priority=prio)
        for_rows(n, fn)

    def wait_scatter(n, ob, s):
        for_rows(n, lambda j, u, r: scatter_row(0, ob, s, j, u).wait())

    def tile(s):
        xb, ob = (xb0, ob0) if s == 0 else (xb1, ob1)
        xo, obo = (xb1, ob1) if s == 0 else (xb0, ob0)
        if s == 0:
            @pl.when(i == 0)
            def _():
                idx_copy(0).start()
                idx_copy(0).wait()
                start_gather(0, xb0, 0)
                idx_copy(1).start()

        @pl.when(i + 2 <= n_tiles)
        def _():
            idx_copy(i + 2).start()

        wait_gather(xb, s)

        @pl.when(i >= 2)
        def _():
            wait_scatter(valid_ref[jnp.maximum(i - 2, 0)], ob, s)

        idx_copy(i + 1).wait()
        n_valid = valid_ref[i]
        next_base = idx_off(i + 1)
        chunk = tms // MOE_GATHER_CHUNKS

        def prefetch_rows(c):
            for r in range(c * chunk, (c + 1) * chunk):
                gather_row(idx_ref[next_base + r], xo, 1 - s, r // sub, r % sub).start(priority=r % 2)

        @pl.when(n_valid > 0)
        def _():
            x = xb[...].reshape(tms, d)
            prefetch_rows(0)
            t = _rms(x, g_ref[...]).astype(BF16)
            gates = gate_ref[...]
            prefetch_rows(1)
            y = x
            for e, (wgu_ref, wd_ref) in enumerate(((wgu_a_ref, wd_a_ref), (wgu_b_ref, wd_b_ref))):
                hgu = jnp.dot(t, wgu_ref[0], preferred_element_type=F32)
                prefetch_rows(2 + 3 * e)
                hg, hu = hgu[:, 0:D_EXPERT], hgu[:, D_EXPERT:2 * D_EXPERT]
                act = (hg * jax.nn.sigmoid(hg) * hu * gates[:, e:e + 1]).astype(BF16)
                prefetch_rows(3 + 3 * e)
                y = y + jnp.dot(act, wd_ref[0], preferred_element_type=F32)
                prefetch_rows(4 + 3 * e)
            if final_norm:
                y = _rms(y, gf_ref[...])
            ob[...] = y.reshape(n_grp, sub, d)

        @pl.when(n_valid <= 0)
        def _():
            for c in range(MOE_GATHER_CHUNKS):
                prefetch_rows(c)

        start_scatter(i, n_valid, ob, s)

        @pl.when(i == n_tiles - 1)
        def _():
            wait_gather(xo, 1 - s)
            if n_tiles > 1:
                wait_scatter(valid_ref[jnp.maximum(i - 1, 0)], obo, 1 - s)
            wait_scatter(n_valid, ob, s)

    parity = lax.rem(i, 2)
    pl.when(parity == 0)(lambda: tile(0))
    pl.when(parity == 1)(lambda: tile(1))


def _moe_plan(route, counts, n_tok):
    tms = MOE_TILE
    n_b = N_GROUPS * PAIRS_PER_GROUP
    n_tiles = n_tok // tms + n_b
    bucket = route[0].astype(jnp.int32)
    rank = route[1].astype(jnp.int32)
    cnt = counts[:n_b, 0].astype(jnp.int32)
    padded = (cnt + tms - 1) // tms * tms
    pad_end = jnp.cumsum(padded)
    pad_off = pad_end - padded
    dest = pad_off[bucket] + rank
    payload = jnp.stack([jnp.arange(n_tok, dtype=F32), route[2], route[3]], axis=1)
    plan = jnp.zeros(((n_tiles + 1) * tms, 3), F32).at[dest].set(payload, unique_indices=True)
    src = plan[:, 0].astype(jnp.int32).reshape(n_tiles + 1, tms)
    gates = plan[:n_tiles * tms, 1:3]
    start = jnp.arange(n_tiles, dtype=jnp.int32) * tms
    tb = jnp.minimum(jnp.sum(pad_end[None, :] <= start[:, None], axis=1), n_b - 1).astype(jnp.int32)
    valid = jnp.clip(cnt[tb] - (start - pad_off[tb]), 0, tms).astype(jnp.int32)
    lo_hi = [(a, b) for a in range(EXPERTS_PER_GROUP) for b in range(a + 1, EXPERTS_PER_GROUP)]
    ea_tab = np.array([EXPERTS_PER_GROUP * g + a for g in range(N_GROUPS) for a, _ in lo_hi], np.int32)
    eb_tab = np.array([EXPERTS_PER_GROUP * g + b for g in range(N_GROUPS) for _, b in lo_hi], np.int32)
    return src, gates, jnp.asarray(ea_tab)[tb], jnp.asarray(eb_tab)[tb], valid


def _moe(x2d, route, counts, g, wgu, wd, gf, final_norm):
    n_tok, d = x2d.shape
    tms = MOE_TILE
    src, gates, ea, eb, valid = _moe_plan(route, counts, n_tok)
    n_tiles = src.shape[0] - 1
    any_spec = pl.BlockSpec(memory_space=pl.ANY)
    grid_spec = pltpu.PrefetchScalarGridSpec(
        num_scalar_prefetch=3,
        grid=(n_tiles,),
        in_specs=[any_spec, any_spec,
                  pl.BlockSpec((tms, 2), lambda i, ea_r, eb_r, v_r: (i, 0)),
                  pl.BlockSpec(g.shape, lambda i, ea_r, eb_r, v_r: (0, 0)),
                  pl.BlockSpec((1, d, 2 * D_EXPERT), lambda i, ea_r, eb_r, v_r: (ea_r[i], 0, 0)),
                  pl.BlockSpec((1, d, 2 * D_EXPERT), lambda i, ea_r, eb_r, v_r: (eb_r[i], 0, 0)),
                  pl.BlockSpec((1, D_EXPERT, d), lambda i, ea_r, eb_r, v_r: (ea_r[i], 0, 0)),
                  pl.BlockSpec((1, D_EXPERT, d), lambda i, ea_r, eb_r, v_r: (eb_r[i], 0, 0)),
                  pl.BlockSpec(gf.shape, lambda i, ea_r, eb_r, v_r: (0, 0))],
        out_specs=any_spec,
        scratch_shapes=[pltpu.SMEM((MOE_IDX_SLOTS * tms,), jnp.int32)]
        + [pltpu.VMEM((tms // SUBLANES, SUBLANES, d), F32)] * 4
        + [pltpu.SemaphoreType.DMA((MOE_IDX_SLOTS,)), pltpu.SemaphoreType.DMA((2,)),
           pltpu.SemaphoreType.DMA((2,))],
    )
    return pl.pallas_call(
        functools.partial(_moe_kernel, n_tiles=n_tiles, final_norm=final_norm),
        out_shape=jax.ShapeDtypeStruct((n_tok, d), F32),
        grid_spec=grid_spec,
        compiler_params=_cparams(("arbitrary",)),
        name="moe",
    )(ea, eb, valid, src, x2d, gates, g, wgu, wgu, wd, wd, gf)


def _row(v):
    return v.reshape(1, -1).astype(F32)


def _rope_tables(seq_len, rot_dim, lane_starts):
    inv = ROPE_THETA ** (-jnp.arange(0, rot_dim, 2, dtype=F32) / rot_dim)
    ang = jnp.arange(seq_len, dtype=F32)[:, None] * inv[None, :]
    cos, sin = jnp.cos(ang), jnp.sin(ang)
    ct = jnp.ones((seq_len, LANES), F32)
    st = jnp.zeros((seq_len, LANES), F32)
    for l0 in lane_starts:
        ct = ct.at[:, l0:l0 + rot_dim].set(jnp.concatenate([cos, cos], axis=1))
        st = st.at[:, l0:l0 + rot_dim].set(jnp.concatenate([-sin, sin], axis=1))
    return ct, st


def _prep_even(w_in, w_uq, w_ukv, w_o):
    c = np.cumsum([0, MLA_Q_RANK, MLA_KV_RANK, MLA_ROPE, DIFF_QK, DIFF_QK, DIFF_VW])
    kpe = jnp.zeros((D_MODEL, LANES), F32).at[:, MLA_NOPE:MLA_NOPE + MLA_ROPE].set(w_in[:, c[2]:c[3]])
    win = jnp.concatenate([w_in[:, c[0]:c[2]], w_in[:, c[3]:c[6]], kpe], axis=1).astype(BF16)
    uq = w_uq.reshape(MLA_Q_RANK, MLA_HEADS, MLA_NOPE + MLA_ROPE)
    uq = jnp.pad(uq, ((0, 0), (0, 0), (0, LANES - MLA_NOPE - MLA_ROPE))).reshape(MLA_Q_RANK, MLA_HEADS * LANES)
    ukv = w_ukv.reshape(MLA_KV_RANK, MLA_HEADS, MLA_NOPE + MLA_V)
    uk = jnp.pad(ukv[:, :, :MLA_NOPE], ((0, 0), (0, 0), (0, LANES - MLA_NOPE))).reshape(MLA_KV_RANK,
                                                                                       MLA_HEADS * LANES)
    uv = ukv[:, :, MLA_NOPE:].reshape(MLA_KV_RANK, MLA_HEADS * MLA_V)
    n_mla = MLA_HEADS * MLA_V
    return (win, uq.astype(BF16), uk.astype(BF16), uv.astype(BF16),
            w_o[:n_mla].astype(BF16), w_o[n_mla:].astype(BF16))


def _prep_moe(w_rg, b_rg, w_re, b_re, w_gate, w_up, w_down):
    e0 = ROUTE_EXPERT_ROW0
    wr = jnp.zeros((ROUTE_ROWS, D_MODEL), F32).at[:N_GROUPS].set(w_rg.T).at[e0:e0 + N_EXPERTS].set(w_re.T)
    br = jnp.zeros((ROUTE_ROWS, 1), F32).at[:N_GROUPS, 0].set(b_rg).at[e0:e0 + N_EXPERTS, 0].set(b_re)
    wr_hi = wr.astype(BF16)
    wr_lo = (wr - wr_hi.astype(F32)).astype(BF16)
    wgu = jnp.concatenate([w_gate, w_up], axis=-1).reshape(N_EXPERTS, D_MODEL, 2 * D_EXPERT).astype(BF16)
    wd = w_down.reshape(N_EXPERTS, D_EXPERT, D_MODEL).astype(BF16)
    return wr_hi, wr_lo, br, wgu, wd


def _trunk(x, mem, p, tables):
    b, s, d = x.shape
    depth = p['ln_mix'].shape[0]
    for layer in range(depth):
        i = layer // 2
        if layer % 2 == 0:
            win, uq, uk, uv, wo_a, wo_b = _prep_even(p['w_in_even'][i], p['mla_w_uq'][i], p['mla_w_ukv'][i],
                                                     p['w_o_even'][i])
            cm, sm, cd, sd = tables[s]
            q, k, v, dq, dk, dv = _even_in(x, _row(p['ln_mix'][layer]), win, _row(p['mla_q_norm'][i]),
                                           _row(p['mla_kv_norm'][i]), uq, uk, uv, cm, sm, cd, sd)
            o_mla = _flash(q, k, v, 2 * LANES)
            lamp = jnp.zeros((8, LANES), F32)
            for r, name in enumerate(('diff_lambda_q1', 'diff_lambda_k1', 'diff_lambda_q2', 'diff_lambda_k2')):
                lamp = lamp.at[r, :DIFF_HEAD_DIM].set(p[name][i])
            lam_init = 0.8 - 0.6 * math.exp(-0.3 * layer)
            o_diff = _flash(dq, dk, dv, LANES, lamp=lamp, subln=_row(p['diff_subln'][i]), lam_init=lam_init)
            a_list, w_list = [o_mla, o_diff], [wo_a, wo_b]
        else:
            q, k, v = _odd_in(x, _row(p['ln_mix'][layer]), p['w_in_odd'][i].astype(BF16))
            tbl, idx = _na_tables(p['na_rpb'][i])
            a_list, w_list = [_na_attention(q, k, v, tbl, idx)], [p['w_o_odd'][i].astype(BF16)]
        mk, mv = _mem_kv(mem, _row(p['ln_mem'][layer]), p['xa_w_kv'][layer].astype(BF16))
        wr_hi, wr_lo, br, wgu, wd = _prep_moe(p['moe_w_group'][layer], p['moe_b_group'][layer],
                                              p['moe_w_expert'][layer], p['moe_b_expert'][layer],
                                              p['moe_w_gate'][layer], p['moe_w_up'][layer], p['moe_w_down'][layer])
        x, route, counts = _post_mix(x, a_list, w_list, _row(p['ln_xattn'][layer]),
                                     p['xa_w_q'][layer].astype(BF16), mk, mv, p['xa_w_o'][layer].astype(BF16),
                                     _row(p['ln_ffn'][layer]), wr_hi, wr_lo, br)
        x = _moe(x.reshape(b * s, d), route, counts, _row(p['ln_ffn'][layer]), wgu, wd,
                 _row(p['ln_final']), final_norm=(layer == depth - 1)).reshape(b, s, d)
    return x


def kernel(x_prompt, x_sample, mem_prompt, mem_sample, ln_mix, w_in_even, mla_q_norm, mla_kv_norm, mla_w_uq, mla_w_ukv, diff_lambda_q1, diff_lambda_k1, diff_lambda_q2, diff_lambda_k2, diff_subln, w_o_even, w_in_odd, na_rpb, w_o_odd, ln_xattn, ln_mem, xa_w_q, xa_w_kv, xa_w_o, ln_ffn, moe_w_group, moe_b_group, moe_w_expert, moe_b_expert, moe_w_gate, moe_w_up, moe_w_down, ln_final):
    p = dict(ln_mix=ln_mix, w_in_even=w_in_even, mla_q_norm=mla_q_norm, mla_kv_norm=mla_kv_norm,
             mla_w_uq=mla_w_uq, mla_w_ukv=mla_w_ukv, diff_lambda_q1=diff_lambda_q1,
             diff_lambda_k1=diff_lambda_k1, diff_lambda_q2=diff_lambda_q2, diff_lambda_k2=diff_lambda_k2,
             diff_subln=diff_subln, w_o_even=w_o_even, w_in_odd=w_in_odd, na_rpb=na_rpb, w_o_odd=w_o_odd,
             ln_xattn=ln_xattn, ln_mem=ln_mem, xa_w_q=xa_w_q, xa_w_kv=xa_w_kv, xa_w_o=xa_w_o,
             ln_ffn=ln_ffn, moe_w_group=moe_w_group, moe_b_group=moe_b_group, moe_w_expert=moe_w_expert,
             moe_b_expert=moe_b_expert, moe_w_gate=moe_w_gate, moe_w_up=moe_w_up, moe_w_down=moe_w_down,
             ln_final=ln_final)
    tables = {}
    for s in {x_prompt.shape[1], x_sample.shape[1]}:
        cm, sm = _rope_tables(s, MLA_ROPE, (MLA_NOPE,))
        cd, sd = _rope_tables(s, DIFF_ROT, (0, DIFF_HEAD_DIM))
        tables[s] = (cm, sm, cd, sd)
    return (_trunk(x_prompt, mem_prompt, p, tables), _trunk(x_sample, mem_sample, p, tables))
```

```python
import functools
import math

import numpy as np
import jax
import jax.numpy as jnp
from jax import lax
from jax.experimental import pallas as pl
from jax.experimental.pallas import tpu as pltpu

F32 = jnp.float32
BF16 = jnp.bfloat16

D_MODEL = 1024
ROPE_THETA = 500000.0
NORM_EPS = 1e-6
NEG_INF = -1e30
LOG2E = math.log2(math.e)

MLA_HEADS = 8
MLA_Q_RANK = 384
MLA_KV_RANK = 256
MLA_NOPE = 64
MLA_ROPE = 32
MLA_V = 64
DIFF_HEADS = 4
DIFF_HEAD_DIM = 64
DIFF_ROT = DIFF_HEAD_DIM // 4
DIFF_QK = DIFF_HEADS * 2 * DIFF_HEAD_DIM
DIFF_VW = DIFF_HEADS * 2 * DIFF_HEAD_DIM
GRID_W = 64
NA_HEADS = 16
NA_HEAD_DIM = 64
WIN_ROWS = 8
WIN_COLS = 16
MEM_TOKENS = 256
XA_HEADS = 4
XA_HEAD_DIM = 128
XA_INNER = XA_HEADS * XA_HEAD_DIM
N_GROUPS = 4
EXPERTS_PER_GROUP = 4
N_EXPERTS = N_GROUPS * EXPERTS_PER_GROUP
PAIRS_PER_GROUP = EXPERTS_PER_GROUP * (EXPERTS_PER_GROUP - 1) // 2
ROUTE_ROWS = 32
ROUTE_EXPERT_ROW0 = 8
D_EXPERT = 256

LANES = 128
SUBLANES = 8
VMEM_LIMIT = 56 * 1024 * 1024

TOKEN_TILE = 512
MOE_TILE = 256
MOE_GATHER_CHUNKS = 8
MOE_IDX_SLOTS = 4
FLASH_TQ = 1024
FLASH_TK = 512
FLASH_BOUND_SLACK = 1.01
FLASH_MIN_ROW_SUM = 2.0 ** -60
FLASH_UNROLL = 8
NA_ROW_BLOCK = 16
NA_KEY_ROWS = NA_ROW_BLOCK + WIN_ROWS
NA_GROUP_ROWS = 4
NA_GROUP_KEY_ROWS = NA_GROUP_ROWS + WIN_ROWS
NA_TILE_CODES = 3
NA_TILES_PER_HEAD = NA_TILE_CODES * 16 + 1


def _cparams(sem):
    return pltpu.CompilerParams(dimension_semantics=sem, vmem_limit_bytes=VMEM_LIMIT)


def _rms(xf, g):
    return xf * lax.rsqrt(jnp.mean(xf * xf, axis=-1, keepdims=True) + NORM_EPS) * g


def _full(shape):
    nd = len(shape)
    return pl.BlockSpec(shape, lambda *_: (0,) * nd)


def _rope_block(x, cos, sin, half):
    lane = lax.broadcasted_iota(jnp.int32, x.shape, 1)
    up = pltpu.roll(x, LANES - half, 1)
    down = pltpu.roll(x, half, 1)
    first = lax.bitwise_and(lane, 2 * half - 1) < half
    return x * cos + jnp.where(first, up, down) * sin


def _even_in_kernel(x_ref, g_ref, win_ref, qn_ref, kvn_ref, wuq_ref, wuk_ref, wuv_ref,
                    cm_ref, sm_ref, cd_ref, sd_ref,
                    q_ref, k_ref, v_ref, dq_ref, dk_ref, dv_ref):
    h = _rms(x_ref[0], g_ref[...]).astype(BF16)
    proj = jnp.dot(h, win_ref[...], preferred_element_type=F32)
    o_cq, o_ckv, o_dq, o_dk, o_dv, o_kpe = 0, 384, 640, 1152, 1664, 2176
    cm, sm, cd, sd = cm_ref[...], sm_ref[...], cd_ref[...], sd_ref[...]

    cq = _rms(proj[:, o_cq:o_cq + MLA_Q_RANK], qn_ref[...]).astype(BF16)
    q = jnp.dot(cq, wuq_ref[...], preferred_element_type=F32)
    mla_c = (MLA_NOPE + MLA_ROPE) ** -0.5 * LOG2E
    for hd in range(MLA_HEADS):
        sl = slice(hd * LANES, (hd + 1) * LANES)
        q_ref[0, :, sl] = (_rope_block(q[:, sl], cm, sm, MLA_ROPE // 2) * mla_c).astype(BF16)

    ckv = _rms(proj[:, o_ckv:o_ckv + MLA_KV_RANK], kvn_ref[...]).astype(BF16)
    kn = jnp.dot(ckv, wuk_ref[...], preferred_element_type=F32)
    kpe = _rope_block(proj[:, o_kpe:o_kpe + LANES], cm, sm, MLA_ROPE // 2)
    for hd in range(MLA_HEADS):
        sl = slice(hd * LANES, (hd + 1) * LANES)
        k_ref[0, :, sl] = (kn[:, sl] + kpe).astype(BF16)
    v_ref[0] = jnp.dot(ckv, wuv_ref[...], preferred_element_type=F32).astype(BF16)

    diff_c = DIFF_HEAD_DIM ** -0.5 * LOG2E
    for hd in range(DIFF_HEADS):
        sl = slice(hd * LANES, (hd + 1) * LANES)
        dq_ref[0, :, sl] = (_rope_block(proj[:, o_dq + hd * LANES:o_dq + (hd + 1) * LANES], cd, sd,
                                        DIFF_ROT // 2) * diff_c).astype(BF16)
        dk_ref[0, :, sl] = _rope_block(proj[:, o_dk + hd * LANES:o_dk + (hd + 1) * LANES], cd, sd,
                                       DIFF_ROT // 2).astype(BF16)
    dv_ref[0] = proj[:, o_dv:o_dv + DIFF_VW].astype(BF16)


def _even_in(x, g, win, qn, kvn, wuq, wuk, wuv, cm, sm, cd, sd):
    b, s, d = x.shape
    tm = TOKEN_TILE
    tok = lambda w: pl.BlockSpec((1, tm, w), lambda i, j: (i, j, 0))
    tab = pl.BlockSpec((tm, LANES), lambda i, j: (j, 0))
    outs = [jax.ShapeDtypeStruct((b, s, w), BF16) for w in (1024, 1024, 512, 512, 512, 512)]
    return pl.pallas_call(
        _even_in_kernel,
        out_shape=outs,
        grid=(b, s // tm),
        in_specs=[tok(d), _full(g.shape), _full(win.shape), _full(qn.shape), _full(kvn.shape),
                  _full(wuq.shape), _full(wuk.shape), _full(wuv.shape), tab, tab, tab, tab],
        out_specs=[tok(1024), tok(1024), tok(512), tok(512), tok(512), tok(512)],
        compiler_params=_cparams(("parallel", "parallel")),
        name="even_in",
    )(x, g, win, qn, kvn, wuq, wuk, wuv, cm, sm, cd, sd)


def _flash_kernel(*refs, tq, tk, n_k, width, diff, lam_init):
    if diff:
        q_ref, k_ref, v_ref, lamp_ref, g_ref, o_ref, qs_ref, m_ref, l_ref, acc_ref, kn_ref, kmax_ref = refs
    else:
        q_ref, k_ref, v_ref, o_ref, qs_ref, m_ref, l_ref, acc_ref, kn_ref, kmax_ref = refs
    half = width // 2
    n_j = tk // LANES
    q = q_ref[0]
    lane = lax.broadcasted_iota(jnp.int32, (tq, width), 1)
    zero = jnp.zeros_like(q)
    qs_ref[0:tq, :] = jnp.where(lane < half, q, zero)
    qs_ref[tq:2 * tq, :] = jnp.where(lane >= half, q, zero)

    @pl.when(pl.program_id(2) == 0)
    def _():
        r_i = lax.broadcasted_iota(jnp.int32, (width, LANES), 0)
        c_i = lax.broadcasted_iota(jnp.int32, (width, LANES), 1)
        sel = jnp.where(((c_i == 0) & (r_i < half)) | ((c_i == 1) & (r_i >= half)), 1.0, 0.0).astype(BF16)
        kn_ref[...] = jnp.zeros(kn_ref.shape, F32)

        def knorm(i, carry):
            kc = k_ref[0, pl.ds(pl.multiple_of(i * tk, tk), tk), :].astype(F32)
            kn_ref[...] = jnp.maximum(kn_ref[...], jnp.dot((kc * kc).astype(BF16), sel, preferred_element_type=F32))
            return carry

        lax.fori_loop(0, n_k, knorm, 0)
        kmax = jnp.sqrt(jnp.max(kn_ref[...], axis=0, keepdims=True))
        kmax_ref[0:1, :] = jnp.broadcast_to(kmax[:, 0:1], (1, LANES))
        kmax_ref[1:2, :] = jnp.broadcast_to(kmax[:, 1:2], (1, LANES))

    def reset(m):
        m_ref[...] = m
        l_ref[...] = jnp.zeros(l_ref.shape, F32)
        acc_ref[...] = jnp.zeros(acc_ref.shape, F32)

    def scores(i):
        off = pl.multiple_of(i * tk, tk)
        kc = k_ref[0, pl.ds(off, tk), :]
        return off, lax.dot_general(qs_ref[...], kc, (((1,), (1,)), ((), ())), preferred_element_type=F32)

    def max_pass(i, carry):
        _, s = scores(i)
        m = m_ref[...]
        for j in range(n_j):
            m = jnp.maximum(m, s[:, j * LANES:(j + 1) * LANES])
        m_ref[...] = m
        return carry

    def pv_pass(i, carry):
        off, s = scores(i)
        vc = v_ref[0, pl.ds(off, tk), :]
        m = m_ref[...]
        ps = [jnp.exp2(s[:, j * LANES:(j + 1) * LANES] - m) for j in range(n_j)]
        lsum = ps[0]
        for pj in ps[1:]:
            lsum = lsum + pj
        l_ref[...] += lsum
        p = jnp.concatenate([pj.astype(BF16) for pj in ps], axis=1)
        acc_ref[...] += jnp.dot(p, vc, preferred_element_type=F32)
        return carry

    qf = qs_ref[...].astype(F32)
    qn = jnp.sqrt(jnp.sum(qf * qf, axis=1, keepdims=True))
    bound_top = qn[0:tq] * kmax_ref[0:1, :] * FLASH_BOUND_SLACK
    bound_bot = qn[tq:2 * tq] * kmax_ref[1:2, :] * FLASH_BOUND_SLACK
    reset(jnp.concatenate([bound_top, bound_bot], axis=0))
    lax.fori_loop(0, n_k, pv_pass, 0, unroll=FLASH_UNROLL)
    l_min = jnp.min(jnp.sum(l_ref[...], axis=1, keepdims=True))

    @pl.when(jnp.logical_not(l_min >= FLASH_MIN_ROW_SUM))
    def _():
        reset(jnp.full(m_ref.shape, NEG_INF, F32))
        lax.fori_loop(0, n_k, max_pass, 0, unroll=FLASH_UNROLL)
        m_ref[...] = jnp.broadcast_to(jnp.max(m_ref[...], axis=1, keepdims=True), m_ref.shape)
        lax.fori_loop(0, n_k, pv_pass, 0, unroll=FLASH_UNROLL)

    o = acc_ref[...] / jnp.sum(l_ref[...], axis=1, keepdims=True)
    top, bot = o[0:tq], o[tq:2 * tq]
    if diff:
        lp = lamp_ref[...]
        lam = (jnp.exp(jnp.sum(lp[0:1] * lp[1:2], axis=1, keepdims=True))
               - jnp.exp(jnp.sum(lp[2:3] * lp[3:4], axis=1, keepdims=True)) + lam_init)
        od = top - lam * bot
        o_ref[0] = (_rms(od, g_ref[...]) * (1.0 - lam_init)).astype(BF16)
    else:
        lane_o = lax.broadcasted_iota(jnp.int32, (tq, LANES), 1)
        o_ref[0] = jnp.where(lane_o < LANES // 2, top, bot).astype(BF16)


def _flash(q, k, v, width, lamp=None, subln=None, lam_init=0.0):
    b, s, _ = q.shape
    n = q.shape[2] // width
    tq, tk = FLASH_TQ, FLASH_TK
    diff = lamp is not None
    kern = functools.partial(_flash_kernel, tq=tq, tk=tk, n_k=s // tk, width=width, diff=diff,
                             lam_init=lam_init)
    in_specs = [pl.BlockSpec((1, tq, width), lambda bi, p, qi: (bi, qi, p)),
                pl.BlockSpec((1, s, width), lambda bi, p, qi: (bi, 0, p)),
                pl.BlockSpec((1, s, LANES), lambda bi, p, qi: (bi, 0, p))]
    args = [q, k, v]
    if diff:
        in_specs += [_full(lamp.shape), _full(subln.shape)]
        args += [lamp, subln]
    return pl.pallas_call(
        kern,
        out_shape=jax.ShapeDtypeStruct((b, s, n * LANES), BF16),
        grid=(b, n, s // tq),
        in_specs=in_specs,
        out_specs=pl.BlockSpec((1, tq, LANES), lambda bi, p, qi: (bi, qi, p)),
        scratch_shapes=[pltpu.VMEM((2 * tq, width), BF16), pltpu.VMEM((2 * tq, LANES), F32),
                        pltpu.VMEM((2 * tq, LANES), F32), pltpu.VMEM((2 * tq, LANES), F32),
                        pltpu.VMEM((tk, LANES), F32), pltpu.VMEM((SUBLANES, LANES), F32)],
        compiler_params=_cparams(("parallel", "parallel", "arbitrary")),
        name="flash_diff" if diff else "flash_mla",
    )(*args)


def _odd_in_kernel(x_ref, g_ref, w_ref, q_ref, k_ref, v_ref):
    h = _rms(x_ref[0], g_ref[...]).astype(BF16)
    qkv = jnp.dot(h, w_ref[...], preferred_element_type=F32)
    c = NA_HEAD_DIM ** -0.5 * LOG2E
    q_ref[0] = (qkv[:, 0:1024] * c).astype(BF16)
    k_ref[0] = qkv[:, 1024:2048].astype(BF16)
    v_ref[0] = qkv[:, 2048:3072].astype(BF16)


def _odd_in(x, g, w):
    b, s, d = x.shape
    tm = TOKEN_TILE
    tok = pl.BlockSpec((1, tm, d), lambda i, j: (i, j, 0))
    return pl.pallas_call(
        _odd_in_kernel,
        out_shape=[jax.ShapeDtypeStruct((b, s, d), BF16)] * 3,
        grid=(b, s // tm),
        in_specs=[tok, _full(g.shape), _full(w.shape)],
        out_specs=[tok, tok, tok],
        compiler_params=_cparams(("parallel", "parallel")),
        name="odd_in",
    )(x, g, w)


def _na_kernel(idx_ref, q_ref, k_ref, v_ref, tbl_ref, o_ref, qs_ref, os_ref, *, rows, n_rb):
    nq = NA_ROW_BLOCK * GRID_W
    nk = NA_KEY_ROWS * GRID_W
    gq = NA_GROUP_ROWS * GRID_W
    n_kp = NA_KEY_ROWS // 2
    rb = pl.program_id(2)
    ks = jnp.clip(NA_ROW_BLOCK * rb - WIN_ROWS // 2, 0, rows - NA_KEY_ROWS)
    pat = jnp.where(rb == 0, 0, jnp.where(rb == n_rb - 1, 2, 1))
    off = pl.multiple_of(ks * GRID_W, (WIN_ROWS // 2) * GRID_W)
    q = q_ref[0]
    lane = lax.broadcasted_iota(jnp.int32, (nq, LANES), 1)
    zero = jnp.zeros_like(q)
    qs_ref[0:nq, :] = jnp.where(lane < LANES // 2, q, zero)
    qs_ref[nq:2 * nq, :] = jnp.where(lane >= LANES // 2, q, zero)

    def group(g, carry):
        r0 = pl.multiple_of(g * gq, gq)
        per_head = NA_ROW_BLOCK // NA_GROUP_ROWS
        hh = g // per_head
        i0 = (g % per_head) * NA_GROUP_ROWS
        w0 = jnp.clip(i0 + (pat - 1) * (WIN_ROWS // 2), 0, NA_KEY_ROWS - NA_GROUP_KEY_ROWS)
        goff = pl.multiple_of(off + w0 * GRID_W, (WIN_ROWS // 2) * GRID_W)
        kw = k_ref[0, pl.ds(goff, NA_GROUP_KEY_ROWS * GRID_W), :]
        vw = v_ref[0, pl.ds(goff, NA_GROUP_KEY_ROWS * GRID_W), :]
        s = lax.dot_general(qs_ref[pl.ds(r0, gq), :], kw, (((1,), (1,)), ((), ())), preferred_element_type=F32)
        bias_rows = []
        for ii in range(NA_GROUP_ROWS):
            base = pat * (NA_ROW_BLOCK * n_kp) + (i0 + ii) * n_kp + lax.shift_right_logical(w0, 1)
            bias_rows.append(jnp.concatenate(
                [tbl_ref[0, hh * NA_TILES_PER_HEAD + idx_ref[base + kp]]
                 for kp in range(NA_GROUP_KEY_ROWS // 2)], axis=1))
        s = s + jnp.concatenate(bias_rows, axis=0)
        m = jnp.max(s, axis=1, keepdims=True)
        p = jnp.exp2(s - m)
        linv = 1.0 / jnp.sum(p, axis=1, keepdims=True)
        os_ref[pl.ds(r0, gq), :] = jnp.dot(p.astype(BF16), vw, preferred_element_type=F32) * linv
        return carry

    lax.fori_loop(0, 2 * nq // gq, group, 0, unroll=True)
    o_ref[0] = jnp.where(lane < LANES // 2, os_ref[0:nq, :], os_ref[nq:2 * nq, :]).astype(BF16)


def _na_tables(rpb):
    col = np.arange(GRID_W)
    col_start = np.clip(col - WIN_COLS // 2, 0, GRID_W - WIN_COLS)
    col_valid = (col[None, :] >= col_start[:, None]) & (col[None, :] < col_start[:, None] + WIN_COLS)
    col_idx = np.clip(col[None, :] - col[:, None], -(WIN_COLS - 1), WIN_COLS - 1) + WIN_COLS - 1
    n_ri = 2 * WIN_ROWS - 1
    ri = np.zeros((NA_TILES_PER_HEAD, LANES), np.int32)
    ok = np.zeros((NA_TILES_PER_HEAD, LANES), bool)
    for code in range(NA_TILE_CODES):
        for a in range(-1, n_ri):
            t = code * 16 + a + 1
            ri[t, :GRID_W] = np.clip(a, 0, n_ri - 1)
            ri[t, GRID_W:] = np.clip(a + 1, 0, n_ri - 1)
            ok[t, :GRID_W] = code in (0, 1) and 0 <= a < n_ri
            ok[t, GRID_W:] = code in (0, 2) and 0 <= a + 1 < n_ri
    onehot = (col_idx.reshape(-1)[None, :] == np.arange(2 * WIN_COLS - 1)[:, None]).astype(np.float32)
    band = jnp.dot(rpb.reshape(NA_HEADS * n_ri, 2 * WIN_COLS - 1).astype(F32), jnp.asarray(onehot),
                   precision=lax.Precision.HIGHEST).reshape(NA_HEADS, n_ri, GRID_W, GRID_W)
    band = jnp.where(col_valid[None, None], band * LOG2E, NEG_INF)
    masked = jnp.full((NA_HEADS, GRID_W, GRID_W), NEG_INF, F32)
    tiles = []
    for t in range(NA_TILES_PER_HEAD):
        left = band[:, ri[t, 0]] if ok[t, 0] else masked
        right = band[:, ri[t, GRID_W]] if ok[t, GRID_W] else masked
        tiles.append(jnp.concatenate([left, right], axis=-1))
    tbl = jnp.stack(tiles, axis=1).reshape(NA_HEADS // 2, 2 * NA_TILES_PER_HEAD, GRID_W, LANES)

    idx = np.zeros((3, NA_ROW_BLOCK, NA_KEY_ROWS // 2), np.int32)
    for pat in range(3):
        for i in range(NA_ROW_BLOCK):
            rel = (i, i + WIN_ROWS // 2, i + WIN_ROWS)[pat]
            start = (max(i - WIN_ROWS // 2, 0), i, min(i + WIN_ROWS // 2, NA_KEY_ROWS - WIN_ROWS))[pat]
            for kp in range(NA_KEY_ROWS // 2):
                kk = 2 * kp
                a = kk - rel + WIN_ROWS - 1
                v0 = start <= kk < start + WIN_ROWS
                v1 = start <= kk + 1 < start + WIN_ROWS
                if v0:
                    assert 0 <= a < n_ri
                if v1:
                    assert 0 <= a + 1 < n_ri
                if v0 and v1:
                    idx[pat, i, kp] = a + 1
                elif v0:
                    idx[pat, i, kp] = 16 + a + 1
                elif v1:
                    idx[pat, i, kp] = 32 + a + 1
                else:
                    idx[pat, i, kp] = NA_TILES_PER_HEAD - 1
    return tbl, jnp.asarray(idx.reshape(-1))


def _na_attention(q, k, v, tbl, idx):
    b, s, d = q.shape
    rows = s // GRID_W
    n_rb = rows // NA_ROW_BLOCK
    nq = NA_ROW_BLOCK * GRID_W
    nk = NA_KEY_ROWS * GRID_W
    kern = functools.partial(_na_kernel, rows=rows, n_rb=n_rb)
    grid_spec = pltpu.PrefetchScalarGridSpec(
        num_scalar_prefetch=1,
        grid=(NA_HEADS // 2, b, n_rb),
        in_specs=[pl.BlockSpec((1, nq, LANES), lambda p, bi, r, idx_r: (bi, r, p)),
                  pl.BlockSpec((1, s, LANES), lambda p, bi, r, idx_r: (bi, 0, p)),
                  pl.BlockSpec((1, s, LANES), lambda p, bi, r, idx_r: (bi, 0, p)),
                  pl.BlockSpec((1, 2 * NA_TILES_PER_HEAD, GRID_W, LANES), lambda p, bi, r, idx_r: (p, 0, 0, 0))],
        out_specs=pl.BlockSpec((1, nq, LANES), lambda p, bi, r, idx_r: (bi, r, p)),
        scratch_shapes=[pltpu.VMEM((2 * nq, LANES), BF16), pltpu.VMEM((2 * nq, LANES), F32)],
    )
    return pl.pallas_call(
        kern,
        out_shape=jax.ShapeDtypeStruct((b, s, d), BF16),
        grid_spec=grid_spec,
        compiler_params=_cparams(("parallel", "parallel", "arbitrary")),
        name="na_attention",
    )(idx, q, k, v, tbl)


def _mem_kv_kernel(m_ref, g_ref, w_ref, k_ref, v_ref):
    h = _rms(m_ref[0], g_ref[...]).astype(BF16)
    kv = jnp.dot(h, w_ref[...], preferred_element_type=F32)
    k_ref[0] = kv[:, 0:XA_INNER].astype(BF16)
    v_ref[0] = kv[:, XA_INNER:2 * XA_INNER].astype(BF16)


def _mem_kv(mem, g, w):
    b, m, d = mem.shape
    return pl.pallas_call(
        _mem_kv_kernel,
        out_shape=[jax.ShapeDtypeStruct((b, m, XA_INNER), BF16)] * 2,
        grid=(b,),
        in_specs=[pl.BlockSpec((1, m, d), lambda i: (i, 0, 0)), _full(g.shape), _full(w.shape)],
        out_specs=[pl.BlockSpec((1, m, XA_INNER), lambda i: (i, 0, 0))] * 2,
        compiler_params=_cparams(("parallel",)),
        name="mem_kv",
    )(mem, g, w)


def _post_mix_kernel(*refs, n_a):
    x_ref = refs[0]
    a_refs = refs[1:1 + n_a]
    w_refs = refs[1 + n_a:1 + 2 * n_a]
    (g_ref, wq_ref, mk_ref, mv_ref, wo_ref, gffn_ref, wr_hi_ref, wr_lo_ref, br_ref, tri_ref,
     o_ref, route_ref, cnt_ref) = refs[1 + 2 * n_a:]

    @pl.when((pl.program_id(0) == 0) & (pl.program_id(1) == 0))
    def _():
        cnt_ref[...] = jnp.zeros(cnt_ref.shape, F32)

    x1 = x_ref[0]
    for a_ref, w_ref in zip(a_refs, w_refs):
        x1 = x1 + jnp.dot(a_ref[0], w_ref[...], preferred_element_type=F32)
    h = _rms(x1, g_ref[...]).astype(BF16)
    c = XA_HEAD_DIM ** -0.5 * LOG2E
    q = (jnp.dot(h, wq_ref[...], preferred_element_type=F32) * c).astype(BF16)
    outs = []
    for hd in range(XA_HEADS):
        sl = slice(hd * XA_HEAD_DIM, (hd + 1) * XA_HEAD_DIM)
        s = lax.dot_general(q[:, sl], mk_ref[0, :, sl], (((1,), (1,)), ((), ())), preferred_element_type=F32)
        p = jnp.exp2(s - jnp.max(s, axis=1, keepdims=True))
        linv = 1.0 / jnp.sum(p, axis=1, keepdims=True)
        o = jnp.dot(p.astype(BF16), mv_ref[0, :, sl], preferred_element_type=F32) * linv
        outs.append(o.astype(BF16))
    o_all = jnp.concatenate(outs, axis=1)
    x2 = x1 + jnp.dot(o_all, wo_ref[...], preferred_element_type=F32)
    o_ref[0] = x2

    bucket, w_a, w_b = _route(_rms(x2, gffn_ref[...]), wr_hi_ref, wr_lo_ref, br_ref)
    n = x2.shape[0]
    row = lax.broadcasted_iota(jnp.int32, (ROUTE_ROWS, n), 0)
    onehot = row == bucket
    oh = jnp.where(onehot, 1.0, 0.0)
    before = jnp.dot(oh.astype(BF16), tri_ref[...], preferred_element_type=F32) + cnt_ref[:, 0:1]
    rank = jnp.sum(jnp.where(onehot, before, 0.0), axis=0, keepdims=True)
    cnt_ref[...] += jnp.broadcast_to(jnp.sum(oh, axis=1, keepdims=True), cnt_ref.shape)
    row8 = lax.broadcasted_iota(jnp.int32, (SUBLANES, n), 0)
    route_ref[...] = (jnp.where(row8 == 0, bucket.astype(F32), 0.0) + jnp.where(row8 == 1, rank, 0.0)
                      + jnp.where(row8 == 2, w_a, 0.0) + jnp.where(row8 == 3, w_b, 0.0))


def _post_mix(x, a_list, w_list, g, wq, mk, mv, wo, gffn, wr_hi, wr_lo, br):
    b, s, d = x.shape
    tm = TOKEN_TILE
    n_a = len(a_list)
    nt = s // tm
    tok = lambda w: pl.BlockSpec((1, tm, w), lambda i, j: (i, j, 0))
    memspec = pl.BlockSpec((1, MEM_TOKENS, XA_INNER), lambda i, j: (i, 0, 0))
    tri = jnp.triu(jnp.ones((tm, tm), F32), 1).astype(BF16)
    return pl.pallas_call(
        functools.partial(_post_mix_kernel, n_a=n_a),
        out_shape=[jax.ShapeDtypeStruct((b, s, d), F32), jax.ShapeDtypeStruct((SUBLANES, b * s), F32),
                   jax.ShapeDtypeStruct((ROUTE_ROWS, LANES), F32)],
        grid=(b, nt),
        in_specs=([tok(d)] + [tok(a.shape[2]) for a in a_list] + [_full(w.shape) for w in w_list]
                  + [_full(g.shape), _full(wq.shape), memspec, memspec, _full(wo.shape), _full(gffn.shape),
                     _full(wr_hi.shape), _full(wr_lo.shape), _full(br.shape), _full(tri.shape)]),
        out_specs=[tok(d), pl.BlockSpec((SUBLANES, tm), lambda i, j: (0, i * nt + j)),
                   _full((ROUTE_ROWS, LANES))],
        compiler_params=_cparams(("arbitrary", "arbitrary")),
        name="post_mix",
    )(x, *a_list, *w_list, g, wq, mk, mv, wo, gffn, wr_hi, wr_lo, br, tri)


def _route(tf, wr_hi_ref, wr_lo_ref, br_ref):
    t_hi = tf.astype(BF16)
    t_lo = (tf - t_hi.astype(F32)).astype(BF16)
    w_hi, w_lo = wr_hi_ref[...], wr_lo_ref[...]
    nt = (((1,), (1,)), ((), ()))
    logits = (lax.dot_general(w_hi, t_hi, nt, preferred_element_type=F32)
              + lax.dot_general(w_hi, t_lo, nt, preferred_element_type=F32)
              + lax.dot_general(w_lo, t_hi, nt, preferred_element_type=F32)) + br_ref[...]
    row = lax.broadcasted_iota(jnp.int32, logits.shape, 0)
    big = jnp.int32(ROUTE_ROWS)
    is_g = row < N_GROUPS
    lg = jnp.where(is_g, logits, NEG_INF)
    gmax = jnp.max(lg, axis=0, keepdims=True)
    g_p = 1.0 / jnp.sum(jnp.where(is_g, jnp.exp(lg - gmax), 0.0), axis=0, keepdims=True)
    g_i = jnp.min(jnp.where(lg == gmax, row, big), axis=0, keepdims=True)
    lo = ROUTE_EXPERT_ROW0 + EXPERTS_PER_GROUP * g_i
    in_grp = (row >= lo) & (row < lo + EXPERTS_PER_GROUP)
    el = jnp.where(in_grp, logits, NEG_INF)
    v1 = jnp.max(el, axis=0, keepdims=True)
    i1 = jnp.min(jnp.where(el == v1, row, big), axis=0, keepdims=True)
    el2 = jnp.where(row == i1, NEG_INF, el)
    v2 = jnp.max(el2, axis=0, keepdims=True)
    i2 = jnp.min(jnp.where(el2 == v2, row, big), axis=0, keepdims=True)
    e2 = jnp.exp(v2 - v1)
    w1 = g_p / (1.0 + e2)
    w2 = g_p * e2 / (1.0 + e2)
    first_low = i1 < i2
    e_lo = jnp.minimum(i1, i2) - lo
    e_hi = jnp.maximum(i1, i2) - lo
    pair = jnp.where(e_lo == 0, 0, jnp.where(e_lo == 1, 3, 5)) + e_hi - e_lo - 1
    bucket = g_i * PAIRS_PER_GROUP + pair
    return bucket, jnp.where(first_low, w1, w2), jnp.where(first_low, w2, w1)


def _moe_kernel(ea_ref, eb_ref, valid_ref, src_hbm, x_hbm, gate_ref, g_ref, wgu_a_ref, wgu_b_ref,
                wd_a_ref, wd_b_ref, gf_ref, out_hbm, idx_ref, xb0, xb1, ob0, ob1, isem, gsem, ssem, *, n_tiles,
                final_norm):
    n_grp, sub, d = xb0.shape
    tms = n_grp * sub
    i = pl.program_id(0)

    def idx_off(k):
        return lax.rem(k, MOE_IDX_SLOTS) * tms

    def idx_copy(k):
        return pltpu.make_async_copy(src_hbm.at[k], idx_ref.at[pl.ds(idx_off(k), tms)],
                                     isem.at[lax.rem(k, MOE_IDX_SLOTS)])

    def gather_row(tok, xb, s, j, u):
        return pltpu.make_async_copy(x_hbm.at[pl.ds(tok, 1)], xb.at[j, pl.ds(u, 1)], gsem.at[s])

    def scatter_row(tok, ob, s, j, u):
        return pltpu.make_async_copy(ob.at[j, pl.ds(u, 1)], out_hbm.at[pl.ds(tok, 1)], ssem.at[s])

    def start_gather(k, xb, s):
        base = idx_off(k)
        for r in range(tms):
            gather_row(idx_ref[base + r], xb, s, r // sub, r % sub).start(priority=r % 2)

    def wait_gather(xb, s):
        for r in range(tms):
            gather_row(0, xb, s, r // sub, r % sub).wait()

    def for_rows(n, fn):
        @pl.when(n == tms)
        def _():
            for r in range(tms):
                fn(r // sub, r % sub, r)

        @pl.when(n < tms)
        def _():
            full = lax.shift_right_logical(n, 3)

            def grp(j, carry):
                for u in range(sub):
                    fn(j, u, j * sub + u)
                return carry
            lax.fori_loop(0, full, grp, 0)

            def one(u, carry):
                fn(full, u, full * sub + u)
                return carry
            lax.fori_loop(0, lax.bitwise_and(n, sub - 1), one, 0)

    def start_scatter(k, n, ob, s):
        base = idx_off(k)

        def fn(j, u, r):
            prio = r % 2 if isinstance(r, int) else 0
            scatter_row(idx_ref[base + r], ob, s, j, u).start(priority=prio)
        for_rows(n, fn)

    def wait_scatter(n, ob, s):
        for_rows(n, lambda j, u, r: scatter_row(0, ob, s, j, u).wait())

    def tile(s):
        xb, ob = (xb0, ob0) if s == 0 else (xb1, ob1)
        xo, obo = (xb1, ob1) if s == 0 else (xb0, ob0)
        if s == 0:
            @pl.when(i == 0)
            def _():
                idx_copy(0).start()
                idx_copy(0).wait()
                start_gather(0, xb0, 0)
                idx_copy(1).start()

        @pl.when(i + 2 <= n_tiles)
        def _():
            idx_copy(i + 2).start()

        wait_gather(xb, s)

        @pl.when(i >= 2)
        def _():
            wait_scatter(valid_ref[jnp.maximum(i - 2, 0)], ob, s)

        idx_copy(i + 1).wait()
        n_valid = valid_ref[i]
        next_base = idx_off(i + 1)
        chunk = tms // MOE_GATHER_CHUNKS

        def prefetch_rows(c):
            for r in range(c * chunk, (c + 1) * chunk):
                gather_row(idx_ref[next_base + r], xo, 1 - s, r // sub, r % sub).start(priority=r % 2)

        @pl.when(n_valid > 0)
        def _():
            x = xb[...].reshape(tms, d)
            prefetch_rows(0)
            t = _rms(x, g_ref[...]).astype(BF16)
            gates = gate_ref[0].T
            prefetch_rows(1)
            y = x
            for e, (wgu_ref, wd_ref) in enumerate(((wgu_a_ref, wd_a_ref), (wgu_b_ref, wd_b_ref))):
                hgu = jnp.dot(t, wgu_ref[0], preferred_element_type=F32)
                prefetch_rows(2 + 3 * e)
                hg, hu = hgu[:, 0:D_EXPERT], hgu[:, D_EXPERT:2 * D_EXPERT]
                act = (hg * jax.nn.sigmoid(hg) * hu * gates[:, e:e + 1]).astype(BF16)
                prefetch_rows(3 + 3 * e)
                y = y + jnp.dot(act, wd_ref[0], preferred_element_type=F32)
                prefetch_rows(4 + 3 * e)
            if final_norm:
                y = _rms(y, gf_ref[...])
            ob[...] = y.reshape(n_grp, sub, d)

        @pl.when(n_valid <= 0)
        def _():
            for c in range(MOE_GATHER_CHUNKS):
                prefetch_rows(c)

        start_scatter(i, n_valid, ob, s)

        @pl.when(i == n_tiles - 1)
        def _():
            wait_gather(xo, 1 - s)
            if n_tiles > 1:
                wait_scatter(valid_ref[jnp.maximum(i - 1, 0)], obo, 1 - s)
            wait_scatter(n_valid, ob, s)

    parity = lax.rem(i, 2)
    pl.when(parity == 0)(lambda: tile(0))
    pl.when(parity == 1)(lambda: tile(1))


def _moe_plan(route, counts, n_tok):
    tms = MOE_TILE
    n_b = N_GROUPS * PAIRS_PER_GROUP
    n_tiles = n_tok // tms + n_b
    bucket = route[0].astype(jnp.int32)
    rank = route[1].astype(jnp.int32)
    cnt = counts[:n_b, 0].astype(jnp.int32)
    padded = (cnt + tms - 1) // tms * tms
    pad_end = jnp.cumsum(padded)
    pad_off = pad_end - padded
    dest = pad_off[bucket] + rank
    payload = jnp.stack([jnp.arange(n_tok, dtype=F32), route[2], route[3]], axis=1)
    plan = jnp.zeros(((n_tiles + 1) * tms, 3), F32).at[dest].set(payload, unique_indices=True)
    src = plan[:, 0].astype(jnp.int32).reshape(n_tiles + 1, tms)
    gates = jnp.pad(plan[:n_tiles * tms, 1:3].reshape(n_tiles, tms, 2).transpose(0, 2, 1),
                    ((0, 0), (0, SUBLANES - 2), (0, 0)))
    start = jnp.arange(n_tiles, dtype=jnp.int32) * tms
    tb = jnp.minimum(jnp.sum(pad_end[None, :] <= start[:, None], axis=1), n_b - 1).astype(jnp.int32)
    valid = jnp.clip(cnt[tb] - (start - pad_off[tb]), 0, tms).astype(jnp.int32)
    lo_hi = [(a, b) for a in range(EXPERTS_PER_GROUP) for b in range(a + 1, EXPERTS_PER_GROUP)]
    ea_tab = np.array([EXPERTS_PER_GROUP * g + a for g in range(N_GROUPS) for a, _ in lo_hi], np.int32)
    eb_tab = np.array([EXPERTS_PER_GROUP * g + b for g in range(N_GROUPS) for _, b in lo_hi], np.int32)
    return src, gates, jnp.asarray(ea_tab)[tb], jnp.asarray(eb_tab)[tb], valid


def _moe(x2d, route, counts, g, wgu, wd, gf, final_norm):
    n_tok, d = x2d.shape
    tms = MOE_TILE
    src, gates, ea, eb, valid = _moe_plan(route, counts, n_tok)
    n_tiles = src.shape[0] - 1
    any_spec = pl.BlockSpec(memory_space=pl.ANY)
    grid_spec = pltpu.PrefetchScalarGridSpec(
        num_scalar_prefetch=3,
        grid=(n_tiles,),
        in_specs=[any_spec, any_spec,
                  pl.BlockSpec((1, SUBLANES, tms), lambda i, ea_r, eb_r, v_r: (i, 0, 0)),
                  pl.BlockSpec(g.shape, lambda i, ea_r, eb_r, v_r: (0, 0)),
                  pl.BlockSpec((1, d, 2 * D_EXPERT), lambda i, ea_r, eb_r, v_r: (ea_r[i], 0, 0)),
                  pl.BlockSpec((1, d, 2 * D_EXPERT), lambda i, ea_r, eb_r, v_r: (eb_r[i], 0, 0)),
                  pl.BlockSpec((1, D_EXPERT, d), lambda i, ea_r, eb_r, v_r: (ea_r[i], 0, 0)),
                  pl.BlockSpec((1, D_EXPERT, d), lambda i, ea_r, eb_r, v_r: (eb_r[i], 0, 0)),
                  pl.BlockSpec(gf.shape, lambda i, ea_r, eb_r, v_r: (0, 0))],
        out_specs=any_spec,
        scratch_shapes=[pltpu.SMEM((MOE_IDX_SLOTS * tms,), jnp.int32)]
        + [pltpu.VMEM((tms // SUBLANES, SUBLANES, d), F32)] * 4
        + [pltpu.SemaphoreType.DMA((MOE_IDX_SLOTS,)), pltpu.SemaphoreType.DMA((2,)),
           pltpu.SemaphoreType.DMA((2,))],
    )
    return pl.pallas_call(
        functools.partial(_moe_kernel, n_tiles=n_tiles, final_norm=final_norm),
        out_shape=jax.ShapeDtypeStruct((n_tok, d), F32),
        grid_spec=grid_spec,
        compiler_params=_cparams(("arbitrary",)),
        name="moe",
    )(ea, eb, valid, src, x2d, gates, g, wgu, wgu, wd, wd, gf)


def _row(v):
    return v.reshape(1, -1).astype(F32)


def _rope_tables(seq_len, rot_dim, lane_starts):
    inv = ROPE_THETA ** (-jnp.arange(0, rot_dim, 2, dtype=F32) / rot_dim)
    ang = jnp.arange(seq_len, dtype=F32)[:, None] * inv[None, :]
    cos, sin = jnp.cos(ang), jnp.sin(ang)
    c_parts, s_parts, pos = [], [], 0
    for l0 in tuple(lane_starts) + (LANES,):
        if l0 > pos:
            c_parts.append(jnp.ones((seq_len, l0 - pos), F32))
            s_parts.append(jnp.zeros((seq_len, l0 - pos), F32))
        if l0 < LANES:
            c_parts += [cos, cos]
            s_parts += [-sin, sin]
        pos = l0 + rot_dim
    return jnp.concatenate(c_parts, axis=1), jnp.concatenate(s_parts, axis=1)


def _prep_even(w_in, w_uq, w_ukv, w_o):
    c = np.cumsum([0, MLA_Q_RANK, MLA_KV_RANK, MLA_ROPE, DIFF_QK, DIFF_QK, DIFF_VW])
    kpe = jnp.zeros((D_MODEL, LANES), F32).at[:, MLA_NOPE:MLA_NOPE + MLA_ROPE].set(w_in[:, c[2]:c[3]])
    win = jnp.concatenate([w_in[:, c[0]:c[2]], w_in[:, c[3]:c[6]], kpe], axis=1).astype(BF16)
    uq = w_uq.reshape(MLA_Q_RANK, MLA_HEADS, MLA_NOPE + MLA_ROPE)
    uq = jnp.pad(uq, ((0, 0), (0, 0), (0, LANES - MLA_NOPE - MLA_ROPE))).reshape(MLA_Q_RANK, MLA_HEADS * LANES)
    ukv = w_ukv.reshape(MLA_KV_RANK, MLA_HEADS, MLA_NOPE + MLA_V)
    uk = jnp.pad(ukv[:, :, :MLA_NOPE], ((0, 0), (0, 0), (0, LANES - MLA_NOPE))).reshape(MLA_KV_RANK,
                                                                                       MLA_HEADS * LANES)
    uv = ukv[:, :, MLA_NOPE:].reshape(MLA_KV_RANK, MLA_HEADS * MLA_V)
    n_mla = MLA_HEADS * MLA_V
    return (win, uq.astype(BF16), uk.astype(BF16), uv.astype(BF16),
            w_o[:n_mla].astype(BF16), w_o[n_mla:].astype(BF16))


def _prep_moe(w_rg, b_rg, w_re, b_re, w_gate, w_up, w_down):
    e0 = ROUTE_EXPERT_ROW0
    wr = jnp.zeros((ROUTE_ROWS, D_MODEL), F32).at[:N_GROUPS].set(w_rg.T).at[e0:e0 + N_EXPERTS].set(w_re.T)
    br = jnp.zeros((ROUTE_ROWS, 1), F32).at[:N_GROUPS, 0].set(b_rg).at[e0:e0 + N_EXPERTS, 0].set(b_re)
    wr_hi = wr.astype(BF16)
    wr_lo = (wr - wr_hi.astype(F32)).astype(BF16)
    wgu = jnp.concatenate([w_gate, w_up], axis=-1).reshape(N_EXPERTS, D_MODEL, 2 * D_EXPERT).astype(BF16)
    wd = w_down.reshape(N_EXPERTS, D_EXPERT, D_MODEL).astype(BF16)
    return wr_hi, wr_lo, br, wgu, wd


def _trunk(x, mem, p, tables):
    b, s, d = x.shape
    depth = p['ln_mix'].shape[0]
    for layer in range(depth):
        i = layer // 2
        if layer % 2 == 0:
            win, uq, uk, uv, wo_a, wo_b = _prep_even(p['w_in_even'][i], p['mla_w_uq'][i], p['mla_w_ukv'][i],
                                                     p['w_o_even'][i])
            cm, sm, cd, sd = tables[s]
            q, k, v, dq, dk, dv = _even_in(x, _row(p['ln_mix'][layer]), win, _row(p['mla_q_norm'][i]),
                                           _row(p['mla_kv_norm'][i]), uq, uk, uv, cm, sm, cd, sd)
            o_mla = _flash(q, k, v, 2 * LANES)
            lamp = jnp.zeros((8, LANES), F32)
            for r, name in enumerate(('diff_lambda_q1', 'diff_lambda_k1', 'diff_lambda_q2', 'diff_lambda_k2')):
                lamp = lamp.at[r, :DIFF_HEAD_DIM].set(p[name][i])
            lam_init = 0.8 - 0.6 * math.exp(-0.3 * layer)
            o_diff = _flash(dq, dk, dv, LANES, lamp=lamp, subln=_row(p['diff_subln'][i]), lam_init=lam_init)
            a_list, w_list = [o_mla, o_diff], [wo_a, wo_b]
        else:
            q, k, v = _odd_in(x, _row(p['ln_mix'][layer]), p['w_in_odd'][i].astype(BF16))
            tbl, idx = _na_tables(p['na_rpb'][i])
            a_list, w_list = [_na_attention(q, k, v, tbl, idx)], [p['w_o_odd'][i].astype(BF16)]
        mk, mv = _mem_kv(mem, _row(p['ln_mem'][layer]), p['xa_w_kv'][layer].astype(BF16))
        wr_hi, wr_lo, br, wgu, wd = _prep_moe(p['moe_w_group'][layer], p['moe_b_group'][layer],
                                              p['moe_w_expert'][layer], p['moe_b_expert'][layer],
                                              p['moe_w_gate'][layer], p['moe_w_up'][layer], p['moe_w_down'][layer])
        x, route, counts = _post_mix(x, a_list, w_list, _row(p['ln_xattn'][layer]),
                                     p['xa_w_q'][layer].astype(BF16), mk, mv, p['xa_w_o'][layer].astype(BF16),
                                     _row(p['ln_ffn'][layer]), wr_hi, wr_lo, br)
        x = _moe(x.reshape(b * s, d), route, counts, _row(p['ln_ffn'][layer]), wgu, wd,
                 _row(p['ln_final']), final_norm=(layer == depth - 1)).reshape(b, s, d)
    return x


def kernel(x_prompt, x_sample, mem_prompt, mem_sample, ln_mix, w_in_even, mla_q_norm, mla_kv_norm, mla_w_uq, mla_w_ukv, diff_lambda_q1, diff_lambda_k1, diff_lambda_q2, diff_lambda_k2, diff_subln, w_o_even, w_in_odd, na_rpb, w_o_odd, ln_xattn, ln_mem, xa_w_q, xa_w_kv, xa_w_o, ln_ffn, moe_w_group, moe_b_group, moe_w_expert, moe_b_expert, moe_w_gate, moe_w_up, moe_w_down, ln_final):
    p = dict(ln_mix=ln_mix, w_in_even=w_in_even, mla_q_norm=mla_q_norm, mla_kv_norm=mla_kv_norm,
             mla_w_uq=mla_w_uq, mla_w_ukv=mla_w_ukv, diff_lambda_q1=diff_lambda_q1,
             diff_lambda_k1=diff_lambda_k1, diff_lambda_q2=diff_lambda_q2, diff_lambda_k2=diff_lambda_k2,
             diff_subln=diff_subln, w_o_even=w_o_even, w_in_odd=w_in_odd, na_rpb=na_rpb, w_o_odd=w_o_odd,
             ln_xattn=ln_xattn, ln_mem=ln_mem, xa_w_q=xa_w_q, xa_w_kv=xa_w_kv, xa_w_o=xa_w_o,
             ln_ffn=ln_ffn, moe_w_group=moe_w_group, moe_b_group=moe_b_group, moe_w_expert=moe_w_expert,
             moe_b_expert=moe_b_expert, moe_w_gate=moe_w_gate, moe_w_up=moe_w_up, moe_w_down=moe_w_down,
             ln_final=ln_final)
    s_max = max(x_prompt.shape[1], x_sample.shape[1])
    cm, sm = _rope_tables(s_max, MLA_ROPE, (MLA_NOPE,))
    cd, sd = _rope_tables(s_max, DIFF_ROT, (0, DIFF_HEAD_DIM))
    tables = {s: (cm, sm, cd, sd) for s in {x_prompt.shape[1], x_sample.shape[1]}}
    return (_trunk(x_prompt, mem_prompt, p, tables), _trunk(x_sample, mem_sample, p, tables))
```

```python
import functools
import math

import numpy as np
import jax
import jax.numpy as jnp
from jax import lax
from jax.experimental import pallas as pl
from jax.experimental.pallas import tpu as pltpu

F32 = jnp.float32
BF16 = jnp.bfloat16

D_MODEL = 1024
ROPE_THETA = 500000.0
NORM_EPS = 1e-6
NEG_INF = -1e30
LOG2E = math.log2(math.e)

MLA_HEADS = 8
MLA_Q_RANK = 384
MLA_KV_RANK = 256
MLA_NOPE = 64
MLA_ROPE = 32
MLA_V = 64
DIFF_HEADS = 4
DIFF_HEAD_DIM = 64
DIFF_ROT = DIFF_HEAD_DIM // 4
DIFF_QK = DIFF_HEADS * 2 * DIFF_HEAD_DIM
DIFF_VW = DIFF_HEADS * 2 * DIFF_HEAD_DIM
GRID_W = 64
NA_HEADS = 16
NA_HEAD_DIM = 64
WIN_ROWS = 8
WIN_COLS = 16
MEM_TOKENS = 256
XA_HEADS = 4
XA_HEAD_DIM = 128
XA_INNER = XA_HEADS * XA_HEAD_DIM
N_GROUPS = 4
EXPERTS_PER_GROUP = 4
N_EXPERTS = N_GROUPS * EXPERTS_PER_GROUP
PAIRS_PER_GROUP = EXPERTS_PER_GROUP * (EXPERTS_PER_GROUP - 1) // 2
ROUTE_ROWS = 32
ROUTE_EXPERT_ROW0 = 8
D_EXPERT = 256

LANES = 128
SUBLANES = 8
VMEM_LIMIT = 56 * 1024 * 1024

TOKEN_TILE = 512
MOE_TILE = 256
MOE_TILE_LARGE = 512
MOE_GATHER_CHUNKS = 8
MOE_IDX_SLOTS = 4
FLASH_TQ = 1024
FLASH_TK = 512
FLASH_BOUND_SLACK = 1.01
FLASH_MIN_ROW_SUM = 2.0 ** -60
FLASH_UNROLL = 8
NA_ROW_BLOCK = 16
NA_KEY_ROWS = NA_ROW_BLOCK + WIN_ROWS
NA_GROUP_ROWS = 4
NA_GROUP_KEY_ROWS = NA_GROUP_ROWS + WIN_ROWS
NA_TILE_CODES = 3
NA_TILES_PER_HEAD = NA_TILE_CODES * 16 + 1


def _cparams(sem):
    return pltpu.CompilerParams(dimension_semantics=sem, vmem_limit_bytes=VMEM_LIMIT)


def _rms(xf, g):
    return xf * lax.rsqrt(jnp.mean(xf * xf, axis=-1, keepdims=True) + NORM_EPS) * g


def _full(shape):
    nd = len(shape)
    return pl.BlockSpec(shape, lambda *_: (0,) * nd)


def _rope_block(x, cos, sin, half):
    lane = lax.broadcasted_iota(jnp.int32, x.shape, 1)
    up = pltpu.roll(x, LANES - half, 1)
    down = pltpu.roll(x, half, 1)
    first = lax.bitwise_and(lane, 2 * half - 1) < half
    return x * cos + jnp.where(first, up, down) * sin


def _even_in_kernel(x_ref, g_ref, win_ref, qn_ref, kvn_ref, wuq_ref, wuk_ref, wuv_ref,
                    cm_ref, sm_ref, cd_ref, sd_ref,
                    q_ref, k_ref, v_ref, dq_ref, dk_ref, dv_ref):
    h = _rms(x_ref[0], g_ref[...]).astype(BF16)
    proj = jnp.dot(h, win_ref[...], preferred_element_type=F32)
    o_cq, o_ckv, o_dq, o_dk, o_dv, o_kpe = 0, 384, 640, 1152, 1664, 2176
    cm, sm, cd, sd = cm_ref[...], sm_ref[...], cd_ref[...], sd_ref[...]

    cq = _rms(proj[:, o_cq:o_cq + MLA_Q_RANK], qn_ref[...]).astype(BF16)
    q = jnp.dot(cq, wuq_ref[...], preferred_element_type=F32)
    mla_c = (MLA_NOPE + MLA_ROPE) ** -0.5 * LOG2E
    for hd in range(MLA_HEADS):
        sl = slice(hd * LANES, (hd + 1) * LANES)
        q_ref[0, :, sl] = (_rope_block(q[:, sl], cm, sm, MLA_ROPE // 2) * mla_c).astype(BF16)

    ckv = _rms(proj[:, o_ckv:o_ckv + MLA_KV_RANK], kvn_ref[...]).astype(BF16)
    kn = jnp.dot(ckv, wuk_ref[...], preferred_element_type=F32)
    kpe = _rope_block(proj[:, o_kpe:o_kpe + LANES], cm, sm, MLA_ROPE // 2)
    for hd in range(MLA_HEADS):
        sl = slice(hd * LANES, (hd + 1) * LANES)
        k_ref[0, :, sl] = (kn[:, sl] + kpe).astype(BF16)
    v_ref[0] = jnp.dot(ckv, wuv_ref[...], preferred_element_type=F32).astype(BF16)

    diff_c = DIFF_HEAD_DIM ** -0.5 * LOG2E
    for hd in range(DIFF_HEADS):
        sl = slice(hd * LANES, (hd + 1) * LANES)
        dq_ref[0, :, sl] = (_rope_block(proj[:, o_dq + hd * LANES:o_dq + (hd + 1) * LANES], cd, sd,
                                        DIFF_ROT // 2) * diff_c).astype(BF16)
        dk_ref[0, :, sl] = _rope_block(proj[:, o_dk + hd * LANES:o_dk + (hd + 1) * LANES], cd, sd,
                                       DIFF_ROT // 2).astype(BF16)
    dv_ref[0] = proj[:, o_dv:o_dv + DIFF_VW].astype(BF16)


def _even_in(x, g, win, qn, kvn, wuq, wuk, wuv, cm, sm, cd, sd):
    b, s, d = x.shape
    tm = TOKEN_TILE
    tok = lambda w: pl.BlockSpec((1, tm, w), lambda i, j: (i, j, 0))
    tab = pl.BlockSpec((tm, LANES), lambda i, j: (j, 0))
    outs = [jax.ShapeDtypeStruct((b, s, w), BF16) for w in (1024, 1024, 512, 512, 512, 512)]
    return pl.pallas_call(
        _even_in_kernel,
        out_shape=outs,
        grid=(b, s // tm),
        in_specs=[tok(d), _full(g.shape), _full(win.shape), _full(qn.shape), _full(kvn.shape),
                  _full(wuq.shape), _full(wuk.shape), _full(wuv.shape), tab, tab, tab, tab],
        out_specs=[tok(1024), tok(1024), tok(512), tok(512), tok(512), tok(512)],
        compiler_params=_cparams(("parallel", "parallel")),
        name="even_in",
    )(x, g, win, qn, kvn, wuq, wuk, wuv, cm, sm, cd, sd)


def _flash_kernel(*refs, tq, tk, n_k, width, diff, lam_init):
    if diff:
        q_ref, k_ref, v_ref, lamp_ref, g_ref, o_ref, qs_ref, m_ref, l_ref, acc_ref, kn_ref, kmax_ref = refs
    else:
        q_ref, k_ref, v_ref, o_ref, qs_ref, m_ref, l_ref, acc_ref, kn_ref, kmax_ref = refs
    half = width // 2
    n_j = tk // LANES
    q = q_ref[0]
    lane = lax.broadcasted_iota(jnp.int32, (tq, width), 1)
    zero = jnp.zeros_like(q)
    qs_ref[0:tq, :] = jnp.where(lane < half, q, zero)
    qs_ref[tq:2 * tq, :] = jnp.where(lane >= half, q, zero)

    @pl.when(pl.program_id(2) == 0)
    def _():
        r_i = lax.broadcasted_iota(jnp.int32, (width, LANES), 0)
        c_i = lax.broadcasted_iota(jnp.int32, (width, LANES), 1)
        sel = jnp.where(((c_i == 0) & (r_i < half)) | ((c_i == 1) & (r_i >= half)), 1.0, 0.0).astype(BF16)
        kn_ref[...] = jnp.zeros(kn_ref.shape, F32)

        def knorm(i, carry):
            kc = k_ref[0, pl.ds(pl.multiple_of(i * tk, tk), tk), :].astype(F32)
            kn_ref[...] = jnp.maximum(kn_ref[...], jnp.dot((kc * kc).astype(BF16), sel, preferred_element_type=F32))
            return carry

        lax.fori_loop(0, n_k, knorm, 0)
        kmax = jnp.sqrt(jnp.max(kn_ref[...], axis=0, keepdims=True))
        kmax_ref[0:1, :] = jnp.broadcast_to(kmax[:, 0:1], (1, LANES))
        kmax_ref[1:2, :] = jnp.broadcast_to(kmax[:, 1:2], (1, LANES))

    def reset(m):
        m_ref[...] = m
        l_ref[...] = jnp.zeros(l_ref.shape, F32)
        acc_ref[...] = jnp.zeros(acc_ref.shape, F32)

    def scores(i):
        off = pl.multiple_of(i * tk, tk)
        kc = k_ref[0, pl.ds(off, tk), :]
        return off, lax.dot_general(qs_ref[...], kc, (((1,), (1,)), ((), ())), preferred_element_type=F32)

    def max_pass(i, carry):
        _, s = scores(i)
        m = m_ref[...]
        for j in range(n_j):
            m = jnp.maximum(m, s[:, j * LANES:(j + 1) * LANES])
        m_ref[...] = m
        return carry

    def pv_pass(i, carry):
        off, s = scores(i)
        vc = v_ref[0, pl.ds(off, tk), :]
        m = m_ref[...]
        ps = [jnp.exp2(s[:, j * LANES:(j + 1) * LANES] - m) for j in range(n_j)]
        lsum = ps[0]
        for pj in ps[1:]:
            lsum = lsum + pj
        l_ref[...] += lsum
        p = jnp.concatenate([pj.astype(BF16) for pj in ps], axis=1)
        acc_ref[...] += jnp.dot(p, vc, preferred_element_type=F32)
        return carry

    qf = qs_ref[...].astype(F32)
    qn = jnp.sqrt(jnp.sum(qf * qf, axis=1, keepdims=True))
    bound_top = qn[0:tq] * kmax_ref[0:1, :] * FLASH_BOUND_SLACK
    bound_bot = qn[tq:2 * tq] * kmax_ref[1:2, :] * FLASH_BOUND_SLACK
    reset(jnp.concatenate([bound_top, bound_bot], axis=0))
    lax.fori_loop(0, n_k, pv_pass, 0, unroll=FLASH_UNROLL)
    l_min = jnp.min(jnp.sum(l_ref[...], axis=1, keepdims=True))

    @pl.when(jnp.logical_not(l_min >= FLASH_MIN_ROW_SUM))
    def _():
        reset(jnp.full(m_ref.shape, NEG_INF, F32))
        lax.fori_loop(0, n_k, max_pass, 0, unroll=FLASH_UNROLL)
        m_ref[...] = jnp.broadcast_to(jnp.max(m_ref[...], axis=1, keepdims=True), m_ref.shape)
        lax.fori_loop(0, n_k, pv_pass, 0, unroll=FLASH_UNROLL)

    o = acc_ref[...] / jnp.sum(l_ref[...], axis=1, keepdims=True)
    top, bot = o[0:tq], o[tq:2 * tq]
    if diff:
        lp = lamp_ref[...]
        lam = (jnp.exp(jnp.sum(lp[0:1] * lp[1:2], axis=1, keepdims=True))
               - jnp.exp(jnp.sum(lp[2:3] * lp[3:4], axis=1, keepdims=True)) + lam_init)
        od = top - lam * bot
        o_ref[0] = (_rms(od, g_ref[...]) * (1.0 - lam_init)).astype(BF16)
    else:
        lane_o = lax.broadcasted_iota(jnp.int32, (tq, LANES), 1)
        o_ref[0] = jnp.where(lane_o < LANES // 2, top, bot).astype(BF16)


def _flash(q, k, v, width, lamp=None, subln=None, lam_init=0.0):
    b, s, _ = q.shape
    n = q.shape[2] // width
    tq, tk = FLASH_TQ, FLASH_TK
    diff = lamp is not None
    kern = functools.partial(_flash_kernel, tq=tq, tk=tk, n_k=s // tk, width=width, diff=diff,
                             lam_init=lam_init)
    in_specs = [pl.BlockSpec((1, tq, width), lambda bi, p, qi: (bi, qi, p)),
                pl.BlockSpec((1, s, width), lambda bi, p, qi: (bi, 0, p)),
                pl.BlockSpec((1, s, LANES), lambda bi, p, qi: (bi, 0, p))]
    args = [q, k, v]
    if diff:
        in_specs += [_full(lamp.shape), _full(subln.shape)]
        args += [lamp, subln]
    return pl.pallas_call(
        kern,
        out_shape=jax.ShapeDtypeStruct((b, s, n * LANES), BF16),
        grid=(b, n, s // tq),
        in_specs=in_specs,
        out_specs=pl.BlockSpec((1, tq, LANES), lambda bi, p, qi: (bi, qi, p)),
        scratch_shapes=[pltpu.VMEM((2 * tq, width), BF16), pltpu.VMEM((2 * tq, LANES), F32),
                        pltpu.VMEM((2 * tq, LANES), F32), pltpu.VMEM((2 * tq, LANES), F32),
                        pltpu.VMEM((tk, LANES), F32), pltpu.VMEM((SUBLANES, LANES), F32)],
        compiler_params=_cparams(("parallel", "parallel", "arbitrary")),
        name="flash_diff" if diff else "flash_mla",
    )(*args)


def _odd_in_kernel(x_ref, g_ref, w_ref, q_ref, k_ref, v_ref):
    h = _rms(x_ref[0], g_ref[...]).astype(BF16)
    qkv = jnp.dot(h, w_ref[...], preferred_element_type=F32)
    c = NA_HEAD_DIM ** -0.5 * LOG2E
    q_ref[0] = (qkv[:, 0:1024] * c).astype(BF16)
    k_ref[0] = qkv[:, 1024:2048].astype(BF16)
    v_ref[0] = qkv[:, 2048:3072].astype(BF16)


def _odd_in(x, g, w):
    b, s, d = x.shape
    tm = TOKEN_TILE
    tok = pl.BlockSpec((1, tm, d), lambda i, j: (i, j, 0))
    return pl.pallas_call(
        _odd_in_kernel,
        out_shape=[jax.ShapeDtypeStruct((b, s, d), BF16)] * 3,
        grid=(b, s // tm),
        in_specs=[tok, _full(g.shape), _full(w.shape)],
        out_specs=[tok, tok, tok],
        compiler_params=_cparams(("parallel", "parallel")),
        name="odd_in",
    )(x, g, w)


def _na_kernel(idx_ref, q_ref, k_ref, v_ref, tbl_ref, o_ref, qs_ref, os_ref, *, rows, n_rb):
    nq = NA_ROW_BLOCK * GRID_W
    nk = NA_KEY_ROWS * GRID_W
    gq = NA_GROUP_ROWS * GRID_W
    n_kp = NA_KEY_ROWS // 2
    rb = pl.program_id(2)
    ks = jnp.clip(NA_ROW_BLOCK * rb - WIN_ROWS // 2, 0, rows - NA_KEY_ROWS)
    pat = jnp.where(rb == 0, 0, jnp.where(rb == n_rb - 1, 2, 1))
    off = pl.multiple_of(ks * GRID_W, (WIN_ROWS // 2) * GRID_W)
    q = q_ref[0]
    lane = lax.broadcasted_iota(jnp.int32, (nq, LANES), 1)
    zero = jnp.zeros_like(q)
    qs_ref[0:nq, :] = jnp.where(lane < LANES // 2, q, zero)
    qs_ref[nq:2 * nq, :] = jnp.where(lane >= LANES // 2, q, zero)

    def group(g, carry):
        r0 = pl.multiple_of(g * gq, gq)
        per_head = NA_ROW_BLOCK // NA_GROUP_ROWS
        hh = g // per_head
        i0 = (g % per_head) * NA_GROUP_ROWS
        w0 = jnp.clip(i0 + (pat - 1) * (WIN_ROWS // 2), 0, NA_KEY_ROWS - NA_GROUP_KEY_ROWS)
        goff = pl.multiple_of(off + w0 * GRID_W, (WIN_ROWS // 2) * GRID_W)
        kw = k_ref[0, pl.ds(goff, NA_GROUP_KEY_ROWS * GRID_W), :]
        vw = v_ref[0, pl.ds(goff, NA_GROUP_KEY_ROWS * GRID_W), :]
        s = lax.dot_general(qs_ref[pl.ds(r0, gq), :], kw, (((1,), (1,)), ((), ())), preferred_element_type=F32)
        bias_rows = []
        for ii in range(NA_GROUP_ROWS):
            base = pat * (NA_ROW_BLOCK * n_kp) + (i0 + ii) * n_kp + lax.shift_right_logical(w0, 1)
            bias_rows.append(jnp.concatenate(
                [tbl_ref[0, hh * NA_TILES_PER_HEAD + idx_ref[base + kp]]
                 for kp in range(NA_GROUP_KEY_ROWS // 2)], axis=1))
        s = s + jnp.concatenate(bias_rows, axis=0)
        m = jnp.max(s, axis=1, keepdims=True)
        p = jnp.exp2(s - m)
        linv = 1.0 / jnp.sum(p, axis=1, keepdims=True)
        os_ref[pl.ds(r0, gq), :] = jnp.dot(p.astype(BF16), vw, preferred_element_type=F32) * linv
        return carry

    lax.fori_loop(0, 2 * nq // gq, group, 0, unroll=True)
    o_ref[0] = jnp.where(lane < LANES // 2, os_ref[0:nq, :], os_ref[nq:2 * nq, :]).astype(BF16)


def _na_tables(rpb):
    col = np.arange(GRID_W)
    col_start = np.clip(col - WIN_COLS // 2, 0, GRID_W - WIN_COLS)
    col_valid = (col[None, :] >= col_start[:, None]) & (col[None, :] < col_start[:, None] + WIN_COLS)
    col_idx = np.clip(col[None, :] - col[:, None], -(WIN_COLS - 1), WIN_COLS - 1) + WIN_COLS - 1
    n_ri = 2 * WIN_ROWS - 1
    ri = np.zeros((NA_TILES_PER_HEAD, LANES), np.int32)
    ok = np.zeros((NA_TILES_PER_HEAD, LANES), bool)
    for code in range(NA_TILE_CODES):
        for a in range(-1, n_ri):
            t = code * 16 + a + 1
            ri[t, :GRID_W] = np.clip(a, 0, n_ri - 1)
            ri[t, GRID_W:] = np.clip(a + 1, 0, n_ri - 1)
            ok[t, :GRID_W] = code in (0, 1) and 0 <= a < n_ri
            ok[t, GRID_W:] = code in (0, 2) and 0 <= a + 1 < n_ri
    onehot = (col_idx.reshape(-1)[None, :] == np.arange(2 * WIN_COLS - 1)[:, None]).astype(np.float32)
    band = jnp.dot(rpb.reshape(NA_HEADS * n_ri, 2 * WIN_COLS - 1).astype(F32), jnp.asarray(onehot),
                   precision=lax.Precision.HIGHEST).reshape(NA_HEADS, n_ri, GRID_W, GRID_W)
    band = jnp.where(col_valid[None, None], band * LOG2E, NEG_INF)
    masked = jnp.full((NA_HEADS, GRID_W, GRID_W), NEG_INF, F32)
    tiles = []
    for t in range(NA_TILES_PER_HEAD):
        left = band[:, ri[t, 0]] if ok[t, 0] else masked
        right = band[:, ri[t, GRID_W]] if ok[t, GRID_W] else masked
        tiles.append(jnp.concatenate([left, right], axis=-1))
    tbl = jnp.stack(tiles, axis=1).reshape(NA_HEADS // 2, 2 * NA_TILES_PER_HEAD, GRID_W, LANES)

    idx = np.zeros((3, NA_ROW_BLOCK, NA_KEY_ROWS // 2), np.int32)
    for pat in range(3):
        for i in range(NA_ROW_BLOCK):
            rel = (i, i + WIN_ROWS // 2, i + WIN_ROWS)[pat]
            start = (max(i - WIN_ROWS // 2, 0), i, min(i + WIN_ROWS // 2, NA_KEY_ROWS - WIN_ROWS))[pat]
            for kp in range(NA_KEY_ROWS // 2):
                kk = 2 * kp
                a = kk - rel + WIN_ROWS - 1
                v0 = start <= kk < start + WIN_ROWS
                v1 = start <= kk + 1 < start + WIN_ROWS
                if v0:
                    assert 0 <= a < n_ri
                if v1:
                    assert 0 <= a + 1 < n_ri
                if v0 and v1:
                    idx[pat, i, kp] = a + 1
                elif v0:
                    idx[pat, i, kp] = 16 + a + 1
                elif v1:
                    idx[pat, i, kp] = 32 + a + 1
                else:
                    idx[pat, i, kp] = NA_TILES_PER_HEAD - 1
    return tbl, jnp.asarray(idx.reshape(-1))


def _na_attention(q, k, v, tbl, idx):
    b, s, d = q.shape
    rows = s // GRID_W
    n_rb = rows // NA_ROW_BLOCK
    nq = NA_ROW_BLOCK * GRID_W
    nk = NA_KEY_ROWS * GRID_W
    kern = functools.partial(_na_kernel, rows=rows, n_rb=n_rb)
    grid_spec = pltpu.PrefetchScalarGridSpec(
        num_scalar_prefetch=1,
        grid=(NA_HEADS // 2, b, n_rb),
        in_specs=[pl.BlockSpec((1, nq, LANES), lambda p, bi, r, idx_r: (bi, r, p)),
                  pl.BlockSpec((1, s, LANES), lambda p, bi, r, idx_r: (bi, 0, p)),
                  pl.BlockSpec((1, s, LANES), lambda p, bi, r, idx_r: (bi, 0, p)),
                  pl.BlockSpec((1, 2 * NA_TILES_PER_HEAD, GRID_W, LANES), lambda p, bi, r, idx_r: (p, 0, 0, 0))],
        out_specs=pl.BlockSpec((1, nq, LANES), lambda p, bi, r, idx_r: (bi, r, p)),
        scratch_shapes=[pltpu.VMEM((2 * nq, LANES), BF16), pltpu.VMEM((2 * nq, LANES), F32)],
    )
    return pl.pallas_call(
        kern,
        out_shape=jax.ShapeDtypeStruct((b, s, d), BF16),
        grid_spec=grid_spec,
        compiler_params=_cparams(("parallel", "parallel", "arbitrary")),
        name="na_attention",
    )(idx, q, k, v, tbl)


def _mem_kv_kernel(m_ref, g_ref, w_ref, k_ref, v_ref):
    h = _rms(m_ref[0], g_ref[...]).astype(BF16)
    kv = jnp.dot(h, w_ref[...], preferred_element_type=F32)
    k_ref[0] = kv[:, 0:XA_INNER].astype(BF16)
    v_ref[0] = kv[:, XA_INNER:2 * XA_INNER].astype(BF16)


def _mem_kv(mem, g, w):
    b, m, d = mem.shape
    return pl.pallas_call(
        _mem_kv_kernel,
        out_shape=[jax.ShapeDtypeStruct((b, m, XA_INNER), BF16)] * 2,
        grid=(b,),
        in_specs=[pl.BlockSpec((1, m, d), lambda i: (i, 0, 0)), _full(g.shape), _full(w.shape)],
        out_specs=[pl.BlockSpec((1, m, XA_INNER), lambda i: (i, 0, 0))] * 2,
        compiler_params=_cparams(("parallel",)),
        name="mem_kv",
    )(mem, g, w)


def _post_mix_kernel(*refs, n_a):
    x_ref = refs[0]
    a_refs = refs[1:1 + n_a]
    w_refs = refs[1 + n_a:1 + 2 * n_a]
    (g_ref, wq_ref, mk_ref, mv_ref, wo_ref, gffn_ref, wr_hi_ref, wr_lo_ref, br_ref, tri_ref,
     o_ref, route_ref, cnt_ref) = refs[1 + 2 * n_a:]

    @pl.when((pl.program_id(0) == 0) & (pl.program_id(1) == 0))
    def _():
        cnt_ref[...] = jnp.zeros(cnt_ref.shape, F32)

    x1 = x_ref[0]
    for a_ref, w_ref in zip(a_refs, w_refs):
        x1 = x1 + jnp.dot(a_ref[0], w_ref[...], preferred_element_type=F32)
    h = _rms(x1, g_ref[...]).astype(BF16)
    c = XA_HEAD_DIM ** -0.5 * LOG2E
    q = (jnp.dot(h, wq_ref[...], preferred_element_type=F32) * c).astype(BF16)
    outs = []
    for hd in range(XA_HEADS):
        sl = slice(hd * XA_HEAD_DIM, (hd + 1) * XA_HEAD_DIM)
        s = lax.dot_general(q[:, sl], mk_ref[0, :, sl], (((1,), (1,)), ((), ())), preferred_element_type=F32)
        p = jnp.exp2(s - jnp.max(s, axis=1, keepdims=True))
        linv = 1.0 / jnp.sum(p, axis=1, keepdims=True)
        o = jnp.dot(p.astype(BF16), mv_ref[0, :, sl], preferred_element_type=F32) * linv
        outs.append(o.astype(BF16))
    o_all = jnp.concatenate(outs, axis=1)
    x2 = x1 + jnp.dot(o_all, wo_ref[...], preferred_element_type=F32)
    o_ref[0] = x2

    bucket, w_a, w_b = _route(_rms(x2, gffn_ref[...]), wr_hi_ref, wr_lo_ref, br_ref)
    n = x2.shape[0]
    row = lax.broadcasted_iota(jnp.int32, (ROUTE_ROWS, n), 0)
    onehot = row == bucket
    oh = jnp.where(onehot, 1.0, 0.0)
    before = jnp.dot(oh.astype(BF16), tri_ref[...], preferred_element_type=F32) + cnt_ref[:, 0:1]
    rank = jnp.sum(jnp.where(onehot, before, 0.0), axis=0, keepdims=True)
    cnt_ref[...] += jnp.broadcast_to(jnp.sum(oh, axis=1, keepdims=True), cnt_ref.shape)
    row8 = lax.broadcasted_iota(jnp.int32, (SUBLANES, n), 0)
    route_ref[...] = (jnp.where(row8 == 0, bucket.astype(F32), 0.0) + jnp.where(row8 == 1, rank, 0.0)
                      + jnp.where(row8 == 2, w_a, 0.0) + jnp.where(row8 == 3, w_b, 0.0))


def _post_mix(x, a_list, w_list, g, wq, mk, mv, wo, gffn, wr_hi, wr_lo, br):
    b, s, d = x.shape
    tm = TOKEN_TILE
    n_a = len(a_list)
    nt = s // tm
    tok = lambda w: pl.BlockSpec((1, tm, w), lambda i, j: (i, j, 0))
    memspec = pl.BlockSpec((1, MEM_TOKENS, XA_INNER), lambda i, j: (i, 0, 0))
    tri = jnp.triu(jnp.ones((tm, tm), F32), 1).astype(BF16)
    return pl.pallas_call(
        functools.partial(_post_mix_kernel, n_a=n_a),
        out_shape=[jax.ShapeDtypeStruct((b, s, d), F32), jax.ShapeDtypeStruct((SUBLANES, b * s), F32),
                   jax.ShapeDtypeStruct((ROUTE_ROWS, LANES), F32)],
        grid=(b, nt),
        in_specs=([tok(d)] + [tok(a.shape[2]) for a in a_list] + [_full(w.shape) for w in w_list]
                  + [_full(g.shape), _full(wq.shape), memspec, memspec, _full(wo.shape), _full(gffn.shape),
                     _full(wr_hi.shape), _full(wr_lo.shape), _full(br.shape), _full(tri.shape)]),
        out_specs=[tok(d), pl.BlockSpec((SUBLANES, tm), lambda i, j: (0, i * nt + j)),
                   _full((ROUTE_ROWS, LANES))],
        compiler_params=_cparams(("arbitrary", "arbitrary")),
        name="post_mix",
    )(x, *a_list, *w_list, g, wq, mk, mv, wo, gffn, wr_hi, wr_lo, br, tri)


def _route(tf, wr_hi_ref, wr_lo_ref, br_ref):
    t_hi = tf.astype(BF16)
    t_lo = (tf - t_hi.astype(F32)).astype(BF16)
    w_hi, w_lo = wr_hi_ref[...], wr_lo_ref[...]
    nt = (((1,), (1,)), ((), ()))
    logits = (lax.dot_general(w_hi, t_hi, nt, preferred_element_type=F32)
              + lax.dot_general(w_hi, t_lo, nt, preferred_element_type=F32)
              + lax.dot_general(w_lo, t_hi, nt, preferred_element_type=F32)) + br_ref[...]
    row = lax.broadcasted_iota(jnp.int32, logits.shape, 0)
    big = jnp.int32(ROUTE_ROWS)
    is_g = row < N_GROUPS
    lg = jnp.where(is_g, logits, NEG_INF)
    gmax = jnp.max(lg, axis=0, keepdims=True)
    g_p = 1.0 / jnp.sum(jnp.where(is_g, jnp.exp(lg - gmax), 0.0), axis=0, keepdims=True)
    g_i = jnp.min(jnp.where(lg == gmax, row, big), axis=0, keepdims=True)
    lo = ROUTE_EXPERT_ROW0 + EXPERTS_PER_GROUP * g_i
    in_grp = (row >= lo) & (row < lo + EXPERTS_PER_GROUP)
    el = jnp.where(in_grp, logits, NEG_INF)
    v1 = jnp.max(el, axis=0, keepdims=True)
    i1 = jnp.min(jnp.where(el == v1, row, big), axis=0, keepdims=True)
    el2 = jnp.where(row == i1, NEG_INF, el)
    v2 = jnp.max(el2, axis=0, keepdims=True)
    i2 = jnp.min(jnp.where(el2 == v2, row, big), axis=0, keepdims=True)
    e2 = jnp.exp(v2 - v1)
    w1 = g_p / (1.0 + e2)
    w2 = g_p * e2 / (1.0 + e2)
    first_low = i1 < i2
    e_lo = jnp.minimum(i1, i2) - lo
    e_hi = jnp.maximum(i1, i2) - lo
    pair = jnp.where(e_lo == 0, 0, jnp.where(e_lo == 1, 3, 5)) + e_hi - e_lo - 1
    bucket = g_i * PAIRS_PER_GROUP + pair
    return bucket, jnp.where(first_low, w1, w2), jnp.where(first_low, w2, w1)


def _moe_kernel(ea_ref, eb_ref, valid_ref, src_hbm, x_hbm, gate_ref, g_ref, wgu_a_ref, wgu_b_ref,
                wd_a_ref, wd_b_ref, gf_ref, out_hbm, idx_ref, xb0, xb1, ob0, ob1, isem, gsem, ssem, *, n_tiles,
                final_norm):
    n_grp, sub, d = xb0.shape
    tms = n_grp * sub
    i = pl.program_id(0)

    def idx_off(k):
        return lax.rem(k, MOE_IDX_SLOTS) * tms

    def idx_copy(k):
        return pltpu.make_async_copy(src_hbm.at[k], idx_ref.at[pl.ds(idx_off(k), tms)],
                                     isem.at[lax.rem(k, MOE_IDX_SLOTS)])

    def gather_row(tok, xb, s, j, u):
        return pltpu.make_async_copy(x_hbm.at[pl.ds(tok, 1)], xb.at[j, pl.ds(u, 1)], gsem.at[s])

    def scatter_row(tok, ob, s, j, u):
        return pltpu.make_async_copy(ob.at[j, pl.ds(u, 1)], out_hbm.at[pl.ds(tok, 1)], ssem.at[s])

    def start_gather(k, xb, s):
        base = idx_off(k)
        for r in range(tms):
            gather_row(idx_ref[base + r], xb, s, r // sub, r % sub).start(priority=r % 2)

    def wait_gather(xb, s):
        for r in range(tms):
            gather_row(0, xb, s, r // sub, r % sub).wait()

    def for_rows(n, fn):
        @pl.when(n == tms)
        def _():
            for r in range(tms):
                fn(r // sub, r % sub, r)

        @pl.when(n < tms)
        def _():
            full = lax.shift_right_logical(n, 3)

            def grp(j, carry):
                for u in range(sub):
                    fn(j, u, j * sub + u)
                return carry
            lax.fori_loop(0, full, grp, 0)

            def one(u, carry):
                fn(full, u, full * sub + u)
                return carry
            lax.fori_loop(0, lax.bitwise_and(n, sub - 1), one, 0)

    def start_scatter(k, n, ob, s):
        base = idx_off(k)

        def fn(j, u, r):
            prio = r % 2 if isinstance(r, int) else 0
            scatter_row(idx_ref[base + r], ob, s, j, u).start(priority=prio)
        for_rows(n, fn)

    def wait_scatter(n, ob, s):
        for_rows(n, lambda j, u, r: scatter_row(0, ob, s, j, u).wait())

    def tile(s):
        xb, ob = (xb0, ob0) if s == 0 else (xb1, ob1)
        xo, obo = (xb1, ob1) if s == 0 else (xb0, ob0)
        if s == 0:
            @pl.when(i == 0)
            def _():
                idx_copy(0).start()
                idx_copy(0).wait()
                start_gather(0, xb0, 0)
                idx_copy(1).start()

        @pl.when(i + 2 <= n_tiles)
        def _():
            idx_copy(i + 2).start()

        wait_gather(xb, s)

        @pl.when(i >= 2)
        def _():
            wait_scatter(valid_ref[jnp.maximum(i - 2, 0)], ob, s)

        idx_copy(i + 1).wait()
        n_valid = valid_ref[i]
        next_base = idx_off(i + 1)
        chunk = tms // MOE_GATHER_CHUNKS

        def prefetch_rows(c):
            for r in range(c * chunk, (c + 1) * chunk):
                gather_row(idx_ref[next_base + r], xo, 1 - s, r // sub, r % sub).start(priority=r % 2)

        @pl.when(n_valid > 0)
        def _():
            x = xb[...].reshape(tms, d)
            prefetch_rows(0)
            t = _rms(x, g_ref[...]).astype(BF16)
            gates = gate_ref[0].T
            prefetch_rows(1)
            y = x
            for e, (wgu_ref, wd_ref) in enumerate(((wgu_a_ref, wd_a_ref), (wgu_b_ref, wd_b_ref))):
                hgu = jnp.dot(t, wgu_ref[0], preferred_element_type=F32)
                prefetch_rows(2 + 3 * e)
                hg, hu = hgu[:, 0:D_EXPERT], hgu[:, D_EXPERT:2 * D_EXPERT]
                act = (hg * jax.nn.sigmoid(hg) * hu * gates[:, e:e + 1]).astype(BF16)
                prefetch_rows(3 + 3 * e)
                y = y + jnp.dot(act, wd_ref[0], preferred_element_type=F32)
                prefetch_rows(4 + 3 * e)
            if final_norm:
                y = _rms(y, gf_ref[...])
            ob[...] = y.reshape(n_grp, sub, d)

        @pl.when(n_valid <= 0)
        def _():
            for c in range(MOE_GATHER_CHUNKS):
                prefetch_rows(c)

        start_scatter(i, n_valid, ob, s)

        @pl.when(i == n_tiles - 1)
        def _():
            wait_gather(xo, 1 - s)
            if n_tiles > 1:
                wait_scatter(valid_ref[jnp.maximum(i - 1, 0)], obo, 1 - s)
            wait_scatter(n_valid, ob, s)

    parity = lax.rem(i, 2)
    pl.when(parity == 0)(lambda: tile(0))
    pl.when(parity == 1)(lambda: tile(1))


def _moe_tile(n_tok):
    per_bucket = n_tok // (N_GROUPS * PAIRS_PER_GROUP)
    return MOE_TILE_LARGE if per_bucket >= 4 * MOE_TILE_LARGE else MOE_TILE


def _moe_plan(route, counts, n_tok):
    tms = _moe_tile(n_tok)
    n_b = N_GROUPS * PAIRS_PER_GROUP
    n_tiles = n_tok // tms + n_b
    bucket = route[0].astype(jnp.int32)
    rank = route[1].astype(jnp.int32)
    cnt = counts[:n_b, 0].astype(jnp.int32)
    padded = (cnt + tms - 1) // tms * tms
    pad_end = jnp.cumsum(padded)
    pad_off = pad_end - padded
    dest = pad_off[bucket] + rank
    payload = jnp.stack([jnp.arange(n_tok, dtype=F32), route[2], route[3]], axis=1)
    plan = jnp.zeros(((n_tiles + 1) * tms, 3), F32).at[dest].set(payload, unique_indices=True)
    src = plan[:, 0].astype(jnp.int32).reshape(n_tiles + 1, tms)
    gates = jnp.pad(plan[:n_tiles * tms, 1:3].reshape(n_tiles, tms, 2).transpose(0, 2, 1),
                    ((0, 0), (0, SUBLANES - 2), (0, 0)))
    start = jnp.arange(n_tiles, dtype=jnp.int32) * tms
    tb = jnp.minimum(jnp.sum(pad_end[None, :] <= start[:, None], axis=1), n_b - 1).astype(jnp.int32)
    valid = jnp.clip(cnt[tb] - (start - pad_off[tb]), 0, tms).astype(jnp.int32)
    lo_hi = [(a, b) for a in range(EXPERTS_PER_GROUP) for b in range(a + 1, EXPERTS_PER_GROUP)]
    ea_tab = np.array([EXPERTS_PER_GROUP * g + a for g in range(N_GROUPS) for a, _ in lo_hi], np.int32)
    eb_tab = np.array([EXPERTS_PER_GROUP * g + b for g in range(N_GROUPS) for _, b in lo_hi], np.int32)
    return src, gates, jnp.asarray(ea_tab)[tb], jnp.asarray(eb_tab)[tb], valid


def _moe(x2d, route, counts, g, wgu, wd, gf, final_norm):
    n_tok, d = x2d.shape
    tms = _moe_tile(n_tok)
    src, gates, ea, eb, valid = _moe_plan(route, counts, n_tok)
    n_tiles = src.shape[0] - 1
    any_spec = pl.BlockSpec(memory_space=pl.ANY)
    grid_spec = pltpu.PrefetchScalarGridSpec(
        num_scalar_prefetch=3,
        grid=(n_tiles,),
        in_specs=[any_spec, any_spec,
                  pl.BlockSpec((1, SUBLANES, tms), lambda i, ea_r, eb_r, v_r: (i, 0, 0)),
                  pl.BlockSpec(g.shape, lambda i, ea_r, eb_r, v_r: (0, 0)),
                  pl.BlockSpec((1, d, 2 * D_EXPERT), lambda i, ea_r, eb_r, v_r: (ea_r[i], 0, 0)),
                  pl.BlockSpec((1, d, 2 * D_EXPERT), lambda i, ea_r, eb_r, v_r: (eb_r[i], 0, 0)),
                  pl.BlockSpec((1, D_EXPERT, d), lambda i, ea_r, eb_r, v_r: (ea_r[i], 0, 0)),
                  pl.BlockSpec((1, D_EXPERT, d), lambda i, ea_r, eb_r, v_r: (eb_r[i], 0, 0)),
                  pl.BlockSpec(gf.shape, lambda i, ea_r, eb_r, v_r: (0, 0))],
        out_specs=any_spec,
        scratch_shapes=[pltpu.SMEM((MOE_IDX_SLOTS * tms,), jnp.int32)]
        + [pltpu.VMEM((tms // SUBLANES, SUBLANES, d), F32)] * 4
        + [pltpu.SemaphoreType.DMA((MOE_IDX_SLOTS,)), pltpu.SemaphoreType.DMA((2,)),
           pltpu.SemaphoreType.DMA((2,))],
    )
    return pl.pallas_call(
        functools.partial(_moe_kernel, n_tiles=n_tiles, final_norm=final_norm),
        out_shape=jax.ShapeDtypeStruct((n_tok, d), F32),
        grid_spec=grid_spec,
        compiler_params=_cparams(("arbitrary",)),
        name="moe",
    )(ea, eb, valid, src, x2d, gates, g, wgu, wgu, wd, wd, gf)


def _row(v):
    return v.reshape(1, -1).astype(F32)


def _rope_tables(seq_len, rot_dim, lane_starts):
    inv = ROPE_THETA ** (-jnp.arange(0, rot_dim, 2, dtype=F32) / rot_dim)
    ang = jnp.arange(seq_len, dtype=F32)[:, None] * inv[None, :]
    cos, sin = jnp.cos(ang), jnp.sin(ang)
    c_parts, s_parts, pos = [], [], 0
    for l0 in tuple(lane_starts) + (LANES,):
        if l0 > pos:
            c_parts.append(jnp.ones((seq_len, l0 - pos), F32))
            s_parts.append(jnp.zeros((seq_len, l0 - pos), F32))
        if l0 < LANES:
            c_parts += [cos, cos]
            s_parts += [-sin, sin]
        pos = l0 + rot_dim
    return jnp.concatenate(c_parts, axis=1), jnp.concatenate(s_parts, axis=1)


def _prep_even(w_in, w_uq, w_ukv, w_o):
    c = np.cumsum([0, MLA_Q_RANK, MLA_KV_RANK, MLA_ROPE, DIFF_QK, DIFF_QK, DIFF_VW])
    kpe = jnp.zeros((D_MODEL, LANES), F32).at[:, MLA_NOPE:MLA_NOPE + MLA_ROPE].set(w_in[:, c[2]:c[3]])
    win = jnp.concatenate([w_in[:, c[0]:c[2]], w_in[:, c[3]:c[6]], kpe], axis=1).astype(BF16)
    uq = w_uq.reshape(MLA_Q_RANK, MLA_HEADS, MLA_NOPE + MLA_ROPE)
    uq = jnp.pad(uq, ((0, 0), (0, 0), (0, LANES - MLA_NOPE - MLA_ROPE))).reshape(MLA_Q_RANK, MLA_HEADS * LANES)
    ukv = w_ukv.reshape(MLA_KV_RANK, MLA_HEADS, MLA_NOPE + MLA_V)
    uk = jnp.pad(ukv[:, :, :MLA_NOPE], ((0, 0), (0, 0), (0, LANES - MLA_NOPE))).reshape(MLA_KV_RANK,
                                                                                       MLA_HEADS * LANES)
    uv = ukv[:, :, MLA_NOPE:].reshape(MLA_KV_RANK, MLA_HEADS * MLA_V)
    n_mla = MLA_HEADS * MLA_V
    return (win, uq.astype(BF16), uk.astype(BF16), uv.astype(BF16),
            w_o[:n_mla].astype(BF16), w_o[n_mla:].astype(BF16))


def _prep_moe(w_rg, b_rg, w_re, b_re, w_gate, w_up, w_down):
    e0 = ROUTE_EXPERT_ROW0
    wr = jnp.zeros((ROUTE_ROWS, D_MODEL), F32).at[:N_GROUPS].set(w_rg.T).at[e0:e0 + N_EXPERTS].set(w_re.T)
    br = jnp.zeros((ROUTE_ROWS, 1), F32).at[:N_GROUPS, 0].set(b_rg).at[e0:e0 + N_EXPERTS, 0].set(b_re)
    wr_hi = wr.astype(BF16)
    wr_lo = (wr - wr_hi.astype(F32)).astype(BF16)
    wgu = jnp.concatenate([w_gate, w_up], axis=-1).reshape(N_EXPERTS, D_MODEL, 2 * D_EXPERT).astype(BF16)
    wd = w_down.reshape(N_EXPERTS, D_EXPERT, D_MODEL).astype(BF16)
    return wr_hi, wr_lo, br, wgu, wd


def _trunk(x, mem, p, tables):
    b, s, d = x.shape
    depth = p['ln_mix'].shape[0]
    for layer in range(depth):
        i = layer // 2
        if layer % 2 == 0:
            win, uq, uk, uv, wo_a, wo_b = _prep_even(p['w_in_even'][i], p['mla_w_uq'][i], p['mla_w_ukv'][i],
                                                     p['w_o_even'][i])
            cm, sm, cd, sd = tables[s]
            q, k, v, dq, dk, dv = _even_in(x, _row(p['ln_mix'][layer]), win, _row(p['mla_q_norm'][i]),
                                           _row(p['mla_kv_norm'][i]), uq, uk, uv, cm, sm, cd, sd)
            o_mla = _flash(q, k, v, 2 * LANES)
            lamp = jnp.zeros((8, LANES), F32)
            for r, name in enumerate(('diff_lambda_q1', 'diff_lambda_k1', 'diff_lambda_q2', 'diff_lambda_k2')):
                lamp = lamp.at[r, :DIFF_HEAD_DIM].set(p[name][i])
            lam_init = 0.8 - 0.6 * math.exp(-0.3 * layer)
            o_diff = _flash(dq, dk, dv, LANES, lamp=lamp, subln=_row(p['diff_subln'][i]), lam_init=lam_init)
            a_list, w_list = [o_mla, o_diff], [wo_a, wo_b]
        else:
            q, k, v = _odd_in(x, _row(p['ln_mix'][layer]), p['w_in_odd'][i].astype(BF16))
            tbl, idx = _na_tables(p['na_rpb'][i])
            a_list, w_list = [_na_attention(q, k, v, tbl, idx)], [p['w_o_odd'][i].astype(BF16)]
        mk, mv = _mem_kv(mem, _row(p['ln_mem'][layer]), p['xa_w_kv'][layer].astype(BF16))
        wr_hi, wr_lo, br, wgu, wd = _prep_moe(p['moe_w_group'][layer], p['moe_b_group'][layer],
                                              p['moe_w_expert'][layer], p['moe_b_expert'][layer],
                                              p['moe_w_gate'][layer], p['moe_w_up'][layer], p['moe_w_down'][layer])
        x, route, counts = _post_mix(x, a_list, w_list, _row(p['ln_xattn'][layer]),
                                     p['xa_w_q'][layer].astype(BF16), mk, mv, p['xa_w_o'][layer].astype(BF16),
                                     _row(p['ln_ffn'][layer]), wr_hi, wr_lo, br)
        x = _moe(x.reshape(b * s, d), route, counts, _row(p['ln_ffn'][layer]), wgu, wd,
                 _row(p['ln_final']), final_norm=(layer == depth - 1)).reshape(b, s, d)
    return x


def kernel(x_prompt, x_sample, mem_prompt, mem_sample, ln_mix, w_in_even, mla_q_norm, mla_kv_norm, mla_w_uq, mla_w_ukv, diff_lambda_q1, diff_lambda_k1, diff_lambda_q2, diff_lambda_k2, diff_subln, w_o_even, w_in_odd, na_rpb, w_o_odd, ln_xattn, ln_mem, xa_w_q, xa_w_kv, xa_w_o, ln_ffn, moe_w_group, moe_b_group, moe_w_expert, moe_b_expert, moe_w_gate, moe_w_up, moe_w_down, ln_final):
    p = dict(ln_mix=ln_mix, w_in_even=w_in_even, mla_q_norm=mla_q_norm, mla_kv_norm=mla_kv_norm,
             mla_w_uq=mla_w_uq, mla_w_ukv=mla_w_ukv, diff_lambda_q1=diff_lambda_q1,
             diff_lambda_k1=diff_lambda_k1, diff_lambda_q2=diff_lambda_q2, diff_lambda_k2=diff_lambda_k2,
             diff_subln=diff_subln, w_o_even=w_o_even, w_in_odd=w_in_odd, na_rpb=na_rpb, w_o_odd=w_o_odd,
             ln_xattn=ln_xattn, ln_mem=ln_mem, xa_w_q=xa_w_q, xa_w_kv=xa_w_kv, xa_w_o=xa_w_o,
             ln_ffn=ln_ffn, moe_w_group=moe_w_group, moe_b_group=moe_b_group, moe_w_expert=moe_w_expert,
             moe_b_expert=moe_b_expert, moe_w_gate=moe_w_gate, moe_w_up=moe_w_up, moe_w_down=moe_w_down,
             ln_final=ln_final)
    s_max = max(x_prompt.shape[1], x_sample.shape[1])
    cm, sm = _rope_tables(s_max, MLA_ROPE, (MLA_NOPE,))
    cd, sd = _rope_tables(s_max, DIFF_ROT, (0, DIFF_HEAD_DIM))
    tables = {s: (cm, sm, cd, sd) for s in {x_prompt.shape[1], x_sample.shape[1]}}
    return (_trunk(x_prompt, mem_prompt, p, tables), _trunk(x_sample, mem_sample, p, tables))
```
